```python
import math
import jax
import jax.numpy as jnp
from jax import lax
import numpy as np

D_MODEL = 1024
BATCH = 4
SEQ = 4096
DEPTH = 2

N_EVEN = (DEPTH + 1) // 2
N_ODD = DEPTH // 2

CONV_CH = D_MODEL // 2
CONV_WIDTH = 31
ATT_HEADS = 8
HEAD_DIM = 64
ATT_WIDTH = ATT_HEADS * HEAD_DIM
DILATED = ((128, 1), (512, 4), (2048, 16))
IN_COLS = 2 * CONV_CH + 3 * ATT_WIDTH
MIX_WIDTH = CONV_CH + ATT_WIDTH
SSM_GROUP = 16
SSM_GROUPS = D_MODEL // SSM_GROUP
SSM_STATE = 64
N_EXPERTS = 16
EXPERT_HIDDEN = D_MODEL
CAPACITY_FACTOR = 2
EPS = 1e-6
NEG_INF = -1e30

kernel_name = 'hybrid_conv_dilattn_s5_ecmoe'

F32 = jnp.float32


def rmsnorm(x, g):
    xf = x.astype(F32)
    y = xf * lax.rsqrt(jnp.mean(xf * xf, axis=-1, keepdims=True) + EPS)
    return (y * g.astype(F32)).astype(x.dtype)


def layernorm(x, g, b):
    mu = jnp.mean(x, axis=-1, keepdims=True)
    var = jnp.mean(jnp.square(x - mu), axis=-1, keepdims=True)
    return (x - mu) * lax.rsqrt(var + EPS) * g + b


def conformer_conv(a_val, a_gate, conv_w, conv_b, ln_g, ln_b):
    h = a_val.astype(F32) * jax.nn.sigmoid(a_gate.astype(F32))
    h = lax.conv_general_dilated(
        h, conv_w.astype(F32)[:, None, :], window_strides=(1,),
        padding=[(CONV_WIDTH // 2, CONV_WIDTH // 2)],
        dimension_numbers=('NWC', 'WIO', 'NWC'),
        feature_group_count=CONV_CH) + conv_b.astype(F32)
    h = layernorm(h, ln_g.astype(F32), ln_b.astype(F32))
    return jax.nn.silu(h)


def dilated_band_attention(q, k, v, slopes, window, dilation):
    B, H, S, Dh = q.shape
    r = window // (2 * dilation)
    L = S // dilation
    nb = -(-L // r)
    Lp = nb * r

    def to_classes(t):
        t = t.reshape(B, H, L, dilation, Dh).transpose(0, 1, 3, 2, 4)
        return jnp.pad(t, ((0, 0), (0, 0), (0, 0), (0, Lp - L), (0, 0)))

    def key_blocks(t):
        t = jnp.pad(to_classes(t), ((0, 0), (0, 0), (0, 0), (r, r), (0, 0)))
        return t.reshape(B, H, dilation, nb + 2, r, Dh)

    qb = to_classes(q).reshape(B, H, dilation, nb, r, Dh)
    kb = key_blocks(k)
    vb = key_blocks(v)

    blk = jnp.arange(nb)[:, None, None, None]
    a_q = jnp.arange(r)[None, :, None, None]
    off = jnp.arange(3)[None, None, :, None]
    a_k = jnp.arange(r)[None, None, None, :]
    qi = blk * r + a_q
    kj = (blk + off - 1) * r + a_k
    rel = kj - qi
    valid = (jnp.abs(rel) <= r) & (kj >= 0) & (kj < L)
    dist = (dilation * jnp.abs(rel)).astype(F32)
    bias = -slopes[:, None, None, None, None, None] * dist

    scale = 1.0 / math.sqrt(Dh)
    s = jnp.stack([jnp.einsum('bhcnqd,bhcnkd->bhcnqk', qb, kb[:, :, :, o:o + nb])
                   for o in range(3)], axis=-2)
    s = jnp.where(valid, s * scale + bias, NEG_INF)
    m = jnp.max(s, axis=(-2, -1), keepdims=True)
    p = jnp.exp(s - m)
    z = jnp.sum(p, axis=(-2, -1))
    o_acc = sum(jnp.einsum('bhcnqk,bhcnkd->bhcnqd', p[..., o, :], vb[:, :, :, o:o + nb])
                for o in range(3))
    out = o_acc / z[..., None]
    lse = m[..., 0, 0] + jnp.log(z)
    out = out.reshape(B, H, dilation, Lp, Dh)[:, :, :, :L]
    out = out.transpose(0, 1, 3, 2, 4).reshape(B, H, S, Dh)
    lse = lse.reshape(B, H, dilation, Lp)[:, :, :, :L].transpose(0, 1, 3, 2).reshape(B, H, S)
    return out, lse


def even_mixer(h, w_in, conv_w, conv_b, ln_g, ln_b, q_g, k_g, w_out):
    B, S, _ = h.shape
    proj = jnp.einsum('bsd,de->bse', h, w_in)
    a_val, a_gate, q, k, v = jnp.split(
        proj, [CONV_CH, 2 * CONV_CH, 2 * CONV_CH + ATT_WIDTH, 2 * CONV_CH + 2 * ATT_WIDTH], axis=-1)
    conv_out = conformer_conv(a_val, a_gate, conv_w, conv_b, ln_g, ln_b)

    def heads(t):
        return t.reshape(B, S, ATT_HEADS, HEAD_DIM).transpose(0, 2, 1, 3).astype(F32)

    qh = rmsnorm(heads(q), q_g)
    kh = rmsnorm(heads(k), k_g)
    vh = heads(v)
    slopes = 2.0 ** (-8.0 * jnp.arange(1, ATT_HEADS + 1, dtype=F32) / ATT_HEADS)
    results = [dilated_band_attention(qh, kh, vh, slopes, w, d) for (w, d) in DILATED]
    outs = jnp.stack([res[0] for res in results])
    lses = jnp.stack([res[1] for res in results])
    wts = jax.nn.softmax(lses, axis=0)
    att = jnp.sum(wts[..., None] * outs, axis=0)
    att = att.transpose(0, 2, 1, 3).reshape(B, S, ATT_WIDTH)
    mixed = jnp.concatenate([conv_out, att], axis=-1).astype(h.dtype)
    return jnp.einsum('bsc,cd->bsd', mixed, w_out)


def s5_scan(u, lam_re, lam_im, log_dt, b_re, b_im, c_re, c_im):
    lam_re = lam_re.astype(F32)
    lam_im = lam_im.astype(F32)
    dt = jnp.exp(log_dt.astype(F32))[:, None]
    mag = jnp.exp(lam_re * dt)
    ang = lam_im * dt
    lb_re = mag * jnp.cos(ang)
    lb_im = mag * jnp.sin(ang)
    den = lam_re * lam_re + lam_im * lam_im
    nr = lb_re - 1.0
    f_re = (nr * lam_re + lb_im * lam_im) / den
    f_im = (lb_im * lam_re - nr * lam_im) / den
    b_re = b_re.astype(F32)
    b_im = b_im.astype(F32)
    bb_re = f_re[..., None] * b_re - f_im[..., None] * b_im
    bb_im = f_re[..., None] * b_im + f_im[..., None] * b_re
    bu_re = jnp.einsum('bsgc,gpc->bsgp', u, bb_re)
    bu_im = jnp.einsum('bsgc,gpc->bsgp', u, bb_im)
    a_re = jnp.broadcast_to(lb_re, bu_re.shape)
    a_im = jnp.broadcast_to(lb_im, bu_im.shape)

    def combine(e1, e2):
        a1r, a1i, b1r, b1i = e1
        a2r, a2i, b2r, b2i = e2
        return (a2r * a1r - a2i * a1i,
                a2r * a1i + a2i * a1r,
                a2r * b1r - a2i * b1i + b2r,
                a2r * b1i + a2i * b1r + b2i)

    _, _, h_re, h_im = lax.associative_scan(combine, (a_re, a_im, bu_re, bu_im), axis=1)
    return (jnp.einsum('bsgp,gcp->bsgc', h_re, c_re.astype(F32))
            - jnp.einsum('bsgp,gcp->bsgc', h_im, c_im.astype(F32)))


def s5_mixer(h, lam_re, lam_im, log_dt, b_re, b_im, c_re, c_im, d_skip, w_glu):
    B, S, D = h.shape
    hf = h.astype(F32)
    u = hf.reshape(B, S, SSM_GROUPS, SSM_GROUP)
    y_f = s5_scan(u, lam_re[0], lam_im[0], log_dt[0], b_re[0], b_im[0], c_re[0], c_im[0])
    y_b = jnp.flip(s5_scan(jnp.flip(u, axis=1), lam_re[1], lam_im[1], log_dt[1],
                           b_re[1], b_im[1], c_re[1], c_im[1]), axis=1)
    y = (y_f + y_b).reshape(B, S, D) + d_skip.astype(F32) * hf
    z = jax.nn.gelu(y)
    val, gate = jnp.split(jnp.einsum('bsd,de->bse', z, w_glu.astype(F32)), 2, axis=-1)
    return (val * jax.nn.sigmoid(gate)).astype(h.dtype)


def expert_choice_ffn(h, w_router, b_router, w_gate, w_up, w_down):
    B, S, D = h.shape
    cap = CAPACITY_FACTOR * S // N_EXPERTS
    logits = jnp.einsum('bsd,de->bse', h.astype(F32), w_router.astype(F32)) + b_router.astype(F32)
    aff = jax.nn.softmax(logits, axis=-1)
    gate, idx = lax.top_k(jnp.swapaxes(aff, 1, 2), cap)
    xin = jax.vmap(lambda hb, ib: hb[ib])(h, idx)
    g = jnp.einsum('becd,edf->becf', xin, w_gate)
    u = jnp.einsum('becd,edf->becf', xin, w_up)
    out = jnp.einsum('becf,efd->becd', jax.nn.silu(g) * u, w_down)
    out = out * gate[..., None].astype(out.dtype)

    def scatter(ob, ib):
        return jnp.zeros((S, D), ob.dtype).at[ib.reshape(-1)].add(ob.reshape(-1, D))

    return jax.vmap(scatter)(out, idx).astype(h.dtype)


def setup_inputs(seed: int = 0) -> dict:
    key = jax.random.key(seed)
    ks = iter(jax.random.split(key, 40))

    def nrm(shape, scale):
        return scale * jax.random.normal(next(ks), shape, F32)

    def gain(shape):
        return 1.0 + 0.05 * jax.random.normal(next(ks), shape, F32)

    P = SSM_STATE
    G = SSM_GROUPS
    lam_im_base = math.pi * jnp.arange(P, dtype=F32)
    return {
        'x': jax.random.normal(next(ks), (BATCH, SEQ, D_MODEL), F32),
        'mix_norm_even': gain((N_EVEN, D_MODEL)),
        'w_in': nrm((N_EVEN, D_MODEL, IN_COLS), D_MODEL ** -0.5),
        'conv_w': nrm((N_EVEN, CONV_WIDTH, CONV_CH), CONV_WIDTH ** -0.5),
        'conv_b': nrm((N_EVEN, CONV_CH), 0.01),
        'conv_ln_g': gain((N_EVEN, CONV_CH)),
        'conv_ln_b': nrm((N_EVEN, CONV_CH), 0.01),
        'q_norm': gain((N_EVEN, HEAD_DIM)),
        'k_norm': gain((N_EVEN, HEAD_DIM)),
        'w_out': nrm((N_EVEN, MIX_WIDTH, D_MODEL), MIX_WIDTH ** -0.5),
        'mix_norm_odd': gain((N_ODD, D_MODEL)),
        'ssm_lam_re': -0.5 + nrm((N_ODD, 2, G, P), 0.01),
        'ssm_lam_im': lam_im_base + nrm((N_ODD, 2, G, P), 0.01),
        'ssm_log_dt': jax.random.uniform(next(ks), (N_ODD, 2, G), F32,
                                         math.log(1e-3), math.log(1e-1)),
        'ssm_b_re': nrm((N_ODD, 2, G, P, SSM_GROUP), (2 * SSM_GROUP) ** -0.5),
        'ssm_b_im': nrm((N_ODD, 2, G, P, SSM_GROUP), (2 * SSM_GROUP) ** -0.5),
        'ssm_c_re': nrm((N_ODD, 2, G, SSM_GROUP, P), P ** -0.5),
        'ssm_c_im': nrm((N_ODD, 2, G, SSM_GROUP, P), P ** -0.5),
        'ssm_d': nrm((N_ODD, D_MODEL), 0.5),
        'w_glu': nrm((N_ODD, D_MODEL, 2 * D_MODEL), D_MODEL ** -0.5),
        'ffn_norm': gain((DEPTH, D_MODEL)),
        'w_router': nrm((DEPTH, D_MODEL, N_EXPERTS), D_MODEL ** -0.5),
        'b_router': nrm((DEPTH, N_EXPERTS), 0.01),
        'w_e_gate': nrm((DEPTH, N_EXPERTS, D_MODEL, EXPERT_HIDDEN), D_MODEL ** -0.5),
        'w_e_up': nrm((DEPTH, N_EXPERTS, D_MODEL, EXPERT_HIDDEN), D_MODEL ** -0.5),
        'w_e_down': nrm((DEPTH, N_EXPERTS, EXPERT_HIDDEN, D_MODEL), EXPERT_HIDDEN ** -0.5),
    }


def reference(x, mix_norm_even, w_in, conv_w, conv_b, conv_ln_g, conv_ln_b, q_norm, k_norm,
              w_out, mix_norm_odd, ssm_lam_re, ssm_lam_im, ssm_log_dt, ssm_b_re, ssm_b_im,
              ssm_c_re, ssm_c_im, ssm_d, w_glu, ffn_norm, w_router, b_router,
              w_e_gate, w_e_up, w_e_down):
    for layer in range(DEPTH):
        i = layer // 2
        if layer % 2 == 0:
            x = x + even_mixer(rmsnorm(x, mix_norm_even[i]), w_in[i], conv_w[i], conv_b[i],
                               conv_ln_g[i], conv_ln_b[i], q_norm[i], k_norm[i], w_out[i])
        else:
            x = x + s5_mixer(rmsnorm(x, mix_norm_odd[i]), ssm_lam_re[i], ssm_lam_im[i],
                             ssm_log_dt[i], ssm_b_re[i], ssm_b_im[i], ssm_c_re[i], ssm_c_im[i],
                             ssm_d[i], w_glu[i])
        x = x + expert_choice_ffn(rmsnorm(x, ffn_norm[layer]), w_router[layer], b_router[layer],
                                  w_e_gate[layer], w_e_up[layer], w_e_down[layer])
    return x
```

```python
import functools
import math

import jax
import jax.numpy as jnp
from jax import lax
from jax.experimental import pallas as pl
from jax.experimental.pallas import tpu as pltpu

F32 = jnp.float32
BF16 = jnp.bfloat16
I32 = jnp.int32

EPS = 1e-6
NEG_INF = -1e30

CONV_CH = 512
CONV_WIDTH = 31
ATT_HEADS = 8
HEAD_DIM = 64
ATT_WIDTH = ATT_HEADS * HEAD_DIM
DILATED = ((128, 1), (512, 4), (2048, 16))
SSM_GROUP = 16
SSM_STATE = 64
N_EXPERTS = 16
CAPACITY_FACTOR = 2

LANES = 128
F32_MIN_EXP = -126
F32_MANTISSA_BITS = 23
MIB = 1024 * 1024


def _params(sem, vmem_mib):
    return pltpu.CompilerParams(dimension_semantics=sem, vmem_limit_bytes=vmem_mib * MIB)


def _rmsnorm(x, g):
    return x * lax.rsqrt(jnp.mean(x * x, axis=-1, keepdims=True) + EPS) * g


def _norm_matmul_kernel(x_ref, g_ref, w_ref, o_ref):
    y = _rmsnorm(x_ref[...], g_ref[...])
    o_ref[...] = jnp.dot(y.astype(BF16), w_ref[...], preferred_element_type=F32)


def _norm_matmul(x2d, g, w_bf16, tm=512):
    m, d = x2d.shape
    n = w_bf16.shape[1]
    return pl.pallas_call(
        _norm_matmul_kernel,
        grid=(m // tm,),
        in_specs=[pl.BlockSpec((tm, d), lambda i: (i, 0)),
                  pl.BlockSpec((1, d), lambda i: (0, 0)),
                  pl.BlockSpec((d, n), lambda i: (0, 0))],
        out_specs=pl.BlockSpec((tm, n), lambda i: (i, 0)),
        out_shape=jax.ShapeDtypeStruct((m, n), F32),
        compiler_params=_params(("parallel",), 48),
        name="norm_in_proj",
    )(x2d, g.reshape(1, d), w_bf16)


CONV_HALO = 16
CONV_ROWS = 256


def _conv_kernel(val_ref, gate_ref, w_ref, b_ref, o_ref, hp_ref):
    s = val_ref.shape[1]
    zeros = jnp.zeros((CONV_HALO, LANES), F32)
    hp_ref[0:CONV_HALO, :] = zeros
    hp_ref[CONV_HALO + s:2 * CONV_HALO + s, :] = zeros
    hp_ref[CONV_HALO:CONV_HALO + s, :] = val_ref[0] * jax.nn.sigmoid(gate_ref[0])
    first = CONV_HALO - CONV_WIDTH // 2
    for c in range(s // CONV_ROWS):
        r0 = c * CONV_ROWS
        acc = jnp.broadcast_to(b_ref[...], (CONV_ROWS, LANES))
        for k in range(CONV_WIDTH):
            acc = acc + w_ref[k:k + 1, :] * hp_ref[r0 + first + k:r0 + first + k + CONV_ROWS, :]
        o_ref[0, r0:r0 + CONV_ROWS, :] = acc


def _conformer_conv(proj3, conv_w, conv_b):
    b, s, _ = proj3.shape
    nct = CONV_CH // LANES
    return pl.pallas_call(
        _conv_kernel,
        grid=(b, nct),
        in_specs=[pl.BlockSpec((1, s, LANES), lambda i, j: (i, 0, j)),
                  pl.BlockSpec((1, s, LANES), lambda i, j: (i, 0, nct + j)),
                  pl.BlockSpec((CONV_WIDTH, LANES), lambda i, j: (0, j)),
                  pl.BlockSpec((1, LANES), lambda i, j: (0, j))],
        out_specs=pl.BlockSpec((1, s, LANES), lambda i, j: (i, 0, j)),
        out_shape=jax.ShapeDtypeStruct((b, s, CONV_CH), F32),
        scratch_shapes=[pltpu.VMEM((s + 2 * CONV_HALO, LANES), F32)],
        compiler_params=_params(("parallel", "parallel"), 32),
        name="glu_dwconv",
    )(proj3, proj3, conv_w, conv_b.reshape(1, CONV_CH))


ATT_QB = 128
ATT_RAD = 64
ATT_KW = ATT_QB + 2 * ATT_RAD
ATT_PREP_ROWS = 512


def _attn_kernel(slopes_ref, q_ref, k_ref, v_ref, qg_ref, kg_ref, o_ref,
                 qs_ref, kp_ref, vp_ref, ob_ref, *, seq, pad):
    pair = pl.program_id(1)
    lane = lax.broadcasted_iota(I32, (1, LANES), 1)
    lo = lane < HEAD_DIM
    scale = 1.0 / math.sqrt(HEAD_DIM)

    zpad = jnp.zeros((pad, LANES), F32)
    for hh in range(2):
        kp_ref[hh, 0:pad, :] = zpad
        kp_ref[hh, pad + seq:2 * pad + seq, :] = zpad
        vp_ref[hh, 0:pad, :] = zpad
        vp_ref[hh, pad + seq:2 * pad + seq, :] = zpad

    def head_rms(x, g):
        x2 = x * x
        s_lo = jnp.sum(jnp.where(lo, x2, 0.0), axis=-1, keepdims=True)
        s_hi = jnp.sum(jnp.where(lo, 0.0, x2), axis=-1, keepdims=True)
        ms = jnp.where(lo, s_lo, s_hi) * (1.0 / HEAD_DIM)
        return x * lax.rsqrt(ms + EPS) * g

    def prep(i, carry):
        r = pl.multiple_of(i * ATT_PREP_ROWS, ATT_PREP_ROWS)
        rows = pl.ds(r, ATT_PREP_ROWS)
        prows = pl.ds(pad + r, ATT_PREP_ROWS)
        qs_ref[rows, :] = head_rms(q_ref[0, rows, :], qg_ref[...]) * scale
        kn = head_rms(k_ref[0, rows, :], kg_ref[...])
        kp_ref[0, prows, :] = jnp.where(lo, kn, 0.0)
        kp_ref[1, prows, :] = jnp.where(lo, 0.0, kn)
        v = v_ref[0, rows, :]
        vp_ref[0, prows, :] = jnp.where(lo, v, jnp.where(lane == HEAD_DIM, 1.0, 0.0))
        vp_ref[1, prows, :] = jnp.where(lo, jnp.where(lane == 0, 1.0, 0.0), v)
        return carry

    lax.fori_loop(0, seq // ATT_PREP_ROWS, prep, 0)

    rowi = lax.broadcasted_iota(I32, (ATT_QB, ATT_KW), 0)
    coli = lax.broadcasted_iota(I32, (ATT_QB, ATT_KW), 1)
    arel = jnp.abs(coli - ATT_RAD - rowi)
    kcol = lax.broadcasted_iota(I32, (1, ATT_KW), 1) - ATT_RAD

    for hh in range(2):
        slope = slopes_ref[pair * 2 + hh]
        zlane = HEAD_DIM if hh == 0 else 0
        mlane = zlane + 1
        for bi, (window, dil) in enumerate(DILATED):
            assert window // (2 * dil) == ATT_RAD
            cls_len = seq // dil
            nbc = cls_len // ATT_QB
            bias = jnp.where(arel <= ATT_RAD, -(slope * dil) * arel.astype(F32), NEG_INF)

            def blk(j, carry, bi=bi, dil=dil, cls_len=cls_len, nbc=nbc, bias=bias, hh=hh, mlane=mlane):
                c = j // nbc
                n = j - c * nbc
                qstart = c + dil * ATT_QB * n
                kstart = pad + c + dil * (ATT_QB * n - ATT_RAD)
                if dil == 1:
                    qsl = pl.ds(qstart, ATT_QB)
                    ksl = pl.ds(kstart, ATT_KW)
                else:
                    qsl = pl.ds(qstart, ATT_QB, stride=dil)
                    ksl = pl.ds(kstart, ATT_KW, stride=dil)
                qb = qs_ref[qsl, :].astype(BF16)
                kw = kp_ref[hh, ksl, :].astype(BF16)
                vw = vp_ref[hh, ksl, :].astype(BF16)
                sc = lax.dot_general(qb, kw, (((1,), (1,)), ((), ())),
                                     preferred_element_type=F32) + bias
                kj = kcol + ATT_QB * n
                sc = jnp.where((kj >= 0) & (kj < cls_len), sc, NEG_INF)
                m = jnp.max(sc, axis=-1, keepdims=True)
                p = jnp.exp(sc - m)
                o = jnp.dot(p.astype(BF16), vw, preferred_element_type=F32)
                ob_ref[bi, qsl, :] = jnp.where(lane == mlane, m, o)
                return carry

            lax.fori_loop(0, seq // ATT_QB, blk, 0)

        def combine(i, carry, hh=hh, zlane=zlane, mlane=mlane):
            r = pl.multiple_of(i * ATT_PREP_ROWS, ATT_PREP_ROWS)
            rows = pl.ds(r, ATT_PREP_ROWS)
            parts = [ob_ref[bi, rows, :] for bi in range(len(DILATED))]
            ms = [jnp.sum(jnp.where(lane == mlane, o, 0.0), axis=-1, keepdims=True) for o in parts]
            mx = functools.reduce(jnp.maximum, ms)
            acc = sum(jnp.exp(m - mx) * o for m, o in zip(ms, parts))
            den = jnp.sum(jnp.where(lane == zlane, acc, 0.0), axis=-1, keepdims=True)
            res = acc / den
            if hh == 0:
                o_ref[0, rows, :] = res
            else:
                o_ref[0, rows, :] = jnp.where(lo, o_ref[0, rows, :], res)
            return carry

        lax.fori_loop(0, seq // ATT_PREP_ROWS, combine, 0)


def _dilated_attention(proj3, q_norm, k_norm):
    b, s, _ = proj3.shape
    pad = ATT_RAD * max(d for _, d in DILATED)
    npair = ATT_HEADS // 2
    qoff = 2 * CONV_CH // LANES
    slopes = jnp.asarray([2.0 ** (-8.0 * (h + 1) / ATT_HEADS) for h in range(ATT_HEADS)], F32)
    qg = jnp.tile(q_norm.reshape(1, HEAD_DIM), (1, 2))
    kg = jnp.tile(k_norm.reshape(1, HEAD_DIM), (1, 2))
    kernel = functools.partial(_attn_kernel, seq=s, pad=pad)
    return pl.pallas_call(
        kernel,
        grid=(b, npair),
        in_specs=[pl.BlockSpec(memory_space=pltpu.SMEM),
                  pl.BlockSpec((1, s, LANES), lambda i, j: (i, 0, qoff + j)),
                  pl.BlockSpec((1, s, LANES), lambda i, j: (i, 0, qoff + npair + j)),
                  pl.BlockSpec((1, s, LANES), lambda i, j: (i, 0, qoff + 2 * npair + j)),
                  pl.BlockSpec((1, LANES), lambda i, j: (0, 0)),
                  pl.BlockSpec((1, LANES), lambda i, j: (0, 0))],
        out_specs=pl.BlockSpec((1, s, LANES), lambda i, j: (i, 0, j)),
        out_shape=jax.ShapeDtypeStruct((b, s, ATT_WIDTH), F32),
        scratch_shapes=[pltpu.VMEM((s, LANES), F32),
                        pltpu.VMEM((2, s + 2 * pad, LANES), F32),
                        pltpu.VMEM((2, s + 2 * pad, LANES), F32),
                        pltpu.VMEM((len(DILATED), s, LANES), F32)],
        compiler_params=_params(("parallel", "parallel"), 48),
        name="dilated_attn",
    )(slopes, proj3, proj3, proj3, qg, kg)


def _outproj_kernel(x_ref, conv_ref, att_ref, g_ref, b_ref, w1_ref, w2_ref, o_ref):
    c = conv_ref[...]
    mu = jnp.mean(c, axis=-1, keepdims=True)
    cc = c - mu
    var = jnp.mean(cc * cc, axis=-1, keepdims=True)
    hn = cc * lax.rsqrt(var + EPS) * g_ref[...] + b_ref[...]
    a = hn * jax.nn.sigmoid(hn)
    y = jnp.dot(a.astype(BF16), w1_ref[...], preferred_element_type=F32)
    y = y + jnp.dot(att_ref[...].astype(BF16), w2_ref[...], preferred_element_type=F32)
    o_ref[...] = x_ref[...] + y


def _out_projection(x2d, conv2d, att2d, ln_g, ln_b, w_out_bf16, tm=512):
    m, d = x2d.shape
    row = lambda i: (i, 0)
    fixed = lambda i: (0, 0)
    return pl.pallas_call(
        _outproj_kernel,
        grid=(m // tm,),
        in_specs=[pl.BlockSpec((tm, d), row),
                  pl.BlockSpec((tm, CONV_CH), row),
                  pl.BlockSpec((tm, ATT_WIDTH), row),
                  pl.BlockSpec((1, CONV_CH), fixed),
                  pl.BlockSpec((1, CONV_CH), fixed),
                  pl.BlockSpec((CONV_CH, d), fixed),
                  pl.BlockSpec((ATT_WIDTH, d), lambda i: (1, 0))],
        out_specs=pl.BlockSpec((tm, d), row),
        out_shape=jax.ShapeDtypeStruct((m, d), F32),
        compiler_params=_params(("parallel",), 32),
        name="ln_out_proj",
    )(x2d, conv2d, att2d, ln_g.reshape(1, CONV_CH), ln_b.reshape(1, CONV_CH), w_out_bf16, w_out_bf16)


def _even_mixer(x, mix_g, w_in, conv_w, conv_b, ln_g, ln_b, q_norm, k_norm, w_out):
    b, s, d = x.shape
    x2d = x.reshape(b * s, d)
    proj = _norm_matmul(x2d, mix_g, w_in.astype(BF16))
    proj3 = proj.reshape(b, s, -1)
    conv = _conformer_conv(proj3, conv_w, conv_b)
    att = _dilated_attention(proj3, q_norm, k_norm)
    out = _out_projection(x2d, conv.reshape(b * s, CONV_CH), att.reshape(b * s, ATT_WIDTH),
                          ln_g, ln_b, w_out.astype(BF16))
    return out.reshape(b, s, d)


def _router_kernel(x_ref, g_ref, wr_ref, br_ref, ht_ref, aff_ref):
    h = _rmsnorm(x_ref[0], g_ref[...])
    logits = jnp.dot(h, wr_ref[...], precision=lax.Precision.HIGHEST,
                     preferred_element_type=F32) + br_ref[...]
    lt = logits.T[:N_EXPERTS, :]
    mx = jnp.max(lt, axis=0, keepdims=True)
    ex = jnp.exp(lt - mx)
    aff_ref[0] = ex / jnp.sum(ex, axis=0, keepdims=True)
    ht_ref[0] = h.T.astype(BF16)


def _router(x, g, w_router, b_router, tm=512):
    b, s, d = x.shape
    wr = jnp.zeros((d, LANES), F32).at[:, :N_EXPERTS].set(w_router)
    br = jnp.zeros((1, LANES), F32).at[0, :N_EXPERTS].set(b_router)
    return pl.pallas_call(
        _router_kernel,
        grid=(b, s // tm),
        in_specs=[pl.BlockSpec((1, tm, d), lambda i, j: (i, j, 0)),
                  pl.BlockSpec((1, d), lambda i, j: (0, 0)),
                  pl.BlockSpec((d, LANES), lambda i, j: (0, 0)),
                  pl.BlockSpec((1, LANES), lambda i, j: (0, 0))],
        out_specs=[pl.BlockSpec((1, d, tm), lambda i, j: (i, 0, j)),
                   pl.BlockSpec((1, N_EXPERTS, tm), lambda i, j: (i, 0, j))],
        out_shape=[jax.ShapeDtypeStruct((b, d, s), BF16),
                   jax.ShapeDtypeStruct((b, N_EXPERTS, s), F32)],
        compiler_params=_params(("parallel", "parallel"), 32),
        name="moe_router",
    )(x, g.reshape(1, d), wr, br)


def _select_kernel(aff_ref, pos_ref, gsel_ref, cs_ref, *, cap):
    a = aff_ref[0]
    s = a.shape[1]
    capf = float(cap)

    def count(mask):
        return jnp.sum(jnp.where(mask, 1.0, 0.0), axis=-1, keepdims=True)

    def enough(cand):
        return count(a >= cand) >= capf

    tiny = float(2.0 ** F32_MIN_EXP)
    thr = jnp.full((N_EXPERTS, 1), tiny, F32)
    normal = enough(thr)
    for j in range(6, -1, -1):
        cand = thr * float(2.0 ** (2 ** j))
        thr = jnp.where(enough(cand), cand, thr)
    step = thr * 0.5
    for _ in range(F32_MANTISSA_BITS):
        cand = thr + step
        thr = jnp.where(enough(cand), cand, thr)
        step = step * 0.5
    thr = jnp.where(normal, thr, 0.0)

    ri = lax.broadcasted_iota(I32, (LANES, LANES), 0)
    ci = lax.broadcasted_iota(I32, (LANES, LANES), 1)
    tri = jnp.where(ri < ci, 1.0, 0.0).astype(BF16)

    def exclusive_cumsum(mask):
        ind = jnp.where(mask, 1.0, 0.0)
        off = jnp.zeros((N_EXPERTS, 1), F32)
        for j in range(s // LANES):
            xc = ind[:, j * LANES:(j + 1) * LANES]
            cs_ref[:, j * LANES:(j + 1) * LANES] = (
                jnp.dot(xc.astype(BF16), tri, preferred_element_type=F32) + off)
            off = off + jnp.sum(xc, axis=-1, keepdims=True)
        return cs_ref[...]

    gt = a > thr
    eq = a == thr
    need = capf - count(gt)
    tie_rank = exclusive_cumsum(eq)
    sel = gt | (eq & (tie_rank < need))
    slot = exclusive_cumsum(sel)
    pos_ref[0] = jnp.where(sel, slot, -1.0).astype(I32)
    gsel_ref[0] = jnp.where(sel, a, 0.0)


def _select(aff, cap):
    b, e, s = aff.shape
    blk = pl.BlockSpec((1, e, s), lambda i: (i, 0, 0))
    return pl.pallas_call(
        functools.partial(_select_kernel, cap=cap),
        grid=(b,),
        in_specs=[blk],
        out_specs=[blk, blk],
        out_shape=[jax.ShapeDtypeStruct((b, e, s), I32),
                   jax.ShapeDtypeStruct((b, e, s), F32)],
        scratch_shapes=[pltpu.VMEM((e, s), F32)],
        compiler_params=_params(("parallel",), 32),
        name="moe_select",
    )(aff)


def _expert_kernel(ht_ref, pos_ref, wg_ref, wu_ref, wd_ref, yt_ref, *, cap):
    pos = pos_ref[0, 0]
    s = pos.shape[1]
    slot = lax.broadcasted_iota(I32, (cap, s), 0)
    onehot = jnp.where(pos == slot, 1.0, 0.0).astype(BF16)
    xt = lax.dot_general(ht_ref[0], onehot, (((1,), (1,)), ((), ())),
                         preferred_element_type=F32).astype(BF16)
    g = jnp.dot(wg_ref[0], xt, preferred_element_type=F32)
    u = jnp.dot(wu_ref[0], xt, preferred_element_type=F32)
    a = (g * jax.nn.sigmoid(g) * u).astype(BF16)
    yt_ref[0, 0] = jnp.dot(wd_ref[0], a, preferred_element_type=F32).astype(BF16)


def _experts(ht, pos4, wg_t, wu_t, wd_t, cap):
    b, d, s = ht.shape
    e, f, _ = wg_t.shape
    return pl.pallas_call(
        functools.partial(_expert_kernel, cap=cap),
        grid=(b, e),
        in_specs=[pl.BlockSpec((1, d, s), lambda i, j: (i, 0, 0)),
                  pl.BlockSpec((1, 1, 1, s), lambda i, j: (i, j, 0, 0)),
                  pl.BlockSpec((1, f, d), lambda i, j: (j, 0, 0)),
                  pl.BlockSpec((1, f, d), lambda i, j: (j, 0, 0)),
                  pl.BlockSpec((1, d, f), lambda i, j: (j, 0, 0))],
        out_specs=pl.BlockSpec((1, 1, d, cap), lambda i, j: (i, j, 0, 0)),
        out_shape=jax.ShapeDtypeStruct((b, e, d, cap), BF16),
        compiler_params=_params(("parallel", "parallel"), 48),
        name="moe_experts",
    )(ht, pos4, wg_t, wu_t, wd_t)


def _scatter_kernel(x_ref, yt_ref, pos_ref, gsel_ref, o_ref, acc_ref, *, cap):
    e = pl.program_id(2)

    @pl.when(e == 0)
    def _():
        acc_ref[...] = jnp.zeros_like(acc_ref)

    pos = pos_ref[0, 0]
    tn = pos.shape[1]
    slot = lax.broadcasted_iota(I32, (cap, tn), 0)
    onehot = jnp.where(pos == slot, 1.0, 0.0).astype(BF16)
    contrib = jnp.dot(yt_ref[0, 0], onehot, preferred_element_type=F32)
    acc_ref[...] += contrib * gsel_ref[0, 0]

    @pl.when(e == pl.num_programs(2) - 1)
    def _():
        o_ref[0] = x_ref[0] + acc_ref[...].T


def _scatter(x, yt, pos4, gsel4, cap, tn=1024):
    b, s, d = x.shape
    e = yt.shape[1]
    return pl.pallas_call(
        functools.partial(_scatter_kernel, cap=cap),
        grid=(b, s // tn, e),
        in_specs=[pl.BlockSpec((1, tn, d), lambda i, j, k: (i, j, 0)),
                  pl.BlockSpec((1, 1, d, cap), lambda i, j, k: (i, k, 0, 0)),
                  pl.BlockSpec((1, 1, 1, tn), lambda i, j, k: (i, k, 0, j)),
                  pl.BlockSpec((1, 1, 1, tn), lambda i, j, k: (i, k, 0, j))],
        out_specs=pl.BlockSpec((1, tn, d), lambda i, j, k: (i, j, 0)),
        out_shape=jax.ShapeDtypeStruct((b, s, d), F32),
        scratch_shapes=[pltpu.VMEM((d, tn), F32)],
        compiler_params=_params(("parallel", "parallel", "arbitrary"), 48),
        name="moe_scatter",
    )(x, yt, pos4, gsel4)


def _moe(x, ffn_g, w_router, b_router, w_gate, w_up, w_down):
    b, s, d = x.shape
    cap = CAPACITY_FACTOR * s // N_EXPERTS
    ht, aff = _router(x, ffn_g, w_router, b_router)
    pos, gsel = _select(aff, cap)
    pos4 = pos.reshape(b, N_EXPERTS, 1, s)
    gsel4 = gsel.reshape(b, N_EXPERTS, 1, s)
    wg_t = jnp.swapaxes(w_gate, 1, 2).astype(BF16)
    wu_t = jnp.swapaxes(w_up, 1, 2).astype(BF16)
    wd_t = jnp.swapaxes(w_down, 1, 2).astype(BF16)
    yt = _experts(ht, pos4, wg_t, wu_t, wd_t, cap)
    return _scatter(x, yt, pos4, gsel4, cap)


SSM_CHUNK = 16


def _norm_cast_kernel(x_ref, g_ref, o_ref):
    o_ref[...] = _rmsnorm(x_ref[...], g_ref[...]).astype(BF16)


def _norm_cast(x2d, g, tm=1024):
    m, d = x2d.shape
    return pl.pallas_call(
        _norm_cast_kernel,
        grid=(m // tm,),
        in_specs=[pl.BlockSpec((tm, d), lambda i: (i, 0)),
                  pl.BlockSpec((1, d), lambda i: (0, 0))],
        out_specs=pl.BlockSpec((tm, d), lambda i: (i, 0)),
        out_shape=jax.ShapeDtypeStruct((m, d), BF16),
        compiler_params=_params(("parallel",), 32),
        name="s5_norm",
    )(x2d, g.reshape(1, d))


def _s5_core_kernel(u_ref, m_ref, w_ref, v_ref, a_ref, y_ref, *, nc):
    u = u_ref[0]
    n_rows = u.shape[0]
    y = jnp.dot(u, m_ref[0], preferred_element_type=F32)
    x = jnp.dot(u, w_ref[0], preferred_element_type=F32)
    hr = x[:, :LANES]
    hi = x[:, LANES:]
    ar = a_ref[0, 0:1, :]
    ai = a_ref[0, 1:2, :]
    lane = lax.broadcasted_iota(I32, (1, LANES), 1)
    fwd = lane < SSM_STATE
    rown = lax.broadcasted_iota(I32, (n_rows, 1), 0) % nc

    def shifted(val, step):
        down = jnp.where(rown >= step, pltpu.roll(val, step, axis=0), 0.0)
        up = jnp.where(rown < nc - step, pltpu.roll(val, n_rows - step, axis=0), 0.0)
        return jnp.where(fwd, down, up)

    step = 1
    while step < nc:
        pr = shifted(hr, step)
        pi = shifted(hi, step)
        hr, hi = hr + ar * pr - ai * pi, hi + ar * pi + ai * pr
        ar, ai = ar * ar - ai * ai, 2.0 * ar * ai
        step *= 2
    hin_r = shifted(hr, 1).astype(BF16)
    hin_i = shifted(hi, 1).astype(BF16)
    y = y + jnp.dot(hin_r, v_ref[0, :LANES, :], preferred_element_type=F32)
    y = y + jnp.dot(hin_i, v_ref[0, LANES:, :], preferred_element_type=F32)
    y_ref[0] = y


def _s5_core(ug, mmat, wmat, vmat, amat, nc):
    g, n, k = ug.shape
    blk = lambda r, c: pl.BlockSpec((1, r, c), lambda i: (i, 0, 0))
    return pl.pallas_call(
        functools.partial(_s5_core_kernel, nc=nc),
        grid=(g,),
        in_specs=[blk(n, k), blk(k, k), blk(k, 2 * LANES), blk(2 * LANES, k), blk(8, LANES)],
        out_specs=blk(n, k),
        out_shape=jax.ShapeDtypeStruct((g, n, k), F32),
        compiler_params=_params(("parallel",), 48),
        name="s5_core",
    )(ug, mmat, wmat, vmat, amat)


def _s5_post_kernel(x_ref, g_ref, y_ref, d_ref, w_ref, o_ref):
    x = x_ref[...]
    d = x.shape[1]
    hf = _rmsnorm(x, g_ref[...])
    z = jax.nn.gelu(y_ref[...] + d_ref[...] * hf)
    vg = jnp.dot(z.astype(BF16), w_ref[...], preferred_element_type=F32)
    o_ref[...] = x + vg[:, :d] * jax.nn.sigmoid(vg[:, d:])


def _s5_post(x2d, g, y2d, d_skip, w_glu_bf16, tm=512):
    m, d = x2d.shape
    row = lambda i: (i, 0)
    fixed = lambda i: (0, 0)
    return pl.pallas_call(
        _s5_post_kernel,
        grid=(m // tm,),
        in_specs=[pl.BlockSpec((tm, d), row), pl.BlockSpec((1, d), fixed),
                  pl.BlockSpec((tm, d), row), pl.BlockSpec((1, d), fixed),
                  pl.BlockSpec((d, 2 * d), fixed)],
        out_specs=pl.BlockSpec((tm, d), row),
        out_shape=jax.ShapeDtypeStruct((m, d), F32),
        compiler_params=_params(("parallel",), 48),
        name="s5_gelu_glu",
    )(x2d, g.reshape(1, d), y2d, d_skip.reshape(1, d), w_glu_bf16)


def _s5_matrices(lam_re, lam_im, log_dt, b_re, b_im, c_re, c_im):
    t = SSM_CHUNK
    g, p = lam_re.shape[1:]
    c = SSM_GROUP
    dt = jnp.exp(log_dt)[..., None]
    den = lam_re * lam_re + lam_im * lam_im
    mag = jnp.exp(lam_re * dt)
    ang = lam_im * dt
    lb_re = mag * jnp.cos(ang)
    lb_im = mag * jnp.sin(ang)
    nr = lb_re - 1.0
    f_re = (nr * lam_re + lb_im * lam_im) / den
    f_im = (lb_im * lam_re - nr * lam_im) / den
    bb_re = f_re[..., None] * b_re - f_im[..., None] * b_im
    bb_im = f_re[..., None] * b_im + f_im[..., None] * b_re
    lag = jnp.arange(t + 1, dtype=F32)[:, None, None, None]
    pmag = jnp.exp(lag * (lam_re * dt))
    pw_re = pmag * jnp.cos(lag * ang)
    pw_im = pmag * jnp.sin(lag * ang)
    cl_re = c_re[None] * pw_re[..., None, :] - c_im[None] * pw_im[..., None, :]
    cl_im = c_re[None] * pw_im[..., None, :] + c_im[None] * pw_re[..., None, :]
    kern = (jnp.einsum('ldgcp,dgpe->ldgce', cl_re[:t], bb_re)
            - jnp.einsum('ldgcp,dgpe->ldgce', cl_im[:t], bb_im))
    tt = jnp.arange(t)
    diff = tt[None, :] - tt[:, None]
    kf = jnp.where((diff >= 0)[:, :, None, None, None], kern[jnp.clip(diff, 0, t - 1), 0], 0.0)
    kb = jnp.where((diff <= 0)[:, :, None, None, None], kern[jnp.clip(-diff, 0, t - 1), 1], 0.0)
    mmat = jnp.transpose(kf + kb, (2, 0, 4, 1, 3)).reshape(g, t * c, t * c)
    wf_re = pw_re[t - 1 - tt, 0][..., None] * bb_re[0][None]
    wf_im = pw_im[t - 1 - tt, 0][..., None] * bb_re[0][None]
    wf_re = wf_re - pw_im[t - 1 - tt, 0][..., None] * bb_im[0][None]
    wf_im = wf_im + pw_re[t - 1 - tt, 0][..., None] * bb_im[0][None]
    wb_re = pw_re[tt, 1][..., None] * bb_re[1][None] - pw_im[tt, 1][..., None] * bb_im[1][None]
    wb_im = pw_im[tt, 1][..., None] * bb_re[1][None] + pw_re[tt, 1][..., None] * bb_im[1][None]
    wmat = jnp.concatenate([wf_re, wb_re, wf_im, wb_im], axis=2)
    wmat = jnp.transpose(wmat, (1, 0, 3, 2)).reshape(g, t * c, 4 * p)
    vf_re = cl_re[tt + 1, 0]
    vf_im = -cl_im[tt + 1, 0]
    vb_re = cl_re[t - tt, 1]
    vb_im = -cl_im[t - tt, 1]
    vmat = jnp.concatenate([vf_re, vb_re, vf_im, vb_im], axis=3)
    vmat = jnp.transpose(vmat, (1, 3, 0, 2)).reshape(g, 4 * p, t * c)
    a_re = jnp.concatenate([pw_re[t, 0], pw_re[t, 1]], axis=-1)
    a_im = jnp.concatenate([pw_im[t, 0], pw_im[t, 1]], axis=-1)
    amat = jnp.zeros((g, 8, 2 * p), F32).at[:, 0].set(a_re).at[:, 1].set(a_im)
    return mmat.astype(BF16), wmat.astype(BF16), vmat.astype(BF16), amat


def _s5_mixer(x, mix_g, lam_re, lam_im, log_dt, b_re, b_im, c_re, c_im, d_skip, w_glu):
    b, s, d = x.shape
    t = SSM_CHUNK
    g = d // SSM_GROUP
    nc = s // t
    x2d = x.reshape(b * s, d)
    u = _norm_cast(x2d, mix_g)
    ug = jnp.transpose(u.reshape(b * nc, t, g, SSM_GROUP), (2, 0, 1, 3)).reshape(g, b * nc, t * SSM_GROUP)
    mmat, wmat, vmat, amat = _s5_matrices(lam_re, lam_im, log_dt, b_re, b_im, c_re, c_im)
    yg = _s5_core(ug, mmat, wmat, vmat, amat, nc)
    y2d = jnp.transpose(yg.reshape(g, b * nc, t, SSM_GROUP), (1, 2, 0, 3)).reshape(b * s, d)
    out = _s5_post(x2d, mix_g, y2d, d_skip, w_glu.astype(BF16))
    return out.reshape(b, s, d)


def kernel(x, mix_norm_even, w_in, conv_w, conv_b, conv_ln_g, conv_ln_b, q_norm, k_norm, w_out, mix_norm_odd, ssm_lam_re, ssm_lam_im, ssm_log_dt, ssm_b_re, ssm_b_im, ssm_c_re, ssm_c_im, ssm_d, w_glu, ffn_norm, w_router, b_router, w_e_gate, w_e_up, w_e_down):
    depth = ffn_norm.shape[0]
    for layer in range(depth):
        i = layer // 2
        if layer % 2 == 0:
            x = _even_mixer(x, mix_norm_even[i], w_in[i], conv_w[i], conv_b[i], conv_ln_g[i],
                            conv_ln_b[i], q_norm[i], k_norm[i], w_out[i])
        else:
            x = _s5_mixer(x, mix_norm_odd[i], ssm_lam_re[i], ssm_lam_im[i], ssm_log_dt[i],
                          ssm_b_re[i], ssm_b_im[i], ssm_c_re[i], ssm_c_im[i], ssm_d[i], w_glu[i])
        x = _moe(x, ffn_norm[layer], w_router[layer], b_router[layer],
                 w_e_gate[layer], w_e_up[layer], w_e_down[layer])
    return x
```

```python
import functools
import math

import jax
import jax.numpy as jnp
from jax import lax
from jax.experimental import pallas as pl
from jax.experimental.pallas import tpu as pltpu

F32 = jnp.float32
BF16 = jnp.bfloat16
I32 = jnp.int32

EPS = 1e-6
NEG_INF = -1e30

CONV_CH = 512
CONV_WIDTH = 31
ATT_HEADS = 8
HEAD_DIM = 64
ATT_WIDTH = ATT_HEADS * HEAD_DIM
DILATED = ((128, 1), (512, 4), (2048, 16))
SSM_GROUP = 16
SSM_STATE = 64
N_EXPERTS = 16
CAPACITY_FACTOR = 2
MOE_GATHER_TOKENS = 1024

LANES = 128
F32_MIN_EXP = -126
F32_MANTISSA_BITS = 23
MIB = 1024 * 1024


def _params(sem, vmem_mib):
    return pltpu.CompilerParams(dimension_semantics=sem, vmem_limit_bytes=vmem_mib * MIB)


def _rmsnorm(x, g):
    return x * lax.rsqrt(jnp.mean(x * x, axis=-1, keepdims=True) + EPS) * g


def _norm_matmul_kernel(x_ref, g_ref, w_ref, o_ref):
    y = _rmsnorm(x_ref[...], g_ref[...])
    o_ref[...] = jnp.dot(y.astype(BF16), w_ref[...], preferred_element_type=F32)


def _norm_matmul(x2d, g, w_bf16, tm=512):
    m, d = x2d.shape
    n = w_bf16.shape[1]
    return pl.pallas_call(
        _norm_matmul_kernel,
        grid=(m // tm,),
        in_specs=[pl.BlockSpec((tm, d), lambda i: (i, 0)),
                  pl.BlockSpec((1, d), lambda i: (0, 0)),
                  pl.BlockSpec((d, n), lambda i: (0, 0))],
        out_specs=pl.BlockSpec((tm, n), lambda i: (i, 0)),
        out_shape=jax.ShapeDtypeStruct((m, n), F32),
        compiler_params=_params(("parallel",), 48),
        name="norm_in_proj",
    )(x2d, g.reshape(1, d), w_bf16)


CONV_HALO = 16
CONV_ROWS = 256


def _conv_kernel(val_ref, gate_ref, w_ref, b_ref, o_ref, hp_ref):
    s = val_ref.shape[1]
    zeros = jnp.zeros((CONV_HALO, LANES), F32)
    hp_ref[0:CONV_HALO, :] = zeros
    hp_ref[CONV_HALO + s:2 * CONV_HALO + s, :] = zeros
    hp_ref[CONV_HALO:CONV_HALO + s, :] = val_ref[0] * jax.nn.sigmoid(gate_ref[0])
    first = CONV_HALO - CONV_WIDTH // 2
    for c in range(s // CONV_ROWS):
        r0 = c * CONV_ROWS
        acc = jnp.broadcast_to(b_ref[...], (CONV_ROWS, LANES))
        for k in range(CONV_WIDTH):
            acc = acc + w_ref[k:k + 1, :] * hp_ref[r0 + first + k:r0 + first + k + CONV_ROWS, :]
        o_ref[0, r0:r0 + CONV_ROWS, :] = acc


def _conformer_conv(proj3, conv_w, conv_b):
    b, s, _ = proj3.shape
    nct = CONV_CH // LANES
    return pl.pallas_call(
        _conv_kernel,
        grid=(b, nct),
        in_specs=[pl.BlockSpec((1, s, LANES), lambda i, j: (i, 0, j)),
                  pl.BlockSpec((1, s, LANES), lambda i, j: (i, 0, nct + j)),
                  pl.BlockSpec((CONV_WIDTH, LANES), lambda i, j: (0, j)),
                  pl.BlockSpec((1, LANES), lambda i, j: (0, j))],
        out_specs=pl.BlockSpec((1, s, LANES), lambda i, j: (i, 0, j)),
        out_shape=jax.ShapeDtypeStruct((b, s, CONV_CH), F32),
        scratch_shapes=[pltpu.VMEM((s + 2 * CONV_HALO, LANES), F32)],
        compiler_params=_params(("parallel", "parallel"), 32),
        name="glu_dwconv",
    )(proj3, proj3, conv_w, conv_b.reshape(1, CONV_CH))


ATT_QB = 128
ATT_RAD = 64
ATT_KW = ATT_QB + 2 * ATT_RAD
ATT_PREP_ROWS = 512
ATT_UNROLL = 8


def _attn_kernel(slopes_ref, q_ref, k_ref, v_ref, qg_ref, kg_ref, o_ref,
                 qs_ref, kp_ref, vp_ref, ob_ref, *, seq, pad):
    pair = pl.program_id(1)
    lane = lax.broadcasted_iota(I32, (1, LANES), 1)
    lo = lane < HEAD_DIM
    scale = 1.0 / math.sqrt(HEAD_DIM)

    zpad = jnp.zeros((pad, LANES), F32)
    for hh in range(2):
        kp_ref[hh, 0:pad, :] = zpad
        kp_ref[hh, pad + seq:2 * pad + seq, :] = zpad
        vp_ref[hh, 0:pad, :] = zpad
        vp_ref[hh, pad + seq:2 * pad + seq, :] = zpad

    def head_rms(x, g):
        x2 = x * x
        s_lo = jnp.sum(jnp.where(lo, x2, 0.0), axis=-1, keepdims=True)
        s_hi = jnp.sum(jnp.where(lo, 0.0, x2), axis=-1, keepdims=True)
        ms = jnp.where(lo, s_lo, s_hi) * (1.0 / HEAD_DIM)
        return x * lax.rsqrt(ms + EPS) * g

    def prep(i, carry):
        r = pl.multiple_of(i * ATT_PREP_ROWS, ATT_PREP_ROWS)
        rows = pl.ds(r, ATT_PREP_ROWS)
        prows = pl.ds(pad + r, ATT_PREP_ROWS)
        qs_ref[rows, :] = head_rms(q_ref[0, rows, :], qg_ref[...]) * scale
        kn = head_rms(k_ref[0, rows, :], kg_ref[...])
        kp_ref[0, prows, :] = jnp.where(lo, kn, 0.0)
        kp_ref[1, prows, :] = jnp.where(lo, 0.0, kn)
        v = v_ref[0, rows, :]
        vp_ref[0, prows, :] = jnp.where(lo, v, jnp.where(lane == HEAD_DIM, 1.0, 0.0))
        vp_ref[1, prows, :] = jnp.where(lo, jnp.where(lane == 0, 1.0, 0.0), v)
        return carry

    lax.fori_loop(0, seq // ATT_PREP_ROWS, prep, 0)

    rowi = lax.broadcasted_iota(I32, (ATT_QB, ATT_KW), 0)
    coli = lax.broadcasted_iota(I32, (ATT_QB, ATT_KW), 1)
    arel = jnp.abs(coli - ATT_RAD - rowi)
    kcol = lax.broadcasted_iota(I32, (1, ATT_KW), 1) - ATT_RAD

    for hh in range(2):
        slope = slopes_ref[pair * 2 + hh]
        zlane = HEAD_DIM if hh == 0 else 0
        mlane = zlane + 1
        for bi, (window, dil) in enumerate(DILATED):
            assert window // (2 * dil) == ATT_RAD
            cls_len = seq // dil
            nbc = cls_len // ATT_QB
            bias = jnp.where(arel <= ATT_RAD, -(slope * dil) * arel.astype(F32), NEG_INF)

            def blk(j, carry, bi=bi, dil=dil, cls_len=cls_len, nbc=nbc, bias=bias, hh=hh, mlane=mlane):
                c = j // nbc
                n = j - c * nbc
                qstart = c + dil * ATT_QB * n
                kstart = pad + c + dil * (ATT_QB * n - ATT_RAD)
                if dil == 1:
                    qsl = pl.ds(qstart, ATT_QB)
                    ksl = pl.ds(kstart, ATT_KW)
                else:
                    qsl = pl.ds(qstart, ATT_QB, stride=dil)
                    ksl = pl.ds(kstart, ATT_KW, stride=dil)
                qb = qs_ref[qsl, :].astype(BF16)
                kw = kp_ref[hh, ksl, :].astype(BF16)
                vw = vp_ref[hh, ksl, :].astype(BF16)
                sc = lax.dot_general(qb, kw, (((1,), (1,)), ((), ())),
                                     preferred_element_type=F32) + bias
                kj = kcol + ATT_QB * n
                sc = jnp.where((kj >= 0) & (kj < cls_len), sc, NEG_INF)
                m = jnp.max(sc, axis=-1, keepdims=True)
                p = jnp.exp(sc - m)
                o = jnp.dot(p.astype(BF16), vw, preferred_element_type=F32)
                ob_ref[bi, qsl, :] = jnp.where(lane == mlane, m, o)
                return carry

            lax.fori_loop(0, seq // ATT_QB, blk, 0, unroll=ATT_UNROLL)

        def combine(i, carry, hh=hh, zlane=zlane, mlane=mlane):
            r = pl.multiple_of(i * ATT_PREP_ROWS, ATT_PREP_ROWS)
            rows = pl.ds(r, ATT_PREP_ROWS)
            parts = [ob_ref[bi, rows, :] for bi in range(len(DILATED))]
            ms = [jnp.sum(jnp.where(lane == mlane, o, 0.0), axis=-1, keepdims=True) for o in parts]
            mx = functools.reduce(jnp.maximum, ms)
            acc = sum(jnp.exp(m - mx) * o for m, o in zip(ms, parts))
            den = jnp.sum(jnp.where(lane == zlane, acc, 0.0), axis=-1, keepdims=True)
            res = acc / den
            if hh == 0:
                o_ref[0, rows, :] = res
            else:
                o_ref[0, rows, :] = jnp.where(lo, o_ref[0, rows, :], res)
            return carry

        lax.fori_loop(0, seq // ATT_PREP_ROWS, combine, 0)


def _dilated_attention(proj3, q_norm, k_norm):
    b, s, _ = proj3.shape
    pad = ATT_RAD * max(d for _, d in DILATED)
    npair = ATT_HEADS // 2
    qoff = 2 * CONV_CH // LANES
    slopes = jnp.asarray([2.0 ** (-8.0 * (h + 1) / ATT_HEADS) for h in range(ATT_HEADS)], F32)
    qg = jnp.tile(q_norm.reshape(1, HEAD_DIM), (1, 2))
    kg = jnp.tile(k_norm.reshape(1, HEAD_DIM), (1, 2))
    kernel = functools.partial(_attn_kernel, seq=s, pad=pad)
    return pl.pallas_call(
        kernel,
        grid=(b, npair),
        in_specs=[pl.BlockSpec(memory_space=pltpu.SMEM),
                  pl.BlockSpec((1, s, LANES), lambda i, j: (i, 0, qoff + j)),
                  pl.BlockSpec((1, s, LANES), lambda i, j: (i, 0, qoff + npair + j)),
                  pl.BlockSpec((1, s, LANES), lambda i, j: (i, 0, qoff + 2 * npair + j)),
                  pl.BlockSpec((1, LANES), lambda i, j: (0, 0)),
                  pl.BlockSpec((1, LANES), lambda i, j: (0, 0))],
        out_specs=pl.BlockSpec((1, s, LANES), lambda i, j: (i, 0, j)),
        out_shape=jax.ShapeDtypeStruct((b, s, ATT_WIDTH), F32),
        scratch_shapes=[pltpu.VMEM((s, LANES), F32),
                        pltpu.VMEM((2, s + 2 * pad, LANES), F32),
                        pltpu.VMEM((2, s + 2 * pad, LANES), F32),
                        pltpu.VMEM((len(DILATED), s, LANES), F32)],
        compiler_params=_params(("parallel", "parallel"), 48),
        name="dilated_attn",
    )(slopes, proj3, proj3, proj3, qg, kg)


def _outproj_kernel(x_ref, conv_ref, att_ref, g_ref, b_ref, w1_ref, w2_ref, o_ref):
    c = conv_ref[...]
    mu = jnp.mean(c, axis=-1, keepdims=True)
    cc = c - mu
    var = jnp.mean(cc * cc, axis=-1, keepdims=True)
    hn = cc * lax.rsqrt(var + EPS) * g_ref[...] + b_ref[...]
    a = hn * jax.nn.sigmoid(hn)
    y = jnp.dot(a.astype(BF16), w1_ref[...], preferred_element_type=F32)
    y = y + jnp.dot(att_ref[...].astype(BF16), w2_ref[...], preferred_element_type=F32)
    o_ref[...] = x_ref[...] + y


def _out_projection(x2d, conv2d, att2d, ln_g, ln_b, w_out_bf16, tm=512):
    m, d = x2d.shape
    row = lambda i: (i, 0)
    fixed = lambda i: (0, 0)
    return pl.pallas_call(
        _outproj_kernel,
        grid=(m // tm,),
        in_specs=[pl.BlockSpec((tm, d), row),
                  pl.BlockSpec((tm, CONV_CH), row),
                  pl.BlockSpec((tm, ATT_WIDTH), row),
                  pl.BlockSpec((1, CONV_CH), fixed),
                  pl.BlockSpec((1, CONV_CH), fixed),
                  pl.BlockSpec((CONV_CH, d), fixed),
                  pl.BlockSpec((ATT_WIDTH, d), lambda i: (1, 0))],
        out_specs=pl.BlockSpec((tm, d), row),
        out_shape=jax.ShapeDtypeStruct((m, d), F32),
        compiler_params=_params(("parallel",), 32),
        name="ln_out_proj",
    )(x2d, conv2d, att2d, ln_g.reshape(1, CONV_CH), ln_b.reshape(1, CONV_CH), w_out_bf16, w_out_bf16)


def _even_mixer(x, mix_g, w_in, conv_w, conv_b, ln_g, ln_b, q_norm, k_norm, w_out):
    b, s, d = x.shape
    x2d = x.reshape(b * s, d)
    proj = _norm_matmul(x2d, mix_g, w_in.astype(BF16))
    proj3 = proj.reshape(b, s, -1)
    conv = _conformer_conv(proj3, conv_w, conv_b)
    att = _dilated_attention(proj3, q_norm, k_norm)
    out = _out_projection(x2d, conv.reshape(b * s, CONV_CH), att.reshape(b * s, ATT_WIDTH),
                          ln_g, ln_b, w_out.astype(BF16))
    return out.reshape(b, s, d)


def _router_kernel(x_ref, g_ref, wr_ref, br_ref, h_ref, aff_ref):
    h = _rmsnorm(x_ref[0], g_ref[...])
    logits = jnp.dot(h, wr_ref[...], precision=lax.Precision.HIGHEST,
                     preferred_element_type=F32) + br_ref[...]
    lt = logits.T[:N_EXPERTS, :]
    mx = jnp.max(lt, axis=0, keepdims=True)
    ex = jnp.exp(lt - mx)
    aff_ref[0] = ex / jnp.sum(ex, axis=0, keepdims=True)
    h_ref[0] = h.astype(BF16)


def _router(x, g, w_router, b_router, tm=512):
    b, s, d = x.shape
    wr = jnp.zeros((d, LANES), F32).at[:, :N_EXPERTS].set(w_router)
    br = jnp.zeros((1, LANES), F32).at[0, :N_EXPERTS].set(b_router)
    return pl.pallas_call(
        _router_kernel,
        grid=(b, s // tm),
        in_specs=[pl.BlockSpec((1, tm, d), lambda i, j: (i, j, 0)),
                  pl.BlockSpec((1, d), lambda i, j: (0, 0)),
                  pl.BlockSpec((d, LANES), lambda i, j: (0, 0)),
                  pl.BlockSpec((1, LANES), lambda i, j: (0, 0))],
        out_specs=[pl.BlockSpec((1, tm, d), lambda i, j: (i, j, 0)),
                   pl.BlockSpec((1, N_EXPERTS, tm), lambda i, j: (i, 0, j))],
        out_shape=[jax.ShapeDtypeStruct((b, s, d), BF16),
                   jax.ShapeDtypeStruct((b, N_EXPERTS, s), F32)],
        compiler_params=_params(("parallel", "parallel"), 32),
        name="moe_router",
    )(x, g.reshape(1, d), wr, br)


def _select_kernel(aff_ref, pos_ref, post_ref, gselt_ref, cs_ref, *, cap):
    a = aff_ref[0]
    s = a.shape[1]
    capf = float(cap)

    def count(mask):
        return jnp.sum(jnp.where(mask, 1.0, 0.0), axis=-1, keepdims=True)

    def enough(cand):
        return count(a >= cand) >= capf

    tiny = float(2.0 ** F32_MIN_EXP)
    thr = jnp.full((N_EXPERTS, 1), tiny, F32)
    normal = enough(thr)
    for j in range(6, -1, -1):
        cand = thr * float(2.0 ** (2 ** j))
        thr = jnp.where(enough(cand), cand, thr)
    step = thr * 0.5
    for _ in range(F32_MANTISSA_BITS):
        cand = thr + step
        thr = jnp.where(enough(cand), cand, thr)
        step = step * 0.5
    thr = jnp.where(normal, thr, 0.0)

    ri = lax.broadcasted_iota(I32, (LANES, LANES), 0)
    ci = lax.broadcasted_iota(I32, (LANES, LANES), 1)
    tri = jnp.where(ri < ci, 1.0, 0.0).astype(BF16)

    def exclusive_cumsum(mask):
        ind = jnp.where(mask, 1.0, 0.0)
        off = jnp.zeros((N_EXPERTS, 1), F32)
        for j in range(s // LANES):
            xc = ind[:, j * LANES:(j + 1) * LANES]
            cs_ref[:, j * LANES:(j + 1) * LANES] = (
                jnp.dot(xc.astype(BF16), tri, preferred_element_type=F32) + off)
            off = off + jnp.sum(xc, axis=-1, keepdims=True)
        return cs_ref[...]

    gt = a > thr
    eq = a == thr
    need = capf - count(gt)
    tie_rank = exclusive_cumsum(eq)
    sel = gt | (eq & (tie_rank < need))
    slot = jnp.where(sel, exclusive_cumsum(sel), -1.0)
    pos_ref[0] = slot.astype(I32)
    fill = jnp.zeros((LANES - N_EXPERTS, s), F32)
    post_ref[0] = jnp.concatenate([slot, fill], axis=0).T
    gselt_ref[0] = jnp.concatenate([jnp.where(sel, a, 0.0), fill], axis=0).T


def _select(aff, cap):
    b, e, s = aff.shape
    blk = pl.BlockSpec((1, e, s), lambda i: (i, 0, 0))
    blk_t = pl.BlockSpec((1, s, LANES), lambda i: (i, 0, 0))
    return pl.pallas_call(
        functools.partial(_select_kernel, cap=cap),
        grid=(b,),
        in_specs=[blk],
        out_specs=[blk, blk_t, blk_t],
        out_shape=[jax.ShapeDtypeStruct((b, e, s), I32),
                   jax.ShapeDtypeStruct((b, s, LANES), F32),
                   jax.ShapeDtypeStruct((b, s, LANES), F32)],
        scratch_shapes=[pltpu.VMEM((e, s), F32)],
        compiler_params=_params(("parallel",), 32),
        name="moe_select",
    )(aff)


def _expert_kernel(h_ref, pos_ref, wg_ref, wu_ref, wd_ref, y_ref, *, cap):
    s = h_ref.shape[1]
    slot = lax.broadcasted_iota(I32, (cap, MOE_GATHER_TOKENS), 0)
    x = None
    for k in range(s // MOE_GATHER_TOKENS):
        tok = slice(k * MOE_GATHER_TOKENS, (k + 1) * MOE_GATHER_TOKENS)
        onehot = jnp.where(pos_ref[0, 0, :, tok] == slot, 1.0, 0.0).astype(BF16)
        part = jnp.dot(onehot, h_ref[0, tok, :], preferred_element_type=F32)
        x = part if x is None else x + part
    x = x.astype(BF16)
    g = jnp.dot(x, wg_ref[0].astype(BF16), preferred_element_type=F32)
    u = jnp.dot(x, wu_ref[0].astype(BF16), preferred_element_type=F32)
    a = (g * jax.nn.sigmoid(g) * u).astype(BF16)
    y_ref[0, 0] = jnp.dot(a, wd_ref[0].astype(BF16), preferred_element_type=F32).astype(BF16)


def _experts(h, pos4, w_gate, w_up, w_down, cap):
    b, s, d = h.shape
    e, _, f = w_gate.shape
    return pl.pallas_call(
        functools.partial(_expert_kernel, cap=cap),
        grid=(b, e),
        in_specs=[pl.BlockSpec((1, s, d), lambda i, j: (i, 0, 0), pipeline_mode=pl.Buffered(1)),
                  pl.BlockSpec((1, 1, 1, s), lambda i, j: (i, j, 0, 0)),
                  pl.BlockSpec((1, d, f), lambda i, j: (j, 0, 0)),
                  pl.BlockSpec((1, d, f), lambda i, j: (j, 0, 0)),
                  pl.BlockSpec((1, f, d), lambda i, j: (j, 0, 0))],
        out_specs=pl.BlockSpec((1, 1, cap, d), lambda i, j: (i, j, 0, 0)),
        out_shape=jax.ShapeDtypeStruct((b, e, cap, d), BF16),
        compiler_params=_params(("parallel", "parallel"), 56),
        name="moe_experts",
    )(h, pos4, w_gate, w_up, w_down)


def _scatter_kernel(x_ref, y_ref, post_ref, gselt_ref, o_ref, acc_ref, *, cap):
    e = pl.program_id(2)

    @pl.when(e == 0)
    def _():
        acc_ref[...] = jnp.zeros_like(acc_ref)

    lane = lax.broadcasted_iota(I32, (1, LANES), 1)
    mine = lane == e

    def column(ref):
        return jnp.sum(jnp.where(mine, ref[0], 0.0), axis=-1, keepdims=True)

    slot = lax.broadcasted_iota(I32, (1, cap), 1).astype(F32)
    onehot = jnp.where(column(post_ref) == slot, 1.0, 0.0).astype(BF16)
    contrib = jnp.dot(onehot, y_ref[0, 0], preferred_element_type=F32)
    acc_ref[...] += contrib * column(gselt_ref)

    @pl.when(e == pl.num_programs(2) - 1)
    def _():
        o_ref[0] = x_ref[0] + acc_ref[...]


def _scatter(x, y, post, gselt, cap, tn=1024):
    b, s, d = x.shape
    e = y.shape[1]
    return pl.pallas_call(
        functools.partial(_scatter_kernel, cap=cap),
        grid=(b, s // tn, e),
        in_specs=[pl.BlockSpec((1, tn, d), lambda i, j, k: (i, j, 0)),
                  pl.BlockSpec((1, 1, cap, d), lambda i, j, k: (i, k, 0, 0)),
                  pl.BlockSpec((1, tn, LANES), lambda i, j, k: (i, j, 0)),
                  pl.BlockSpec((1, tn, LANES), lambda i, j, k: (i, j, 0))],
        out_specs=pl.BlockSpec((1, tn, d), lambda i, j, k: (i, j, 0)),
        out_shape=jax.ShapeDtypeStruct((b, s, d), F32),
        scratch_shapes=[pltpu.VMEM((tn, d), F32)],
        compiler_params=_params(("parallel", "parallel", "arbitrary"), 48),
        name="moe_scatter",
    )(x, y, post, gselt)


def _moe(x, ffn_g, w_router, b_router, w_gate, w_up, w_down):
    b, s, d = x.shape
    cap = CAPACITY_FACTOR * s // N_EXPERTS
    h, aff = _router(x, ffn_g, w_router, b_router)
    pos, post, gselt = _select(aff, cap)
    y = _experts(h, pos.reshape(b, N_EXPERTS, 1, s), w_gate, w_up, w_down, cap)
    return _scatter(x, y, post, gselt, cap)


SSM_CHUNK = 16


def _norm_cast_kernel(x_ref, g_ref, o_ref):
    o_ref[...] = _rmsnorm(x_ref[...], g_ref[...]).astype(BF16)


def _norm_cast(x2d, g, tm=1024):
    m, d = x2d.shape
    return pl.pallas_call(
        _norm_cast_kernel,
        grid=(m // tm,),
        in_specs=[pl.BlockSpec((tm, d), lambda i: (i, 0)),
                  pl.BlockSpec((1, d), lambda i: (0, 0))],
        out_specs=pl.BlockSpec((tm, d), lambda i: (i, 0)),
        out_shape=jax.ShapeDtypeStruct((m, d), BF16),
        compiler_params=_params(("parallel",), 32),
        name="s5_norm",
    )(x2d, g.reshape(1, d))


def _s5_core_kernel(u_ref, m_ref, w_ref, v_ref, a_ref, y_ref, *, nc):
    u = u_ref[0]
    n_rows = u.shape[0]
    y = jnp.dot(u, m_ref[0], preferred_element_type=F32)
    x = jnp.dot(u, w_ref[0], preferred_element_type=F32)
    hr = x[:, :LANES]
    hi = x[:, LANES:]
    ar = a_ref[0, 0:1, :]
    ai = a_ref[0, 1:2, :]
    lane = lax.broadcasted_iota(I32, (1, LANES), 1)
    fwd = lane < SSM_STATE
    rown = lax.broadcasted_iota(I32, (n_rows, 1), 0) % nc

    def shifted(val, step):
        down = jnp.where(rown >= step, pltpu.roll(val, step, axis=0), 0.0)
        up = jnp.where(rown < nc - step, pltpu.roll(val, n_rows - step, axis=0), 0.0)
        return jnp.where(fwd, down, up)

    step = 1
    while step < nc:
        pr = shifted(hr, step)
        pi = shifted(hi, step)
        hr, hi = hr + ar * pr - ai * pi, hi + ar * pi + ai * pr
        ar, ai = ar * ar - ai * ai, 2.0 * ar * ai
        step *= 2
    hin_r = shifted(hr, 1).astype(BF16)
    hin_i = shifted(hi, 1).astype(BF16)
    y = y + jnp.dot(hin_r, v_ref[0, :LANES, :], preferred_element_type=F32)
    y = y + jnp.dot(hin_i, v_ref[0, LANES:, :], preferred_element_type=F32)
    y_ref[0] = y


def _s5_core(ug, mmat, wmat, vmat, amat, nc):
    g, n, k = ug.shape
    blk = lambda r, c: pl.BlockSpec((1, r, c), lambda i: (i, 0, 0))
    return pl.pallas_call(
        functools.partial(_s5_core_kernel, nc=nc),
        grid=(g,),
        in_specs=[blk(n, k), blk(k, k), blk(k, 2 * LANES), blk(2 * LANES, k), blk(8, LANES)],
        out_specs=blk(n, k),
        out_shape=jax.ShapeDtypeStruct((g, n, k), F32),
        compiler_params=_params(("parallel",), 48),
        name="s5_core",
    )(ug, mmat, wmat, vmat, amat)


def _s5_post_kernel(x_ref, g_ref, y_ref, d_ref, w_ref, o_ref):
    x = x_ref[...]
    d = x.shape[1]
    hf = _rmsnorm(x, g_ref[...])
    z = jax.nn.gelu(y_ref[...] + d_ref[...] * hf)
    vg = jnp.dot(z.astype(BF16), w_ref[...], preferred_element_type=F32)
    o_ref[...] = x + vg[:, :d] * jax.nn.sigmoid(vg[:, d:])


def _s5_post(x2d, g, y2d, d_skip, w_glu_bf16, tm=512):
    m, d = x2d.shape
    row = lambda i: (i, 0)
    fixed = lambda i: (0, 0)
    return pl.pallas_call(
        _s5_post_kernel,
        grid=(m // tm,),
        in_specs=[pl.BlockSpec((tm, d), row), pl.BlockSpec((1, d), fixed),
                  pl.BlockSpec((tm, d), row), pl.BlockSpec((1, d), fixed),
                  pl.BlockSpec((d, 2 * d), fixed)],
        out_specs=pl.BlockSpec((tm, d), row),
        out_shape=jax.ShapeDtypeStruct((m, d), F32),
        compiler_params=_params(("parallel",), 48),
        name="s5_gelu_glu",
    )(x2d, g.reshape(1, d), y2d, d_skip.reshape(1, d), w_glu_bf16)


def _s5_matrices(lam_re, lam_im, log_dt, b_re, b_im, c_re, c_im):
    t = SSM_CHUNK
    g, p = lam_re.shape[1:]
    c = SSM_GROUP
    dt = jnp.exp(log_dt)[..., None]
    den = lam_re * lam_re + lam_im * lam_im
    mag = jnp.exp(lam_re * dt)
    ang = lam_im * dt
    lb_re = mag * jnp.cos(ang)
    lb_im = mag * jnp.sin(ang)
    nr = lb_re - 1.0
    f_re = (nr * lam_re + lb_im * lam_im) / den
    f_im = (lb_im * lam_re - nr * lam_im) / den
    bb_re = f_re[..., None] * b_re - f_im[..., None] * b_im
    bb_im = f_re[..., None] * b_im + f_im[..., None] * b_re
    lag = jnp.arange(t + 1, dtype=F32)[:, None, None, None]
    pmag = jnp.exp(lag * (lam_re * dt))
    pw_re = pmag * jnp.cos(lag * ang)
    pw_im = pmag * jnp.sin(lag * ang)
    cl_re = c_re[None] * pw_re[..., None, :] - c_im[None] * pw_im[..., None, :]
    cl_im = c_re[None] * pw_im[..., None, :] + c_im[None] * pw_re[..., None, :]
    kern = (jnp.einsum('ldgcp,dgpe->ldgce', cl_re[:t], bb_re)
            - jnp.einsum('ldgcp,dgpe->ldgce', cl_im[:t], bb_im))
    tt = jnp.arange(t)
    diff = tt[None, :] - tt[:, None]
    kf = jnp.where((diff >= 0)[:, :, None, None, None], kern[jnp.clip(diff, 0, t - 1), 0], 0.0)
    kb = jnp.where((diff <= 0)[:, :, None, None, None], kern[jnp.clip(-diff, 0, t - 1), 1], 0.0)
    mmat = jnp.transpose(kf + kb, (2, 0, 4, 1, 3)).reshape(g, t * c, t * c)
    wf_re = pw_re[t - 1 - tt, 0][..., None] * bb_re[0][None]
    wf_im = pw_im[t - 1 - tt, 0][..., None] * bb_re[0][None]
    wf_re = wf_re - pw_im[t - 1 - tt, 0][..., None] * bb_im[0][None]
    wf_im = wf_im + pw_re[t - 1 - tt, 0][..., None] * bb_im[0][None]
    wb_re = pw_re[tt, 1][..., None] * bb_re[1][None] - pw_im[tt, 1][..., None] * bb_im[1][None]
    wb_im = pw_im[tt, 1][..., None] * bb_re[1][None] + pw_re[tt, 1][..., None] * bb_im[1][None]
    wmat = jnp.concatenate([wf_re, wb_re, wf_im, wb_im], axis=2)
    wmat = jnp.transpose(wmat, (1, 0, 3, 2)).reshape(g, t * c, 4 * p)
    vf_re = cl_re[tt + 1, 0]
    vf_im = -cl_im[tt + 1, 0]
    vb_re = cl_re[t - tt, 1]
    vb_im = -cl_im[t - tt, 1]
    vmat = jnp.concatenate([vf_re, vb_re, vf_im, vb_im], axis=3)
    vmat = jnp.transpose(vmat, (1, 3, 0, 2)).reshape(g, 4 * p, t * c)
    a_re = jnp.concatenate([pw_re[t, 0], pw_re[t, 1]], axis=-1)
    a_im = jnp.concatenate([pw_im[t, 0], pw_im[t, 1]], axis=-1)
    amat = jnp.zeros((g, 8, 2 * p), F32).at[:, 0].set(a_re).at[:, 1].set(a_im)
    return mmat.astype(BF16), wmat.astype(BF16), vmat.astype(BF16), amat


def _s5_mixer(x, mix_g, lam_re, lam_im, log_dt, b_re, b_im, c_re, c_im, d_skip, w_glu):
    b, s, d = x.shape
    t = SSM_CHUNK
    g = d // SSM_GROUP
    nc = s // t
    x2d = x.reshape(b * s, d)
    u = _norm_cast(x2d, mix_g)
    ug = jnp.transpose(u.reshape(b * nc, t, g, SSM_GROUP), (2, 0, 1, 3)).reshape(g, b * nc, t * SSM_GROUP)
    mmat, wmat, vmat, amat = _s5_matrices(lam_re, lam_im, log_dt, b_re, b_im, c_re, c_im)
    yg = _s5_core(ug, mmat, wmat, vmat, amat, nc)
    y2d = jnp.transpose(yg.reshape(g, b * nc, t, SSM_GROUP), (1, 2, 0, 3)).reshape(b * s, d)
    out = _s5_post(x2d, mix_g, y2d, d_skip, w_glu.astype(BF16))
    return out.reshape(b, s, d)


def kernel(x, mix_norm_even, w_in, conv_w, conv_b, conv_ln_g, conv_ln_b, q_norm, k_norm, w_out, mix_norm_odd, ssm_lam_re, ssm_lam_im, ssm_log_dt, ssm_b_re, ssm_b_im, ssm_c_re, ssm_c_im, ssm_d, w_glu, ffn_norm, w_router, b_router, w_e_gate, w_e_up, w_e_down):
    depth = ffn_norm.shape[0]
    for layer in range(depth):
        i = layer // 2
        if layer % 2 == 0:
            x = _even_mixer(x, mix_norm_even[i], w_in[i], conv_w[i], conv_b[i], conv_ln_g[i],
                            conv_ln_b[i], q_norm[i], k_norm[i], w_out[i])
        else:
            x = _s5_mixer(x, mix_norm_odd[i], ssm_lam_re[i], ssm_lam_im[i], ssm_log_dt[i],
                          ssm_b_re[i], ssm_b_im[i], ssm_c_re[i], ssm_c_im[i], ssm_d[i], w_glu[i])
        x = _moe(x, ffn_norm[layer], w_router[layer], b_router[layer],
                 w_e_gate[layer], w_e_up[layer], w_e_down[layer])
    return x
```

```python
import functools
import math

import jax
import jax.numpy as jnp
from jax import lax
from jax.experimental import pallas as pl
from jax.experimental.pallas import tpu as pltpu

F32 = jnp.float32
BF16 = jnp.bfloat16
I32 = jnp.int32

EPS = 1e-6
NEG_INF = -1e30

CONV_CH = 512
CONV_WIDTH = 31
ATT_HEADS = 8
HEAD_DIM = 64
ATT_WIDTH = ATT_HEADS * HEAD_DIM
DILATED = ((128, 1), (512, 4), (2048, 16))
SSM_GROUP = 16
SSM_STATE = 64
N_EXPERTS = 16
CAPACITY_FACTOR = 2
MOE_GATHER_TOKENS = 1024

LANES = 128
F32_MIN_EXP = -126
F32_MANTISSA_BITS = 23
MIB = 1024 * 1024


def _params(sem, vmem_mib):
    return pltpu.CompilerParams(dimension_semantics=sem, vmem_limit_bytes=vmem_mib * MIB)


def _rmsnorm(x, g):
    return x * lax.rsqrt(jnp.mean(x * x, axis=-1, keepdims=True) + EPS) * g


def _norm_matmul_kernel(x_ref, g_ref, w_ref, o_ref):
    y = _rmsnorm(x_ref[...], g_ref[...])
    o_ref[...] = jnp.dot(y.astype(BF16), w_ref[...], preferred_element_type=F32)


def _norm_matmul(x2d, g, w_bf16, tm=512):
    m, d = x2d.shape
    n = w_bf16.shape[1]
    return pl.pallas_call(
        _norm_matmul_kernel,
        grid=(m // tm,),
        in_specs=[pl.BlockSpec((tm, d), lambda i: (i, 0)),
                  pl.BlockSpec((1, d), lambda i: (0, 0)),
                  pl.BlockSpec((d, n), lambda i: (0, 0))],
        out_specs=pl.BlockSpec((tm, n), lambda i: (i, 0)),
        out_shape=jax.ShapeDtypeStruct((m, n), F32),
        compiler_params=_params(("parallel",), 48),
        name="norm_in_proj",
    )(x2d, g.reshape(1, d), w_bf16)


CONV_HALO = 16
CONV_ROWS = 256


def _conv_kernel(val_ref, gate_ref, w_ref, b_ref, o_ref, hp_ref):
    s = val_ref.shape[1]
    zeros = jnp.zeros((CONV_HALO, LANES), F32)
    hp_ref[0:CONV_HALO, :] = zeros
    hp_ref[CONV_HALO + s:2 * CONV_HALO + s, :] = zeros
    hp_ref[CONV_HALO:CONV_HALO + s, :] = val_ref[0] * jax.nn.sigmoid(gate_ref[0])
    first = CONV_HALO - CONV_WIDTH // 2
    for c in range(s // CONV_ROWS):
        r0 = c * CONV_ROWS
        acc = jnp.broadcast_to(b_ref[...], (CONV_ROWS, LANES))
        for k in range(CONV_WIDTH):
            acc = acc + w_ref[k:k + 1, :] * hp_ref[r0 + first + k:r0 + first + k + CONV_ROWS, :]
        o_ref[0, r0:r0 + CONV_ROWS, :] = acc


def _conformer_conv(proj3, conv_w, conv_b):
    b, s, _ = proj3.shape
    nct = CONV_CH // LANES
    return pl.pallas_call(
        _conv_kernel,
        grid=(b, nct),
        in_specs=[pl.BlockSpec((1, s, LANES), lambda i, j: (i, 0, j)),
                  pl.BlockSpec((1, s, LANES), lambda i, j: (i, 0, nct + j)),
                  pl.BlockSpec((CONV_WIDTH, LANES), lambda i, j: (0, j)),
                  pl.BlockSpec((1, LANES), lambda i, j: (0, j))],
        out_specs=pl.BlockSpec((1, s, LANES), lambda i, j: (i, 0, j)),
        out_shape=jax.ShapeDtypeStruct((b, s, CONV_CH), F32),
        scratch_shapes=[pltpu.VMEM((s + 2 * CONV_HALO, LANES), F32)],
        compiler_params=_params(("parallel", "parallel"), 32),
        name="glu_dwconv",
    )(proj3, proj3, conv_w, conv_b.reshape(1, CONV_CH))


ATT_QB = 128
ATT_RAD = 64
ATT_KW = ATT_QB + 2 * ATT_RAD
ATT_PREP_ROWS = 512
ATT_UNROLL = 8


def _attn_kernel(slopes_ref, q_ref, k_ref, v_ref, qg_ref, kg_ref, o_ref,
                 qs_ref, kp_ref, vp_ref, ob_ref, *, seq, pad):
    pair = pl.program_id(1)
    lane = lax.broadcasted_iota(I32, (1, LANES), 1)
    lo = lane < HEAD_DIM
    scale = 1.0 / math.sqrt(HEAD_DIM)

    zpad = jnp.zeros((pad, LANES), F32)
    for hh in range(2):
        kp_ref[hh, 0:pad, :] = zpad
        kp_ref[hh, pad + seq:2 * pad + seq, :] = zpad
        vp_ref[hh, 0:pad, :] = zpad
        vp_ref[hh, pad + seq:2 * pad + seq, :] = zpad

    def head_rms(x, g):
        x2 = x * x
        s_lo = jnp.sum(jnp.where(lo, x2, 0.0), axis=-1, keepdims=True)
        s_hi = jnp.sum(jnp.where(lo, 0.0, x2), axis=-1, keepdims=True)
        ms = jnp.where(lo, s_lo, s_hi) * (1.0 / HEAD_DIM)
        return x * lax.rsqrt(ms + EPS) * g

    def prep(i, carry):
        r = pl.multiple_of(i * ATT_PREP_ROWS, ATT_PREP_ROWS)
        rows = pl.ds(r, ATT_PREP_ROWS)
        prows = pl.ds(pad + r, ATT_PREP_ROWS)
        qs_ref[rows, :] = head_rms(q_ref[0, rows, :], qg_ref[...]) * scale
        kn = head_rms(k_ref[0, rows, :], kg_ref[...])
        kp_ref[0, prows, :] = jnp.where(lo, kn, 0.0)
        kp_ref[1, prows, :] = jnp.where(lo, 0.0, kn)
        v = v_ref[0, rows, :]
        vp_ref[0, prows, :] = jnp.where(lo, v, jnp.where(lane == HEAD_DIM, 1.0, 0.0))
        vp_ref[1, prows, :] = jnp.where(lo, jnp.where(lane == 0, 1.0, 0.0), v)
        return carry

    lax.fori_loop(0, seq // ATT_PREP_ROWS, prep, 0)

    rowi = lax.broadcasted_iota(I32, (ATT_QB, ATT_KW), 0)
    coli = lax.broadcasted_iota(I32, (ATT_QB, ATT_KW), 1)
    arel = jnp.abs(coli - ATT_RAD - rowi)
    kcol = lax.broadcasted_iota(I32, (1, ATT_KW), 1) - ATT_RAD

    for hh in range(2):
        slope = slopes_ref[pair * 2 + hh]
        zlane = HEAD_DIM if hh == 0 else 0
        mlane = zlane + 1
        for bi, (window, dil) in enumerate(DILATED):
            assert window // (2 * dil) == ATT_RAD
            cls_len = seq // dil
            nbc = cls_len // ATT_QB
            bias = jnp.where(arel <= ATT_RAD, -(slope * dil) * arel.astype(F32), NEG_INF)

            def blk(j, carry, bi=bi, dil=dil, cls_len=cls_len, nbc=nbc, bias=bias, hh=hh, mlane=mlane):
                c = j // nbc
                n = j - c * nbc
                qstart = c + dil * ATT_QB * n
                kstart = pad + c + dil * (ATT_QB * n - ATT_RAD)
                if dil == 1:
                    qsl = pl.ds(qstart, ATT_QB)
                    ksl = pl.ds(kstart, ATT_KW)
                else:
                    qsl = pl.ds(qstart, ATT_QB, stride=dil)
                    ksl = pl.ds(kstart, ATT_KW, stride=dil)
                qb = qs_ref[qsl, :].astype(BF16)
                kw = kp_ref[hh, ksl, :].astype(BF16)
                vw = vp_ref[hh, ksl, :].astype(BF16)
                sc = lax.dot_general(qb, kw, (((1,), (1,)), ((), ())),
                                     preferred_element_type=F32) + bias
                kj = kcol + ATT_QB * n
                sc = jnp.where((kj >= 0) & (kj < cls_len), sc, NEG_INF)
                m = jnp.max(sc, axis=-1, keepdims=True)
                p = jnp.exp(sc - m)
                o = jnp.dot(p.astype(BF16), vw, preferred_element_type=F32)
                ob_ref[bi, qsl, :] = jnp.where(lane == mlane, m, o)
                return carry

            lax.fori_loop(0, seq // ATT_QB, blk, 0, unroll=ATT_UNROLL)

        def combine(i, carry, hh=hh, zlane=zlane, mlane=mlane):
            r = pl.multiple_of(i * ATT_PREP_ROWS, ATT_PREP_ROWS)
            rows = pl.ds(r, ATT_PREP_ROWS)
            parts = [ob_ref[bi, rows, :] for bi in range(len(DILATED))]
            ms = [jnp.sum(jnp.where(lane == mlane, o, 0.0), axis=-1, keepdims=True) for o in parts]
            mx = functools.reduce(jnp.maximum, ms)
            acc = sum(jnp.exp(m - mx) * o for m, o in zip(ms, parts))
            den = jnp.sum(jnp.where(lane == zlane, acc, 0.0), axis=-1, keepdims=True)
            res = acc / den
            if hh == 0:
                o_ref[0, rows, :] = res
            else:
                o_ref[0, rows, :] = jnp.where(lo, o_ref[0, rows, :], res)
            return carry

        lax.fori_loop(0, seq // ATT_PREP_ROWS, combine, 0)


def _dilated_attention(proj3, q_norm, k_norm):
    b, s, _ = proj3.shape
    pad = ATT_RAD * max(d for _, d in DILATED)
    npair = ATT_HEADS // 2
    qoff = 2 * CONV_CH // LANES
    slopes = jnp.asarray([2.0 ** (-8.0 * (h + 1) / ATT_HEADS) for h in range(ATT_HEADS)], F32)
    qg = jnp.tile(q_norm.reshape(1, HEAD_DIM), (1, 2))
    kg = jnp.tile(k_norm.reshape(1, HEAD_DIM), (1, 2))
    kernel = functools.partial(_attn_kernel, seq=s, pad=pad)
    return pl.pallas_call(
        kernel,
        grid=(b, npair),
        in_specs=[pl.BlockSpec(memory_space=pltpu.SMEM),
                  pl.BlockSpec((1, s, LANES), lambda i, j: (i, 0, qoff + j)),
                  pl.BlockSpec((1, s, LANES), lambda i, j: (i, 0, qoff + npair + j)),
                  pl.BlockSpec((1, s, LANES), lambda i, j: (i, 0, qoff + 2 * npair + j)),
                  pl.BlockSpec((1, LANES), lambda i, j: (0, 0)),
                  pl.BlockSpec((1, LANES), lambda i, j: (0, 0))],
        out_specs=pl.BlockSpec((1, s, LANES), lambda i, j: (i, 0, j)),
        out_shape=jax.ShapeDtypeStruct((b, s, ATT_WIDTH), F32),
        scratch_shapes=[pltpu.VMEM((s, LANES), F32),
                        pltpu.VMEM((2, s + 2 * pad, LANES), F32),
                        pltpu.VMEM((2, s + 2 * pad, LANES), F32),
                        pltpu.VMEM((len(DILATED), s, LANES), F32)],
        compiler_params=_params(("parallel", "parallel"), 48),
        name="dilated_attn",
    )(slopes, proj3, proj3, proj3, qg, kg)


def _outproj_kernel(x_ref, conv_ref, att_ref, g_ref, b_ref, w1_ref, w2_ref, o_ref):
    c = conv_ref[...]
    mu = jnp.mean(c, axis=-1, keepdims=True)
    cc = c - mu
    var = jnp.mean(cc * cc, axis=-1, keepdims=True)
    hn = cc * lax.rsqrt(var + EPS) * g_ref[...] + b_ref[...]
    a = hn * jax.nn.sigmoid(hn)
    y = jnp.dot(a.astype(BF16), w1_ref[...], preferred_element_type=F32)
    y = y + jnp.dot(att_ref[...].astype(BF16), w2_ref[...], preferred_element_type=F32)
    o_ref[...] = x_ref[...] + y


def _out_projection(x2d, conv2d, att2d, ln_g, ln_b, w_out_bf16, tm=512):
    m, d = x2d.shape
    row = lambda i: (i, 0)
    fixed = lambda i: (0, 0)
    return pl.pallas_call(
        _outproj_kernel,
        grid=(m // tm,),
        in_specs=[pl.BlockSpec((tm, d), row),
                  pl.BlockSpec((tm, CONV_CH), row),
                  pl.BlockSpec((tm, ATT_WIDTH), row),
                  pl.BlockSpec((1, CONV_CH), fixed),
                  pl.BlockSpec((1, CONV_CH), fixed),
                  pl.BlockSpec((CONV_CH, d), fixed),
                  pl.BlockSpec((ATT_WIDTH, d), lambda i: (1, 0))],
        out_specs=pl.BlockSpec((tm, d), row),
        out_shape=jax.ShapeDtypeStruct((m, d), F32),
        compiler_params=_params(("parallel",), 32),
        name="ln_out_proj",
    )(x2d, conv2d, att2d, ln_g.reshape(1, CONV_CH), ln_b.reshape(1, CONV_CH), w_out_bf16, w_out_bf16)


def _even_mixer(x, mix_g, w_in, conv_w, conv_b, ln_g, ln_b, q_norm, k_norm, w_out):
    b, s, d = x.shape
    x2d = x.reshape(b * s, d)
    proj = _norm_matmul(x2d, mix_g, w_in.astype(BF16))
    proj3 = proj.reshape(b, s, -1)
    conv = _conformer_conv(proj3, conv_w, conv_b)
    att = _dilated_attention(proj3, q_norm, k_norm)
    out = _out_projection(x2d, conv.reshape(b * s, CONV_CH), att.reshape(b * s, ATT_WIDTH),
                          ln_g, ln_b, w_out.astype(BF16))
    return out.reshape(b, s, d)


def _router_kernel(x_ref, g_ref, wr_ref, br_ref, h_ref, aff_ref):
    h = _rmsnorm(x_ref[0], g_ref[...])
    logits = jnp.dot(h, wr_ref[...], precision=lax.Precision.HIGHEST,
                     preferred_element_type=F32) + br_ref[...]
    lt = logits.T[:N_EXPERTS, :]
    mx = jnp.max(lt, axis=0, keepdims=True)
    ex = jnp.exp(lt - mx)
    aff_ref[0] = ex / jnp.sum(ex, axis=0, keepdims=True)
    h_ref[0] = h.astype(BF16)


def _router(x, g, w_router, b_router, tm=512):
    b, s, d = x.shape
    wr = jnp.zeros((d, LANES), F32).at[:, :N_EXPERTS].set(w_router)
    br = jnp.zeros((1, LANES), F32).at[0, :N_EXPERTS].set(b_router)
    return pl.pallas_call(
        _router_kernel,
        grid=(b, s // tm),
        in_specs=[pl.BlockSpec((1, tm, d), lambda i, j: (i, j, 0)),
                  pl.BlockSpec((1, d), lambda i, j: (0, 0)),
                  pl.BlockSpec((d, LANES), lambda i, j: (0, 0)),
                  pl.BlockSpec((1, LANES), lambda i, j: (0, 0))],
        out_specs=[pl.BlockSpec((1, tm, d), lambda i, j: (i, j, 0)),
                   pl.BlockSpec((1, N_EXPERTS, tm), lambda i, j: (i, 0, j))],
        out_shape=[jax.ShapeDtypeStruct((b, s, d), BF16),
                   jax.ShapeDtypeStruct((b, N_EXPERTS, s), F32)],
        compiler_params=_params(("parallel", "parallel"), 32),
        name="moe_router",
    )(x, g.reshape(1, d), wr, br)


def _select_kernel(aff_ref, pos_ref, post_ref, gselt_ref, cs_ref, *, cap):
    a = aff_ref[0]
    s = a.shape[1]
    capf = float(cap)

    def count(mask):
        return jnp.sum(jnp.where(mask, 1.0, 0.0), axis=-1, keepdims=True)

    def enough(cand):
        return count(a >= cand) >= capf

    tiny = float(2.0 ** F32_MIN_EXP)
    thr = jnp.full((N_EXPERTS, 1), tiny, F32)
    normal = enough(thr)
    for j in range(6, -1, -1):
        cand = thr * float(2.0 ** (2 ** j))
        thr = jnp.where(enough(cand), cand, thr)
    step = thr * 0.5
    for _ in range(F32_MANTISSA_BITS):
        cand = thr + step
        thr = jnp.where(enough(cand), cand, thr)
        step = step * 0.5
    thr = jnp.where(normal, thr, 0.0)

    ri = lax.broadcasted_iota(I32, (LANES, LANES), 0)
    ci = lax.broadcasted_iota(I32, (LANES, LANES), 1)
    tri = jnp.where(ri < ci, 1.0, 0.0).astype(BF16)

    def exclusive_cumsum(mask):
        ind = jnp.where(mask, 1.0, 0.0)
        off = jnp.zeros((N_EXPERTS, 1), F32)
        for j in range(s // LANES):
            xc = ind[:, j * LANES:(j + 1) * LANES]
            cs_ref[:, j * LANES:(j + 1) * LANES] = (
                jnp.dot(xc.astype(BF16), tri, preferred_element_type=F32) + off)
            off = off + jnp.sum(xc, axis=-1, keepdims=True)
        return cs_ref[...]

    gt = a > thr
    eq = a == thr
    need = capf - count(gt)
    tie_rank = exclusive_cumsum(eq)
    sel = gt | (eq & (tie_rank < need))
    slot = jnp.where(sel, exclusive_cumsum(sel), -1.0)
    pos_ref[0] = slot.astype(I32)
    fill = jnp.zeros((LANES - N_EXPERTS, s), F32)
    post_ref[0] = jnp.concatenate([slot, fill], axis=0).T
    gselt_ref[0] = jnp.concatenate([jnp.where(sel, a, 0.0), fill], axis=0).T


def _select(aff, cap):
    b, e, s = aff.shape
    blk = pl.BlockSpec((1, e, s), lambda i: (i, 0, 0))
    blk_t = pl.BlockSpec((1, s, LANES), lambda i: (i, 0, 0))
    return pl.pallas_call(
        functools.partial(_select_kernel, cap=cap),
        grid=(b,),
        in_specs=[blk],
        out_specs=[blk, blk_t, blk_t],
        out_shape=[jax.ShapeDtypeStruct((b, e, s), I32),
                   jax.ShapeDtypeStruct((b, s, LANES), F32),
                   jax.ShapeDtypeStruct((b, s, LANES), F32)],
        scratch_shapes=[pltpu.VMEM((e, s), F32)],
        compiler_params=_params(("parallel",), 32),
        name="moe_select",
    )(aff)


def _expert_kernel(h_ref, pos_ref, wg_ref, wu_ref, wd_ref, y_ref, *, cap):
    s = h_ref.shape[1]
    slot = lax.broadcasted_iota(I32, (cap, MOE_GATHER_TOKENS), 0)
    x = None
    for k in range(s // MOE_GATHER_TOKENS):
        tok = slice(k * MOE_GATHER_TOKENS, (k + 1) * MOE_GATHER_TOKENS)
        onehot = jnp.where(pos_ref[0, 0, :, tok] == slot, 1.0, 0.0).astype(BF16)
        part = jnp.dot(onehot, h_ref[0, tok, :], preferred_element_type=F32)
        x = part if x is None else x + part
    x = x.astype(BF16)
    g = jnp.dot(x, wg_ref[0, 0].astype(BF16), preferred_element_type=F32)
    u = jnp.dot(x, wu_ref[0, 0].astype(BF16), preferred_element_type=F32)
    a = (g * jax.nn.sigmoid(g) * u).astype(BF16)
    y_ref[0, 0] = jnp.dot(a, wd_ref[0, 0].astype(BF16), preferred_element_type=F32).astype(BF16)


def _experts(h, pos4, w_gate, w_up, w_down, layer, cap):
    b, s, d = h.shape
    _, e, _, f = w_gate.shape
    return pl.pallas_call(
        functools.partial(_expert_kernel, cap=cap),
        grid=(b, e),
        in_specs=[pl.BlockSpec((1, s, d), lambda i, j: (i, 0, 0), pipeline_mode=pl.Buffered(1)),
                  pl.BlockSpec((1, 1, 1, s), lambda i, j: (i, j, 0, 0)),
                  pl.BlockSpec((1, 1, d, f), lambda i, j: (layer, j, 0, 0)),
                  pl.BlockSpec((1, 1, d, f), lambda i, j: (layer, j, 0, 0)),
                  pl.BlockSpec((1, 1, f, d), lambda i, j: (layer, j, 0, 0))],
        out_specs=pl.BlockSpec((1, 1, cap, d), lambda i, j: (i, j, 0, 0)),
        out_shape=jax.ShapeDtypeStruct((b, e, cap, d), BF16),
        compiler_params=_params(("parallel", "parallel"), 56),
        name="moe_experts",
    )(h, pos4, w_gate, w_up, w_down)


def _scatter_kernel(x_ref, y_ref, post_ref, gselt_ref, o_ref, acc_ref, *, cap):
    e = pl.program_id(2)

    @pl.when(e == 0)
    def _():
        acc_ref[...] = jnp.zeros_like(acc_ref)

    lane = lax.broadcasted_iota(I32, (1, LANES), 1)
    mine = lane == e

    def column(ref):
        return jnp.sum(jnp.where(mine, ref[0], 0.0), axis=-1, keepdims=True)

    slot = lax.broadcasted_iota(I32, (1, cap), 1).astype(F32)
    onehot = jnp.where(column(post_ref) == slot, 1.0, 0.0).astype(BF16)
    contrib = jnp.dot(onehot, y_ref[0, 0], preferred_element_type=F32)
    acc_ref[...] += contrib * column(gselt_ref)

    @pl.when(e == pl.num_programs(2) - 1)
    def _():
        o_ref[0] = x_ref[0] + acc_ref[...]


def _scatter(x, y, post, gselt, cap, tn=1024):
    b, s, d = x.shape
    e = y.shape[1]
    return pl.pallas_call(
        functools.partial(_scatter_kernel, cap=cap),
        grid=(b, s // tn, e),
        in_specs=[pl.BlockSpec((1, tn, d), lambda i, j, k: (i, j, 0)),
                  pl.BlockSpec((1, 1, cap, d), lambda i, j, k: (i, k, 0, 0)),
                  pl.BlockSpec((1, tn, LANES), lambda i, j, k: (i, j, 0)),
                  pl.BlockSpec((1, tn, LANES), lambda i, j, k: (i, j, 0))],
        out_specs=pl.BlockSpec((1, tn, d), lambda i, j, k: (i, j, 0)),
        out_shape=jax.ShapeDtypeStruct((b, s, d), F32),
        scratch_shapes=[pltpu.VMEM((tn, d), F32)],
        compiler_params=_params(("parallel", "parallel", "arbitrary"), 48),
        name="moe_scatter",
    )(x, y, post, gselt)


def _moe(x, ffn_g, w_router, b_router, w_gate, w_up, w_down, layer):
    b, s, d = x.shape
    cap = CAPACITY_FACTOR * s // N_EXPERTS
    h, aff = _router(x, ffn_g, w_router, b_router)
    pos, post, gselt = _select(aff, cap)
    y = _experts(h, pos.reshape(b, N_EXPERTS, 1, s), w_gate, w_up, w_down, layer, cap)
    return _scatter(x, y, post, gselt, cap)


SSM_CHUNK = 16


SSM_SLAB_GROUPS = LANES // SSM_GROUP
SSM_CHUNK_LANES = SSM_CHUNK * SSM_GROUP
SSM_HALF_STEPS = LANES // SSM_GROUP


def _lane_group():
    return lax.broadcasted_iota(I32, (1, LANES), 1) // SSM_GROUP


def _norm_flatten_kernel(x_ref, g_ref, ug_ref, u_scr):
    tm, d = x_ref.shape
    nch = tm // SSM_CHUNK
    u = _rmsnorm(x_ref[...], g_ref[...])
    lgrp = _lane_group()
    for slab in range(d // LANES):
        u_scr[slab] = u[:, slab * LANES:(slab + 1) * LANES]
        steps = [u_scr[slab, pl.ds(t, nch, stride=SSM_CHUNK), :] for t in range(SSM_CHUNK)]
        for gi in range(SSM_SLAB_GROUPS):
            for half in range(SSM_CHUNK_LANES // LANES):
                acc = None
                for tt in range(SSM_HALF_STEPS):
                    shift = ((tt - gi) * SSM_GROUP) % LANES
                    piece = steps[half * SSM_HALF_STEPS + tt]
                    if shift:
                        piece = pltpu.roll(piece, shift, axis=1)
                    acc = piece if acc is None else jnp.where(lgrp == tt, piece, acc)
                ug_ref[slab * SSM_SLAB_GROUPS + gi, :, half * LANES:(half + 1) * LANES] = acc.astype(BF16)


def _norm_flatten(x2d, g, tm=512):
    m, d = x2d.shape
    ngrp = d // SSM_GROUP
    nch = tm // SSM_CHUNK
    return pl.pallas_call(
        _norm_flatten_kernel,
        grid=(m // tm,),
        in_specs=[pl.BlockSpec((tm, d), lambda i: (i, 0)),
                  pl.BlockSpec((1, d), lambda i: (0, 0))],
        out_specs=pl.BlockSpec((ngrp, nch, SSM_CHUNK_LANES), lambda i: (0, i, 0)),
        out_shape=jax.ShapeDtypeStruct((ngrp, m // SSM_CHUNK, SSM_CHUNK_LANES), BF16),
        scratch_shapes=[pltpu.VMEM((d // LANES, tm, LANES), F32)],
        compiler_params=_params(("parallel",), 32),
        name="s5_norm_flatten",
    )(x2d, g.reshape(1, d))


def _s5_core_kernel(u_ref, m_ref, w_ref, v_ref, a_ref, y_ref, *, nc):
    u = u_ref[0]
    n_rows = u.shape[0]
    y = jnp.dot(u, m_ref[0], preferred_element_type=F32)
    x = jnp.dot(u, w_ref[0], preferred_element_type=F32)
    hr = x[:, :LANES]
    hi = x[:, LANES:]
    ar = a_ref[0, 0:1, :]
    ai = a_ref[0, 1:2, :]
    lane = lax.broadcasted_iota(I32, (1, LANES), 1)
    fwd = lane < SSM_STATE
    rown = lax.broadcasted_iota(I32, (n_rows, 1), 0) % nc

    def shifted(val, step):
        down = jnp.where(rown >= step, pltpu.roll(val, step, axis=0), 0.0)
        up = jnp.where(rown < nc - step, pltpu.roll(val, n_rows - step, axis=0), 0.0)
        return jnp.where(fwd, down, up)

    step = 1
    while step < nc:
        pr = shifted(hr, step)
        pi = shifted(hi, step)
        hr, hi = hr + ar * pr - ai * pi, hi + ar * pi + ai * pr
        ar, ai = ar * ar - ai * ai, 2.0 * ar * ai
        step *= 2
    hin_r = shifted(hr, 1).astype(BF16)
    hin_i = shifted(hi, 1).astype(BF16)
    y = y + jnp.dot(hin_r, v_ref[0, :LANES, :], preferred_element_type=F32)
    y = y + jnp.dot(hin_i, v_ref[0, LANES:, :], preferred_element_type=F32)
    y_ref[0] = y


def _s5_core(ug, mmat, wmat, vmat, amat, nc):
    g, n, k = ug.shape
    blk = lambda r, c: pl.BlockSpec((1, r, c), lambda i: (i, 0, 0))
    return pl.pallas_call(
        functools.partial(_s5_core_kernel, nc=nc),
        grid=(g,),
        in_specs=[blk(n, k), blk(k, k), blk(k, 2 * LANES), blk(2 * LANES, k), blk(8, LANES)],
        out_specs=blk(n, k),
        out_shape=jax.ShapeDtypeStruct((g, n, k), F32),
        compiler_params=_params(("parallel",), 48),
        name="s5_core",
    )(ug, mmat, wmat, vmat, amat)


def _s5_post_kernel(x_ref, g_ref, yg_ref, d_ref, w_ref, o_ref, y_scr):
    x = x_ref[...]
    tm, d = x.shape
    nch = tm // SSM_CHUNK
    lgrp = _lane_group()
    for slab in range(d // LANES):
        for t in range(SSM_CHUNK):
            half, tt = divmod(t, SSM_HALF_STEPS)
            acc = None
            for gi in range(SSM_SLAB_GROUPS):
                piece = yg_ref[slab * SSM_SLAB_GROUPS + gi, :, half * LANES:(half + 1) * LANES]
                shift = ((gi - tt) * SSM_GROUP) % LANES
                if shift:
                    piece = pltpu.roll(piece, shift, axis=1)
                acc = piece if acc is None else jnp.where(lgrp == gi, piece, acc)
            y_scr[slab, pl.ds(t, nch, stride=SSM_CHUNK), :] = acc
    y = jnp.concatenate([y_scr[slab] for slab in range(d // LANES)], axis=1)
    hf = _rmsnorm(x, g_ref[...])
    z = jax.nn.gelu(y + d_ref[...] * hf)
    vg = jnp.dot(z.astype(BF16), w_ref[...], preferred_element_type=F32)
    o_ref[...] = x + vg[:, :d] * jax.nn.sigmoid(vg[:, d:])


def _s5_post(x2d, g, yg, d_skip, w_glu_bf16, tm=512):
    m, d = x2d.shape
    ngrp = d // SSM_GROUP
    nch = tm // SSM_CHUNK
    row = lambda i: (i, 0)
    fixed = lambda i: (0, 0)
    return pl.pallas_call(
        _s5_post_kernel,
        grid=(m // tm,),
        in_specs=[pl.BlockSpec((tm, d), row), pl.BlockSpec((1, d), fixed),
                  pl.BlockSpec((ngrp, nch, SSM_CHUNK_LANES), lambda i: (0, i, 0)),
                  pl.BlockSpec((1, d), fixed),
                  pl.BlockSpec((d, 2 * d), fixed)],
        out_specs=pl.BlockSpec((tm, d), row),
        out_shape=jax.ShapeDtypeStruct((m, d), F32),
        scratch_shapes=[pltpu.VMEM((d // LANES, tm, LANES), F32)],
        compiler_params=_params(("parallel",), 48),
        name="s5_gelu_glu",
    )(x2d, g.reshape(1, d), yg, d_skip.reshape(1, d), w_glu_bf16)


def _s5_matrices(lam_re, lam_im, log_dt, b_re, b_im, c_re, c_im):
    t = SSM_CHUNK
    g, p = lam_re.shape[1:]
    c = SSM_GROUP
    dt = jnp.exp(log_dt)[..., None]
    den = lam_re * lam_re + lam_im * lam_im
    mag = jnp.exp(lam_re * dt)
    ang = lam_im * dt
    lb_re = mag * jnp.cos(ang)
    lb_im = mag * jnp.sin(ang)
    nr = lb_re - 1.0
    f_re = (nr * lam_re + lb_im * lam_im) / den
    f_im = (lb_im * lam_re - nr * lam_im) / den
    bb_re = f_re[..., None] * b_re - f_im[..., None] * b_im
    bb_im = f_re[..., None] * b_im + f_im[..., None] * b_re
    lag = jnp.arange(t + 1, dtype=F32)[:, None, None, None]
    pmag = jnp.exp(lag * (lam_re * dt))
    pw_re = pmag * jnp.cos(lag * ang)
    pw_im = pmag * jnp.sin(lag * ang)
    cl_re = c_re[None] * pw_re[..., None, :] - c_im[None] * pw_im[..., None, :]
    cl_im = c_re[None] * pw_im[..., None, :] + c_im[None] * pw_re[..., None, :]
    kern = (jnp.einsum('ldgcp,dgpe->ldgce', cl_re[:t], bb_re)
            - jnp.einsum('ldgcp,dgpe->ldgce', cl_im[:t], bb_im))
    tt = jnp.arange(t)
    by_lag = jnp.concatenate([kern[:0:-1, 1], (kern[0, 0] + kern[0, 1])[None], kern[1:, 0]], axis=0)
    toeplitz = jnp.stack([by_lag[t - 1 - s:2 * t - 1 - s] for s in range(t)], axis=0)
    mmat = jnp.transpose(toeplitz, (2, 0, 4, 1, 3)).reshape(g, t * c, t * c)
    wf_re = pw_re[t - 1 - tt, 0][..., None] * bb_re[0][None]
    wf_im = pw_im[t - 1 - tt, 0][..., None] * bb_re[0][None]
    wf_re = wf_re - pw_im[t - 1 - tt, 0][..., None] * bb_im[0][None]
    wf_im = wf_im + pw_re[t - 1 - tt, 0][..., None] * bb_im[0][None]
    wb_re = pw_re[tt, 1][..., None] * bb_re[1][None] - pw_im[tt, 1][..., None] * bb_im[1][None]
    wb_im = pw_im[tt, 1][..., None] * bb_re[1][None] + pw_re[tt, 1][..., None] * bb_im[1][None]
    wmat = jnp.concatenate([wf_re, wb_re, wf_im, wb_im], axis=2)
    wmat = jnp.transpose(wmat, (1, 0, 3, 2)).reshape(g, t * c, 4 * p)
    vf_re = cl_re[tt + 1, 0]
    vf_im = -cl_im[tt + 1, 0]
    vb_re = cl_re[t - tt, 1]
    vb_im = -cl_im[t - tt, 1]
    vmat = jnp.concatenate([vf_re, vb_re, vf_im, vb_im], axis=3)
    vmat = jnp.transpose(vmat, (1, 3, 0, 2)).reshape(g, 4 * p, t * c)
    a_re = jnp.concatenate([pw_re[t, 0], pw_re[t, 1]], axis=-1)
    a_im = jnp.concatenate([pw_im[t, 0], pw_im[t, 1]], axis=-1)
    amat = jnp.zeros((g, 8, 2 * p), F32).at[:, 0].set(a_re).at[:, 1].set(a_im)
    return mmat.astype(BF16), wmat.astype(BF16), vmat.astype(BF16), amat


def _s5_mixer(x, mix_g, lam_re, lam_im, log_dt, b_re, b_im, c_re, c_im, d_skip, w_glu):
    b, s, d = x.shape
    t = SSM_CHUNK
    g = d // SSM_GROUP
    nc = s // t
    x2d = x.reshape(b * s, d)
    ug = _norm_flatten(x2d, mix_g)
    mmat, wmat, vmat, amat = _s5_matrices(lam_re, lam_im, log_dt, b_re, b_im, c_re, c_im)
    yg = _s5_core(ug, mmat, wmat, vmat, amat, nc)
    out = _s5_post(x2d, mix_g, yg, d_skip, w_glu.astype(BF16))
    return out.reshape(b, s, d)


def kernel(x, mix_norm_even, w_in, conv_w, conv_b, conv_ln_g, conv_ln_b, q_norm, k_norm, w_out, mix_norm_odd, ssm_lam_re, ssm_lam_im, ssm_log_dt, ssm_b_re, ssm_b_im, ssm_c_re, ssm_c_im, ssm_d, w_glu, ffn_norm, w_router, b_router, w_e_gate, w_e_up, w_e_down):
    depth = ffn_norm.shape[0]
    for layer in range(depth):
        i = layer // 2
        if layer % 2 == 0:
            x = _even_mixer(x, mix_norm_even[i], w_in[i], conv_w[i], conv_b[i], conv_ln_g[i],
                            conv_ln_b[i], q_norm[i], k_norm[i], w_out[i])
        else:
            x = _s5_mixer(x, mix_norm_odd[i], ssm_lam_re[i], ssm_lam_im[i], ssm_log_dt[i],
                          ssm_b_re[i], ssm_b_im[i], ssm_c_re[i], ssm_c_im[i], ssm_d[i], w_glu[i])
        x = _moe(x, ffn_norm[layer], w_router[layer], b_router[layer],
                 w_e_gate, w_e_up, w_e_down, layer)
    return x
```

```python
import functools
import math

import jax
import jax.numpy as jnp
from jax import lax
from jax.experimental import pallas as pl
from jax.experimental.pallas import tpu as pltpu

F32 = jnp.float32
BF16 = jnp.bfloat16
I32 = jnp.int32

EPS = 1e-6
NEG_INF = -1e30

CONV_CH = 512
CONV_WIDTH = 31
ATT_HEADS = 8
HEAD_DIM = 64
ATT_WIDTH = ATT_HEADS * HEAD_DIM
DILATED = ((128, 1), (512, 4), (2048, 16))
SSM_GROUP = 16
SSM_STATE = 64
N_EXPERTS = 16
CAPACITY_FACTOR = 2
MOE_GATHER_TOKENS = 1024
MOE_SCATTER_TOKENS = 256

LANES = 128
F32_MIN_EXP = -126
F32_MANTISSA_BITS = 23
MIB = 1024 * 1024


def _params(sem, vmem_mib):
    return pltpu.CompilerParams(dimension_semantics=sem, vmem_limit_bytes=vmem_mib * MIB)


def _rmsnorm(x, g):
    return x * lax.rsqrt(jnp.mean(x * x, axis=-1, keepdims=True) + EPS) * g


def _norm_matmul_kernel(x_ref, g_ref, w_ref, o_ref):
    y = _rmsnorm(x_ref[...], g_ref[...])
    o_ref[...] = jnp.dot(y.astype(BF16), w_ref[...], preferred_element_type=F32)


def _norm_matmul(x2d, g, w_bf16, tm=512):
    m, d = x2d.shape
    n = w_bf16.shape[1]
    return pl.pallas_call(
        _norm_matmul_kernel,
        grid=(m // tm,),
        in_specs=[pl.BlockSpec((tm, d), lambda i: (i, 0)),
                  pl.BlockSpec((1, d), lambda i: (0, 0)),
                  pl.BlockSpec((d, n), lambda i: (0, 0))],
        out_specs=pl.BlockSpec((tm, n), lambda i: (i, 0)),
        out_shape=jax.ShapeDtypeStruct((m, n), F32),
        compiler_params=_params(("parallel",), 48),
        name="norm_in_proj",
    )(x2d, g.reshape(1, d), w_bf16)


CONV_HALO = 16
CONV_ROWS = 256


def _conv_kernel(val_ref, gate_ref, w_ref, b_ref, o_ref, hp_ref):
    s = val_ref.shape[1]
    zeros = jnp.zeros((CONV_HALO, LANES), F32)
    hp_ref[0:CONV_HALO, :] = zeros
    hp_ref[CONV_HALO + s:2 * CONV_HALO + s, :] = zeros
    hp_ref[CONV_HALO:CONV_HALO + s, :] = val_ref[0] * jax.nn.sigmoid(gate_ref[0])
    first = CONV_HALO - CONV_WIDTH // 2
    for c in range(s // CONV_ROWS):
        r0 = c * CONV_ROWS
        acc = jnp.broadcast_to(b_ref[...], (CONV_ROWS, LANES))
        for k in range(CONV_WIDTH):
            acc = acc + w_ref[k:k + 1, :] * hp_ref[r0 + first + k:r0 + first + k + CONV_ROWS, :]
        o_ref[0, r0:r0 + CONV_ROWS, :] = acc


def _conformer_conv(proj3, conv_w, conv_b):
    b, s, _ = proj3.shape
    nct = CONV_CH // LANES
    return pl.pallas_call(
        _conv_kernel,
        grid=(b, nct),
        in_specs=[pl.BlockSpec((1, s, LANES), lambda i, j: (i, 0, j)),
                  pl.BlockSpec((1, s, LANES), lambda i, j: (i, 0, nct + j)),
                  pl.BlockSpec((CONV_WIDTH, LANES), lambda i, j: (0, j)),
                  pl.BlockSpec((1, LANES), lambda i, j: (0, j))],
        out_specs=pl.BlockSpec((1, s, LANES), lambda i, j: (i, 0, j)),
        out_shape=jax.ShapeDtypeStruct((b, s, CONV_CH), F32),
        scratch_shapes=[pltpu.VMEM((s + 2 * CONV_HALO, LANES), F32)],
        compiler_params=_params(("parallel", "parallel"), 32),
        name="glu_dwconv",
    )(proj3, proj3, conv_w, conv_b.reshape(1, CONV_CH))


ATT_QB = 128
ATT_RAD = 64
ATT_KW = ATT_QB + 2 * ATT_RAD
ATT_PREP_ROWS = 512
ATT_UNROLL = 8


def _attn_kernel(slopes_ref, q_ref, k_ref, v_ref, qg_ref, kg_ref, o_ref,
                 qs_ref, kp_ref, vp_ref, ob_ref, *, seq, pad):
    pair = pl.program_id(1)
    lane = lax.broadcasted_iota(I32, (1, LANES), 1)
    lo = lane < HEAD_DIM
    scale = 1.0 / math.sqrt(HEAD_DIM)

    zpad = jnp.zeros((pad, LANES), F32)
    for hh in range(2):
        kp_ref[hh, 0:pad, :] = zpad
        kp_ref[hh, pad + seq:2 * pad + seq, :] = zpad
        vp_ref[hh, 0:pad, :] = zpad
        vp_ref[hh, pad + seq:2 * pad + seq, :] = zpad

    def head_rms(x, g):
        x2 = x * x
        s_lo = jnp.sum(jnp.where(lo, x2, 0.0), axis=-1, keepdims=True)
        s_hi = jnp.sum(jnp.where(lo, 0.0, x2), axis=-1, keepdims=True)
        ms = jnp.where(lo, s_lo, s_hi) * (1.0 / HEAD_DIM)
        return x * lax.rsqrt(ms + EPS) * g

    def prep(i, carry):
        r = pl.multiple_of(i * ATT_PREP_ROWS, ATT_PREP_ROWS)
        rows = pl.ds(r, ATT_PREP_ROWS)
        prows = pl.ds(pad + r, ATT_PREP_ROWS)
        qs_ref[rows, :] = head_rms(q_ref[0, rows, :], qg_ref[...]) * scale
        kn = head_rms(k_ref[0, rows, :], kg_ref[...])
        kp_ref[0, prows, :] = jnp.where(lo, kn, 0.0)
        kp_ref[1, prows, :] = jnp.where(lo, 0.0, kn)
        v = v_ref[0, rows, :]
        vp_ref[0, prows, :] = jnp.where(lo, v, jnp.where(lane == HEAD_DIM, 1.0, 0.0))
        vp_ref[1, prows, :] = jnp.where(lo, jnp.where(lane == 0, 1.0, 0.0), v)
        return carry

    lax.fori_loop(0, seq // ATT_PREP_ROWS, prep, 0)

    rowi = lax.broadcasted_iota(I32, (ATT_QB, ATT_KW), 0)
    coli = lax.broadcasted_iota(I32, (ATT_QB, ATT_KW), 1)
    arel = jnp.abs(coli - ATT_RAD - rowi)
    kcol = lax.broadcasted_iota(I32, (1, ATT_KW), 1) - ATT_RAD

    for hh in range(2):
        slope = slopes_ref[pair * 2 + hh]
        zlane = HEAD_DIM if hh == 0 else 0
        mlane = zlane + 1
        for bi, (window, dil) in enumerate(DILATED):
            assert window // (2 * dil) == ATT_RAD
            cls_len = seq // dil
            nbc = cls_len // ATT_QB
            bias = jnp.where(arel <= ATT_RAD, -(slope * dil) * arel.astype(F32), NEG_INF)

            def blk(j, carry, bi=bi, dil=dil, cls_len=cls_len, nbc=nbc, bias=bias, hh=hh, mlane=mlane):
                c = j // nbc
                n = j - c * nbc
                qstart = c + dil * ATT_QB * n
                kstart = pad + c + dil * (ATT_QB * n - ATT_RAD)
                if dil == 1:
                    qsl = pl.ds(qstart, ATT_QB)
                    ksl = pl.ds(kstart, ATT_KW)
                else:
                    qsl = pl.ds(qstart, ATT_QB, stride=dil)
                    ksl = pl.ds(kstart, ATT_KW, stride=dil)
                qb = qs_ref[qsl, :].astype(BF16)
                kw = kp_ref[hh, ksl, :].astype(BF16)
                vw = vp_ref[hh, ksl, :].astype(BF16)
                sc = lax.dot_general(qb, kw, (((1,), (1,)), ((), ())),
                                     preferred_element_type=F32) + bias
                kj = kcol + ATT_QB * n
                sc = jnp.where((kj >= 0) & (kj < cls_len), sc, NEG_INF)
                m = jnp.max(sc, axis=-1, keepdims=True)
                p = jnp.exp(sc - m)
                o = jnp.dot(p.astype(BF16), vw, preferred_element_type=F32)
                ob_ref[bi, qsl, :] = jnp.where(lane == mlane, m, o)
                return carry

            lax.fori_loop(0, seq // ATT_QB, blk, 0, unroll=ATT_UNROLL)

        def combine(i, carry, hh=hh, zlane=zlane, mlane=mlane):
            r = pl.multiple_of(i * ATT_PREP_ROWS, ATT_PREP_ROWS)
            rows = pl.ds(r, ATT_PREP_ROWS)
            parts = [ob_ref[bi, rows, :] for bi in range(len(DILATED))]
            ms = [jnp.sum(jnp.where(lane == mlane, o, 0.0), axis=-1, keepdims=True) for o in parts]
            mx = functools.reduce(jnp.maximum, ms)
            acc = sum(jnp.exp(m - mx) * o for m, o in zip(ms, parts))
            den = jnp.sum(jnp.where(lane == zlane, acc, 0.0), axis=-1, keepdims=True)
            res = acc / den
            if hh == 0:
                o_ref[0, rows, :] = res
            else:
                o_ref[0, rows, :] = jnp.where(lo, o_ref[0, rows, :], res)
            return carry

        lax.fori_loop(0, seq // ATT_PREP_ROWS, combine, 0)


def _dilated_attention(proj3, q_norm, k_norm):
    b, s, _ = proj3.shape
    pad = ATT_RAD * max(d for _, d in DILATED)
    npair = ATT_HEADS // 2
    qoff = 2 * CONV_CH // LANES
    slopes = jnp.asarray([2.0 ** (-8.0 * (h + 1) / ATT_HEADS) for h in range(ATT_HEADS)], F32)
    qg = jnp.tile(q_norm.reshape(1, HEAD_DIM), (1, 2))
    kg = jnp.tile(k_norm.reshape(1, HEAD_DIM), (1, 2))
    kernel = functools.partial(_attn_kernel, seq=s, pad=pad)
    return pl.pallas_call(
        kernel,
        grid=(b, npair),
        in_specs=[pl.BlockSpec(memory_space=pltpu.SMEM),
                  pl.BlockSpec((1, s, LANES), lambda i, j: (i, 0, qoff + j)),
                  pl.BlockSpec((1, s, LANES), lambda i, j: (i, 0, qoff + npair + j)),
                  pl.BlockSpec((1, s, LANES), lambda i, j: (i, 0, qoff + 2 * npair + j)),
                  pl.BlockSpec((1, LANES), lambda i, j: (0, 0)),
                  pl.BlockSpec((1, LANES), lambda i, j: (0, 0))],
        out_specs=pl.BlockSpec((1, s, LANES), lambda i, j: (i, 0, j)),
        out_shape=jax.ShapeDtypeStruct((b, s, ATT_WIDTH), F32),
        scratch_shapes=[pltpu.VMEM((s, LANES), F32),
                        pltpu.VMEM((2, s + 2 * pad, LANES), F32),
                        pltpu.VMEM((2, s + 2 * pad, LANES), F32),
                        pltpu.VMEM((len(DILATED), s, LANES), F32)],
        compiler_params=_params(("parallel", "parallel"), 48),
        name="dilated_attn",
    )(slopes, proj3, proj3, proj3, qg, kg)


def _outproj_kernel(x_ref, conv_ref, att_ref, g_ref, b_ref, w1_ref, w2_ref, o_ref):
    c = conv_ref[...]
    mu = jnp.mean(c, axis=-1, keepdims=True)
    cc = c - mu
    var = jnp.mean(cc * cc, axis=-1, keepdims=True)
    hn = cc * lax.rsqrt(var + EPS) * g_ref[...] + b_ref[...]
    a = hn * jax.nn.sigmoid(hn)
    y = jnp.dot(a.astype(BF16), w1_ref[...], preferred_element_type=F32)
    y = y + jnp.dot(att_ref[...].astype(BF16), w2_ref[...], preferred_element_type=F32)
    o_ref[...] = x_ref[...] + y


def _out_projection(x2d, conv2d, att2d, ln_g, ln_b, w_out_bf16, tm=512):
    m, d = x2d.shape
    row = lambda i: (i, 0)
    fixed = lambda i: (0, 0)
    return pl.pallas_call(
        _outproj_kernel,
        grid=(m // tm,),
        in_specs=[pl.BlockSpec((tm, d), row),
                  pl.BlockSpec((tm, CONV_CH), row),
                  pl.BlockSpec((tm, ATT_WIDTH), row),
                  pl.BlockSpec((1, CONV_CH), fixed),
                  pl.BlockSpec((1, CONV_CH), fixed),
                  pl.BlockSpec((CONV_CH, d), fixed),
                  pl.BlockSpec((ATT_WIDTH, d), lambda i: (1, 0))],
        out_specs=pl.BlockSpec((tm, d), row),
        out_shape=jax.ShapeDtypeStruct((m, d), F32),
        compiler_params=_params(("parallel",), 32),
        name="ln_out_proj",
    )(x2d, conv2d, att2d, ln_g.reshape(1, CONV_CH), ln_b.reshape(1, CONV_CH), w_out_bf16, w_out_bf16)


def _even_mixer(x, mix_g, w_in, conv_w, conv_b, ln_g, ln_b, q_norm, k_norm, w_out):
    b, s, d = x.shape
    x2d = x.reshape(b * s, d)
    proj = _norm_matmul(x2d, mix_g, w_in.astype(BF16))
    proj3 = proj.reshape(b, s, -1)
    conv = _conformer_conv(proj3, conv_w, conv_b)
    att = _dilated_attention(proj3, q_norm, k_norm)
    out = _out_projection(x2d, conv.reshape(b * s, CONV_CH), att.reshape(b * s, ATT_WIDTH),
                          ln_g, ln_b, w_out.astype(BF16))
    return out.reshape(b, s, d)


def _router_kernel(x_ref, g_ref, wr_ref, br_ref, h_ref, aff_ref):
    h = _rmsnorm(x_ref[0], g_ref[...])
    logits = jnp.dot(h, wr_ref[...], precision=lax.Precision.HIGHEST,
                     preferred_element_type=F32) + br_ref[...]
    lt = logits.T[:N_EXPERTS, :]
    mx = jnp.max(lt, axis=0, keepdims=True)
    ex = jnp.exp(lt - mx)
    aff_ref[0] = ex / jnp.sum(ex, axis=0, keepdims=True)
    h_ref[0] = h.astype(BF16)


def _router(x, g, w_router, b_router, tm=512):
    b, s, d = x.shape
    wr = jnp.zeros((d, LANES), F32).at[:, :N_EXPERTS].set(w_router)
    br = jnp.zeros((1, LANES), F32).at[0, :N_EXPERTS].set(b_router)
    return pl.pallas_call(
        _router_kernel,
        grid=(b, s // tm),
        in_specs=[pl.BlockSpec((1, tm, d), lambda i, j: (i, j, 0)),
                  pl.BlockSpec((1, d), lambda i, j: (0, 0)),
                  pl.BlockSpec((d, LANES), lambda i, j: (0, 0)),
                  pl.BlockSpec((1, LANES), lambda i, j: (0, 0))],
        out_specs=[pl.BlockSpec((1, tm, d), lambda i, j: (i, j, 0)),
                   pl.BlockSpec((1, N_EXPERTS, tm), lambda i, j: (i, 0, j))],
        out_shape=[jax.ShapeDtypeStruct((b, s, d), BF16),
                   jax.ShapeDtypeStruct((b, N_EXPERTS, s), F32)],
        compiler_params=_params(("parallel", "parallel"), 32),
        name="moe_router",
    )(x, g.reshape(1, d), wr, br)


def _select_kernel(aff_ref, pos_ref, gsel_ref, post_ref, starts_ref, cs_ref, *, cap):
    a = aff_ref[0]
    s = a.shape[1]
    capf = float(cap)

    def count(mask):
        return jnp.sum(jnp.where(mask, 1.0, 0.0), axis=-1, keepdims=True)

    def enough(cand):
        return count(a >= cand) >= capf

    tiny = float(2.0 ** F32_MIN_EXP)
    thr = jnp.full((N_EXPERTS, 1), tiny, F32)
    normal = enough(thr)
    for j in range(6, -1, -1):
        cand = thr * float(2.0 ** (2 ** j))
        thr = jnp.where(enough(cand), cand, thr)
    step = thr * 0.5
    for _ in range(F32_MANTISSA_BITS):
        cand = thr + step
        thr = jnp.where(enough(cand), cand, thr)
        step = step * 0.5
    thr = jnp.where(normal, thr, 0.0)

    ri = lax.broadcasted_iota(I32, (LANES, LANES), 0)
    ci = lax.broadcasted_iota(I32, (LANES, LANES), 1)
    tri = jnp.where(ri < ci, 1.0, 0.0).astype(BF16)

    lane = lax.broadcasted_iota(I32, (1, LANES), 1)
    chunks_per_tile = MOE_SCATTER_TOKENS // LANES

    def exclusive_cumsum(mask):
        ind = jnp.where(mask, 1.0, 0.0)
        off = jnp.zeros((N_EXPERTS, 1), F32)
        starts = jnp.zeros((N_EXPERTS, LANES), F32)
        for j in range(s // LANES):
            if j % chunks_per_tile == 0:
                starts = jnp.where(lane == j // chunks_per_tile, off, starts)
            xc = ind[:, j * LANES:(j + 1) * LANES]
            cs_ref[:, j * LANES:(j + 1) * LANES] = (
                jnp.dot(xc.astype(BF16), tri, preferred_element_type=F32) + off)
            off = off + jnp.sum(xc, axis=-1, keepdims=True)
        starts = jnp.where(lane == s // MOE_SCATTER_TOKENS, off, starts)
        return cs_ref[...], starts

    gt = a > thr
    eq = a == thr
    need = capf - count(gt)
    tie_rank, _ = exclusive_cumsum(eq)
    sel = gt | (eq & (tie_rank < need))
    slot, starts = exclusive_cumsum(sel)
    slot = jnp.where(sel, slot, -1.0)
    pos_ref[0] = slot.astype(I32)
    gsel_ref[0] = jnp.where(sel, a, 0.0)
    starts_ref[0] = starts.astype(I32)
    fill = jnp.zeros((LANES - N_EXPERTS, s), F32)
    post_ref[0] = jnp.concatenate([slot, fill], axis=0).T


def _select(aff, cap):
    b, e, s = aff.shape
    assert s // MOE_SCATTER_TOKENS < LANES
    blk = pl.BlockSpec((1, e, s), lambda i: (i, 0, 0))
    return pl.pallas_call(
        functools.partial(_select_kernel, cap=cap),
        grid=(b,),
        in_specs=[blk],
        out_specs=[blk, blk,
                   pl.BlockSpec((1, s, LANES), lambda i: (i, 0, 0)),
                   pl.BlockSpec((1, e, LANES), lambda i: (i, 0, 0))],
        out_shape=[jax.ShapeDtypeStruct((b, e, s), I32),
                   jax.ShapeDtypeStruct((b, e, s), F32),
                   jax.ShapeDtypeStruct((b, s, LANES), F32),
                   jax.ShapeDtypeStruct((b, e, LANES), I32)],
        scratch_shapes=[pltpu.VMEM((e, s), F32)],
        compiler_params=_params(("parallel",), 32),
        name="moe_select",
    )(aff)


def _expert_kernel(h_ref, pos_ref, gsel_ref, wg_ref, wu_ref, wd_ref, y_ref, *, cap):
    s = h_ref.shape[1]
    slot = lax.broadcasted_iota(I32, (cap, MOE_GATHER_TOKENS), 0)
    x = None
    gate = None
    for k in range(s // MOE_GATHER_TOKENS):
        tok = slice(k * MOE_GATHER_TOKENS, (k + 1) * MOE_GATHER_TOKENS)
        hit = pos_ref[0, 0, :, tok] == slot
        part = jnp.dot(jnp.where(hit, 1.0, 0.0).astype(BF16), h_ref[0, tok, :],
                       preferred_element_type=F32)
        gpart = jnp.sum(jnp.where(hit, gsel_ref[0, 0, :, tok], 0.0), axis=-1, keepdims=True)
        x = part if x is None else x + part
        gate = gpart if gate is None else gate + gpart
    x = x.astype(BF16)
    g = jnp.dot(x, wg_ref[0, 0].astype(BF16), preferred_element_type=F32)
    u = jnp.dot(x, wu_ref[0, 0].astype(BF16), preferred_element_type=F32)
    a = (g * jax.nn.sigmoid(g) * u).astype(BF16)
    y = jnp.dot(a, wd_ref[0, 0].astype(BF16), preferred_element_type=F32)
    y_ref[0, 0] = (y * gate).astype(BF16)


def _experts(h, pos4, gsel4, w_gate, w_up, w_down, layer, cap):
    b, s, d = h.shape
    _, e, _, f = w_gate.shape
    return pl.pallas_call(
        functools.partial(_expert_kernel, cap=cap),
        grid=(b, e),
        in_specs=[pl.BlockSpec((1, s, d), lambda i, j: (i, 0, 0), pipeline_mode=pl.Buffered(1)),
                  pl.BlockSpec((1, 1, 1, s), lambda i, j: (i, j, 0, 0)),
                  pl.BlockSpec((1, 1, 1, s), lambda i, j: (i, j, 0, 0)),
                  pl.BlockSpec((1, 1, d, f), lambda i, j: (layer, j, 0, 0)),
                  pl.BlockSpec((1, 1, d, f), lambda i, j: (layer, j, 0, 0)),
                  pl.BlockSpec((1, 1, f, d), lambda i, j: (layer, j, 0, 0))],
        out_specs=pl.BlockSpec((1, 1, cap, d), lambda i, j: (i, j, 0, 0)),
        out_shape=jax.ShapeDtypeStruct((b, e, cap, d), BF16),
        compiler_params=_params(("parallel", "parallel"), 56),
        name="moe_experts",
    )(h, pos4, gsel4, w_gate, w_up, w_down)


def _scatter_kernel(starts_ref, x_ref, y_ref, post_ref, o_ref, *, cap):
    bi = pl.program_id(0)
    ti = pl.program_id(1)
    win = MOE_SCATTER_TOKENS
    lane = lax.broadcasted_iota(I32, (1, LANES), 1)
    wslot = lax.broadcasted_iota(I32, (1, win), 1).astype(F32)
    post = post_ref[0]

    def slot_column(e):
        return jnp.sum(jnp.where(lane == e, post, 0.0), axis=-1, keepdims=True)

    def window_start(e):
        first = starts_ref[bi, e, ti]
        off = jnp.minimum(lax.shift_left(lax.shift_right_logical(first, 4), 4), cap - win)
        return pl.multiple_of(off, 16)

    acc = x_ref[0]
    for e in range(N_EXPERTS):
        off = window_start(e)
        onehot = jnp.where(slot_column(e) - off.astype(F32) == wslot, 1.0, 0.0).astype(BF16)
        acc = acc + jnp.dot(onehot, y_ref[0, e, pl.ds(off, win), :], preferred_element_type=F32)
    o_ref[0] = acc

    tail = cap - win
    for e in range(N_EXPERTS):
        covered = window_start(e) + win

        @pl.when(starts_ref[bi, e, ti + 1] > covered)
        def _(e=e, covered=covered):
            pcol = slot_column(e)
            hit = (pcol - float(tail) == wslot) & (pcol >= covered.astype(F32))
            o_ref[0] += jnp.dot(jnp.where(hit, 1.0, 0.0).astype(BF16),
                                y_ref[0, e, tail:cap, :], preferred_element_type=F32)


def _scatter(x, y, post, starts, cap):
    b, s, d = x.shape
    e = y.shape[1]
    tn = MOE_SCATTER_TOKENS
    assert cap % 16 == 0 and tn <= cap <= 2 * tn
    grid_spec = pltpu.PrefetchScalarGridSpec(
        num_scalar_prefetch=1,
        grid=(b, s // tn),
        in_specs=[pl.BlockSpec((1, tn, d), lambda i, j, st: (i, j, 0)),
                  pl.BlockSpec((1, e, cap, d), lambda i, j, st: (i, 0, 0, 0),
                               pipeline_mode=pl.Buffered(1)),
                  pl.BlockSpec((1, tn, LANES), lambda i, j, st: (i, j, 0))],
        out_specs=pl.BlockSpec((1, tn, d), lambda i, j, st: (i, j, 0)),
    )
    return pl.pallas_call(
        functools.partial(_scatter_kernel, cap=cap),
        grid_spec=grid_spec,
        out_shape=jax.ShapeDtypeStruct((b, s, d), F32),
        compiler_params=_params(("parallel", "parallel"), 48),
        name="moe_scatter",
    )(starts, x, y, post)


def _moe(x, ffn_g, w_router, b_router, w_gate, w_up, w_down, layer):
    b, s, d = x.shape
    cap = CAPACITY_FACTOR * s // N_EXPERTS
    h, aff = _router(x, ffn_g, w_router, b_router)
    pos, gsel, post, starts = _select(aff, cap)
    y = _experts(h, pos.reshape(b, N_EXPERTS, 1, s), gsel.reshape(b, N_EXPERTS, 1, s),
                 w_gate, w_up, w_down, layer, cap)
    return _scatter(x, y, post, starts, cap)


SSM_CHUNK = 16


SSM_SLAB_GROUPS = LANES // SSM_GROUP
SSM_CHUNK_LANES = SSM_CHUNK * SSM_GROUP
SSM_HALF_STEPS = LANES // SSM_GROUP


def _lane_group():
    return lax.broadcasted_iota(I32, (1, LANES), 1) // SSM_GROUP


def _norm_flatten_kernel(x_ref, g_ref, ug_ref, u_scr):
    tm, d = x_ref.shape
    nch = tm // SSM_CHUNK
    u = _rmsnorm(x_ref[...], g_ref[...])
    lgrp = _lane_group()
    for slab in range(d // LANES):
        u_scr[slab] = u[:, slab * LANES:(slab + 1) * LANES]
        steps = [u_scr[slab, pl.ds(t, nch, stride=SSM_CHUNK), :] for t in range(SSM_CHUNK)]
        for gi in range(SSM_SLAB_GROUPS):
            for half in range(SSM_CHUNK_LANES // LANES):
                acc = None
                for tt in range(SSM_HALF_STEPS):
                    shift = ((tt - gi) * SSM_GROUP) % LANES
                    piece = steps[half * SSM_HALF_STEPS + tt]
                    if shift:
                        piece = pltpu.roll(piece, shift, axis=1)
                    acc = piece if acc is None else jnp.where(lgrp == tt, piece, acc)
                ug_ref[slab * SSM_SLAB_GROUPS + gi, :, half * LANES:(half + 1) * LANES] = acc.astype(BF16)


def _norm_flatten(x2d, g, tm=512):
    m, d = x2d.shape
    ngrp = d // SSM_GROUP
    nch = tm // SSM_CHUNK
    return pl.pallas_call(
        _norm_flatten_kernel,
        grid=(m // tm,),
        in_specs=[pl.BlockSpec((tm, d), lambda i: (i, 0)),
                  pl.BlockSpec((1, d), lambda i: (0, 0))],
        out_specs=pl.BlockSpec((ngrp, nch, SSM_CHUNK_LANES), lambda i: (0, i, 0)),
        out_shape=jax.ShapeDtypeStruct((ngrp, m // SSM_CHUNK, SSM_CHUNK_LANES), BF16),
        scratch_shapes=[pltpu.VMEM((d // LANES, tm, LANES), F32)],
        compiler_params=_params(("parallel",), 32),
        name="s5_norm_flatten",
    )(x2d, g.reshape(1, d))


def _s5_core_kernel(u_ref, m_ref, w_ref, v_ref, a_ref, y_ref, *, nc):
    u = u_ref[0]
    n_rows = u.shape[0]
    y = jnp.dot(u, m_ref[0], preferred_element_type=F32)
    x = jnp.dot(u, w_ref[0], preferred_element_type=F32)
    hr = x[:, :LANES]
    hi = x[:, LANES:]
    ar = a_ref[0, 0:1, :]
    ai = a_ref[0, 1:2, :]
    lane = lax.broadcasted_iota(I32, (1, LANES), 1)
    fwd = lane < SSM_STATE
    rown = lax.broadcasted_iota(I32, (n_rows, 1), 0) % nc

    def shifted(val, step):
        down = jnp.where(rown >= step, pltpu.roll(val, step, axis=0), 0.0)
        up = jnp.where(rown < nc - step, pltpu.roll(val, n_rows - step, axis=0), 0.0)
        return jnp.where(fwd, down, up)

    step = 1
    while step < nc:
        pr = shifted(hr, step)
        pi = shifted(hi, step)
        hr, hi = hr + ar * pr - ai * pi, hi + ar * pi + ai * pr
        ar, ai = ar * ar - ai * ai, 2.0 * ar * ai
        step *= 2
    hin_r = shifted(hr, 1).astype(BF16)
    hin_i = shifted(hi, 1).astype(BF16)
    y = y + jnp.dot(hin_r, v_ref[0, :LANES, :], preferred_element_type=F32)
    y = y + jnp.dot(hin_i, v_ref[0, LANES:, :], preferred_element_type=F32)
    y_ref[0] = y


def _s5_core(ug, mmat, wmat, vmat, amat, nc):
    g, n, k = ug.shape
    blk = lambda r, c: pl.BlockSpec((1, r, c), lambda i: (i, 0, 0))
    return pl.pallas_call(
        functools.partial(_s5_core_kernel, nc=nc),
        grid=(g,),
        in_specs=[blk(n, k), blk(k, k), blk(k, 2 * LANES), blk(2 * LANES, k), blk(8, LANES)],
        out_specs=blk(n, k),
        out_shape=jax.ShapeDtypeStruct((g, n, k), F32),
        compiler_params=_params(("parallel",), 48),
        name="s5_core",
    )(ug, mmat, wmat, vmat, amat)


def _s5_post_kernel(x_ref, g_ref, yg_ref, d_ref, w_ref, o_ref, y_scr):
    x = x_ref[...]
    tm, d = x.shape
    nch = tm // SSM_CHUNK
    lgrp = _lane_group()
    for slab in range(d // LANES):
        for t in range(SSM_CHUNK):
            half, tt = divmod(t, SSM_HALF_STEPS)
            acc = None
            for gi in range(SSM_SLAB_GROUPS):
                piece = yg_ref[slab * SSM_SLAB_GROUPS + gi, :, half * LANES:(half + 1) * LANES]
                shift = ((gi - tt) * SSM_GROUP) % LANES
                if shift:
                    piece = pltpu.roll(piece, shift, axis=1)
                acc = piece if acc is None else jnp.where(lgrp == gi, piece, acc)
            y_scr[slab, pl.ds(t, nch, stride=SSM_CHUNK), :] = acc
    y = jnp.concatenate([y_scr[slab] for slab in range(d // LANES)], axis=1)
    hf = _rmsnorm(x, g_ref[...])
    z = jax.nn.gelu(y + d_ref[...] * hf)
    vg = jnp.dot(z.astype(BF16), w_ref[...], preferred_element_type=F32)
    o_ref[...] = x + vg[:, :d] * jax.nn.sigmoid(vg[:, d:])


def _s5_post(x2d, g, yg, d_skip, w_glu_bf16, tm=512):
    m, d = x2d.shape
    ngrp = d // SSM_GROUP
    nch = tm // SSM_CHUNK
    row = lambda i: (i, 0)
    fixed = lambda i: (0, 0)
    return pl.pallas_call(
        _s5_post_kernel,
        grid=(m // tm,),
        in_specs=[pl.BlockSpec((tm, d), row), pl.BlockSpec((1, d), fixed),
                  pl.BlockSpec((ngrp, nch, SSM_CHUNK_LANES), lambda i: (0, i, 0)),
                  pl.BlockSpec((1, d), fixed),
                  pl.BlockSpec((d, 2 * d), fixed)],
        out_specs=pl.BlockSpec((tm, d), row),
        out_shape=jax.ShapeDtypeStruct((m, d), F32),
        scratch_shapes=[pltpu.VMEM((d // LANES, tm, LANES), F32)],
        compiler_params=_params(("parallel",), 48),
        name="s5_gelu_glu",
    )(x2d, g.reshape(1, d), yg, d_skip.reshape(1, d), w_glu_bf16)


def _s5_matrices(lam_re, lam_im, log_dt, b_re, b_im, c_re, c_im):
    t = SSM_CHUNK
    g, p = lam_re.shape[1:]
    c = SSM_GROUP
    dt = jnp.exp(log_dt)[..., None]
    den = lam_re * lam_re + lam_im * lam_im
    mag = jnp.exp(lam_re * dt)
    ang = lam_im * dt
    lb_re = mag * jnp.cos(ang)
    lb_im = mag * jnp.sin(ang)
    nr = lb_re - 1.0
    f_re = (nr * lam_re + lb_im * lam_im) / den
    f_im = (lb_im * lam_re - nr * lam_im) / den
    bb_re = f_re[..., None] * b_re - f_im[..., None] * b_im
    bb_im = f_re[..., None] * b_im + f_im[..., None] * b_re
    lag = jnp.arange(t + 1, dtype=F32)[:, None, None, None]
    pmag = jnp.exp(lag * (lam_re * dt))
    pw_re = pmag * jnp.cos(lag * ang)
    pw_im = pmag * jnp.sin(lag * ang)
    cl_re = c_re[None] * pw_re[..., None, :] - c_im[None] * pw_im[..., None, :]
    cl_im = c_re[None] * pw_im[..., None, :] + c_im[None] * pw_re[..., None, :]
    kern = (jnp.einsum('ldgcp,dgpe->ldgce', cl_re[:t], bb_re)
            - jnp.einsum('ldgcp,dgpe->ldgce', cl_im[:t], bb_im))
    tt = jnp.arange(t)
    by_lag = jnp.concatenate([kern[:0:-1, 1], (kern[0, 0] + kern[0, 1])[None], kern[1:, 0]], axis=0)
    toeplitz = jnp.stack([by_lag[t - 1 - s:2 * t - 1 - s] for s in range(t)], axis=0)
    mmat = jnp.transpose(toeplitz, (2, 0, 4, 1, 3)).reshape(g, t * c, t * c)
    wf_re = pw_re[t - 1 - tt, 0][..., None] * bb_re[0][None]
    wf_im = pw_im[t - 1 - tt, 0][..., None] * bb_re[0][None]
    wf_re = wf_re - pw_im[t - 1 - tt, 0][..., None] * bb_im[0][None]
    wf_im = wf_im + pw_re[t - 1 - tt, 0][..., None] * bb_im[0][None]
    wb_re = pw_re[tt, 1][..., None] * bb_re[1][None] - pw_im[tt, 1][..., None] * bb_im[1][None]
    wb_im = pw_im[tt, 1][..., None] * bb_re[1][None] + pw_re[tt, 1][..., None] * bb_im[1][None]
    wmat = jnp.concatenate([wf_re, wb_re, wf_im, wb_im], axis=2)
    wmat = jnp.transpose(wmat, (1, 0, 3, 2)).reshape(g, t * c, 4 * p)
    vf_re = cl_re[tt + 1, 0]
    vf_im = -cl_im[tt + 1, 0]
    vb_re = cl_re[t - tt, 1]
    vb_im = -cl_im[t - tt, 1]
    vmat = jnp.concatenate([vf_re, vb_re, vf_im, vb_im], axis=3)
    vmat = jnp.transpose(vmat, (1, 3, 0, 2)).reshape(g, 4 * p, t * c)
    a_re = jnp.concatenate([pw_re[t, 0], pw_re[t, 1]], axis=-1)
    a_im = jnp.concatenate([pw_im[t, 0], pw_im[t, 1]], axis=-1)
    amat = jnp.zeros((g, 8, 2 * p), F32).at[:, 0].set(a_re).at[:, 1].set(a_im)
    return mmat.astype(BF16), wmat.astype(BF16), vmat.astype(BF16), amat


def _s5_mixer(x, mix_g, lam_re, lam_im, log_dt, b_re, b_im, c_re, c_im, d_skip, w_glu):
    b, s, d = x.shape
    t = SSM_CHUNK
    g = d // SSM_GROUP
    nc = s // t
    x2d = x.reshape(b * s, d)
    ug = _norm_flatten(x2d, mix_g)
    mmat, wmat, vmat, amat = _s5_matrices(lam_re, lam_im, log_dt, b_re, b_im, c_re, c_im)
    yg = _s5_core(ug, mmat, wmat, vmat, amat, nc)
    out = _s5_post(x2d, mix_g, yg, d_skip, w_glu.astype(BF16))
    return out.reshape(b, s, d)


def kernel(x, mix_norm_even, w_in, conv_w, conv_b, conv_ln_g, conv_ln_b, q_norm, k_norm, w_out, mix_norm_odd, ssm_lam_re, ssm_lam_im, ssm_log_dt, ssm_b_re, ssm_b_im, ssm_c_re, ssm_c_im, ssm_d, w_glu, ffn_norm, w_router, b_router, w_e_gate, w_e_up, w_e_down):
    depth = ffn_norm.shape[0]
    for layer in range(depth):
        i = layer // 2
        if layer % 2 == 0:
            x = _even_mixer(x, mix_norm_even[i], w_in[i], conv_w[i], conv_b[i], conv_ln_g[i],
                            conv_ln_b[i], q_norm[i], k_norm[i], w_out[i])
        else:
            x = _s5_mixer(x, mix_norm_odd[i], ssm_lam_re[i], ssm_lam_im[i], ssm_log_dt[i],
                          ssm_b_re[i], ssm_b_im[i], ssm_c_re[i], ssm_c_im[i], ssm_d[i], w_glu[i])
        x = _moe(x, ffn_norm[layer], w_router[layer], b_router[layer],
                 w_e_gate, w_e_up, w_e_down, layer)
    return x
```

```python
import functools
import math

import jax
import jax.numpy as jnp
from jax import lax
from jax.experimental import pallas as pl
from jax.experimental.pallas import tpu as pltpu

F32 = jnp.float32
BF16 = jnp.bfloat16
I32 = jnp.int32

EPS = 1e-6
NEG_INF = -1e30
LOG2_E = math.log2(math.e)

CONV_CH = 512
CONV_WIDTH = 31
ATT_HEADS = 8
HEAD_DIM = 64
ATT_WIDTH = ATT_HEADS * HEAD_DIM
DILATED = ((128, 1), (512, 4), (2048, 16))
SSM_GROUP = 16
SSM_STATE = 64
N_EXPERTS = 16
CAPACITY_FACTOR = 2
MOE_GATHER_TOKENS = 1024
MOE_SCATTER_TOKENS = 256

LANES = 128
F32_MIN_EXP = -126
F32_MANTISSA_BITS = 23
MIB = 1024 * 1024


def _params(sem, vmem_mib):
    return pltpu.CompilerParams(dimension_semantics=sem, vmem_limit_bytes=vmem_mib * MIB)


def _rmsnorm(x, g):
    return x * lax.rsqrt(jnp.mean(x * x, axis=-1, keepdims=True) + EPS) * g


def _norm_matmul_kernel(x_ref, g_ref, w_ref, o_ref):
    y = _rmsnorm(x_ref[...], g_ref[...])
    o_ref[...] = jnp.dot(y.astype(BF16), w_ref[...], preferred_element_type=F32)


def _norm_matmul(x2d, g, w_bf16, tm=512):
    m, d = x2d.shape
    n = w_bf16.shape[1]
    return pl.pallas_call(
        _norm_matmul_kernel,
        grid=(m // tm,),
        in_specs=[pl.BlockSpec((tm, d), lambda i: (i, 0)),
                  pl.BlockSpec((1, d), lambda i: (0, 0)),
                  pl.BlockSpec((d, n), lambda i: (0, 0))],
        out_specs=pl.BlockSpec((tm, n), lambda i: (i, 0)),
        out_shape=jax.ShapeDtypeStruct((m, n), F32),
        compiler_params=_params(("parallel",), 48),
        name="norm_in_proj",
    )(x2d, g.reshape(1, d), w_bf16)


CONV_HALO = 16
CONV_ROWS = 256


def _conv_kernel(val_ref, gate_ref, w_ref, b_ref, o_ref, hp_ref):
    s = val_ref.shape[1]
    zeros = jnp.zeros((CONV_HALO, LANES), F32)
    hp_ref[0:CONV_HALO, :] = zeros
    hp_ref[CONV_HALO + s:2 * CONV_HALO + s, :] = zeros
    hp_ref[CONV_HALO:CONV_HALO + s, :] = val_ref[0] * jax.nn.sigmoid(gate_ref[0])
    first = CONV_HALO - CONV_WIDTH // 2
    for c in range(s // CONV_ROWS):
        r0 = c * CONV_ROWS
        acc = jnp.broadcast_to(b_ref[...], (CONV_ROWS, LANES))
        for k in range(CONV_WIDTH):
            acc = acc + w_ref[k:k + 1, :] * hp_ref[r0 + first + k:r0 + first + k + CONV_ROWS, :]
        o_ref[0, r0:r0 + CONV_ROWS, :] = acc


def _conformer_conv(proj3, conv_w, conv_b):
    b, s, _ = proj3.shape
    nct = CONV_CH // LANES
    return pl.pallas_call(
        _conv_kernel,
        grid=(b, nct),
        in_specs=[pl.BlockSpec((1, s, LANES), lambda i, j: (i, 0, j)),
                  pl.BlockSpec((1, s, LANES), lambda i, j: (i, 0, nct + j)),
                  pl.BlockSpec((CONV_WIDTH, LANES), lambda i, j: (0, j)),
                  pl.BlockSpec((1, LANES), lambda i, j: (0, j))],
        out_specs=pl.BlockSpec((1, s, LANES), lambda i, j: (i, 0, j)),
        out_shape=jax.ShapeDtypeStruct((b, s, CONV_CH), F32),
        scratch_shapes=[pltpu.VMEM((s + 2 * CONV_HALO, LANES), F32)],
        compiler_params=_params(("parallel", "parallel"), 32),
        name="glu_dwconv",
    )(proj3, proj3, conv_w, conv_b.reshape(1, CONV_CH))


ATT_QB = 128
ATT_RAD = 64
ATT_KW = ATT_QB + 2 * ATT_RAD
ATT_PREP_ROWS = 512
ATT_UNROLL = 8


def _attn_kernel(slopes_ref, q_ref, k_ref, v_ref, qg_ref, kg_ref, o_ref,
                 qs_ref, kp_ref, vp_ref, ob_ref, mb_ref, *, seq, pad):
    pair = pl.program_id(1)
    lane = lax.broadcasted_iota(I32, (1, LANES), 1)
    lo = lane < HEAD_DIM
    scale = LOG2_E / math.sqrt(HEAD_DIM)
    spare = (HEAD_DIM, 0)

    zpad = jnp.zeros((pad, LANES), F32)
    for hh in range(2):
        kpad = jnp.broadcast_to(jnp.where(lane == spare[hh], NEG_INF, 0.0), (pad, LANES))
        kp_ref[hh, 0:pad, :] = kpad
        kp_ref[hh, pad + seq:2 * pad + seq, :] = kpad
        vp_ref[hh, 0:pad, :] = zpad
        vp_ref[hh, pad + seq:2 * pad + seq, :] = zpad

    def head_rms(x, g):
        x2 = x * x
        s_lo = jnp.sum(jnp.where(lo, x2, 0.0), axis=-1, keepdims=True)
        s_hi = jnp.sum(jnp.where(lo, 0.0, x2), axis=-1, keepdims=True)
        ms = jnp.where(lo, s_lo, s_hi) * (1.0 / HEAD_DIM)
        return x * lax.rsqrt(ms + EPS) * g

    def prep(i, carry):
        r = pl.multiple_of(i * ATT_PREP_ROWS, ATT_PREP_ROWS)
        rows = pl.ds(r, ATT_PREP_ROWS)
        prows = pl.ds(pad + r, ATT_PREP_ROWS)
        qn = head_rms(q_ref[0, rows, :], qg_ref[...]) * scale
        qs_ref[0, rows, :] = jnp.where(lo, qn, jnp.where(lane == spare[0], 1.0, 0.0))
        qs_ref[1, rows, :] = jnp.where(lo, jnp.where(lane == spare[1], 1.0, 0.0), qn)
        kn = head_rms(k_ref[0, rows, :], kg_ref[...])
        kp_ref[0, prows, :] = jnp.where(lo, kn, 0.0)
        kp_ref[1, prows, :] = jnp.where(lo, 0.0, kn)
        v = v_ref[0, rows, :]
        vp_ref[0, prows, :] = jnp.where(lo, v, jnp.where(lane == spare[0], 1.0, 0.0))
        vp_ref[1, prows, :] = jnp.where(lo, jnp.where(lane == spare[1], 1.0, 0.0), v)
        return carry

    lax.fori_loop(0, seq // ATT_PREP_ROWS, prep, 0)

    rowi = lax.broadcasted_iota(I32, (ATT_QB, ATT_KW), 0)
    coli = lax.broadcasted_iota(I32, (ATT_QB, ATT_KW), 1)
    arel = jnp.abs(coli - ATT_RAD - rowi)

    for hh in range(2):
        slope = slopes_ref[pair * 2 + hh] * LOG2_E
        for bi, (window, dil) in enumerate(DILATED):
            assert window // (2 * dil) == ATT_RAD
            nbc = seq // dil // ATT_QB
            bias = jnp.where(arel <= ATT_RAD, -(slope * dil) * arel.astype(F32), NEG_INF)

            def blk(j, carry, bi=bi, dil=dil, nbc=nbc, bias=bias, hh=hh):
                c = j // nbc
                n = j - c * nbc
                qstart = c + dil * ATT_QB * n
                kstart = pad + c + dil * (ATT_QB * n - ATT_RAD)
                if dil == 1:
                    qsl = pl.ds(qstart, ATT_QB)
                    ksl = pl.ds(kstart, ATT_KW)
                else:
                    qsl = pl.ds(qstart, ATT_QB, stride=dil)
                    ksl = pl.ds(kstart, ATT_KW, stride=dil)
                qb = qs_ref[hh, qsl, :].astype(BF16)
                kw = kp_ref[hh, ksl, :].astype(BF16)
                vw = vp_ref[hh, ksl, :].astype(BF16)
                sc = lax.dot_general(qb, kw, (((1,), (1,)), ((), ())),
                                     preferred_element_type=F32) + bias
                m = jnp.max(sc, axis=-1, keepdims=True)
                p = jnp.exp2(sc - m)
                ob_ref[bi, qsl, :] = jnp.dot(p.astype(BF16), vw, preferred_element_type=F32)
                mb_ref[bi, qsl, :] = jnp.broadcast_to(m, (ATT_QB, LANES))
                return carry

            lax.fori_loop(0, seq // ATT_QB, blk, 0, unroll=ATT_UNROLL)

        def combine(i, carry, hh=hh):
            r = pl.multiple_of(i * ATT_PREP_ROWS, ATT_PREP_ROWS)
            rows = pl.ds(r, ATT_PREP_ROWS)
            nbr = len(DILATED)
            ms = [mb_ref[bi, rows, :] for bi in range(nbr)]
            mx = functools.reduce(jnp.maximum, ms)
            acc = sum(jnp.exp2(ms[bi] - mx) * ob_ref[bi, rows, :] for bi in range(nbr))
            den = jnp.sum(jnp.where(lane == spare[hh], acc, 0.0), axis=-1, keepdims=True)
            res = acc / den
            if hh == 0:
                o_ref[0, rows, :] = res
            else:
                o_ref[0, rows, :] = jnp.where(lo, o_ref[0, rows, :], res)
            return carry

        lax.fori_loop(0, seq // ATT_PREP_ROWS, combine, 0)


def _dilated_attention(proj3, q_norm, k_norm):
    b, s, _ = proj3.shape
    pad = ATT_RAD * max(d for _, d in DILATED)
    npair = ATT_HEADS // 2
    qoff = 2 * CONV_CH // LANES
    slopes = jnp.asarray([2.0 ** (-8.0 * (h + 1) / ATT_HEADS) for h in range(ATT_HEADS)], F32)
    qg = jnp.tile(q_norm.reshape(1, HEAD_DIM), (1, 2))
    kg = jnp.tile(k_norm.reshape(1, HEAD_DIM), (1, 2))
    kernel = functools.partial(_attn_kernel, seq=s, pad=pad)
    return pl.pallas_call(
        kernel,
        grid=(b, npair),
        in_specs=[pl.BlockSpec(memory_space=pltpu.SMEM),
                  pl.BlockSpec((1, s, LANES), lambda i, j: (i, 0, qoff + j)),
                  pl.BlockSpec((1, s, LANES), lambda i, j: (i, 0, qoff + npair + j)),
                  pl.BlockSpec((1, s, LANES), lambda i, j: (i, 0, qoff + 2 * npair + j)),
                  pl.BlockSpec((1, LANES), lambda i, j: (0, 0)),
                  pl.BlockSpec((1, LANES), lambda i, j: (0, 0))],
        out_specs=pl.BlockSpec((1, s, LANES), lambda i, j: (i, 0, j)),
        out_shape=jax.ShapeDtypeStruct((b, s, ATT_WIDTH), F32),
        scratch_shapes=[pltpu.VMEM((2, s, LANES), F32),
                        pltpu.VMEM((2, s + 2 * pad, LANES), F32),
                        pltpu.VMEM((2, s + 2 * pad, LANES), F32),
                        pltpu.VMEM((len(DILATED), s, LANES), F32),
                        pltpu.VMEM((len(DILATED), s, LANES), F32)],
        compiler_params=_params(("parallel", "parallel"), 56),
        name="dilated_attn",
    )(slopes, proj3, proj3, proj3, qg, kg)


def _outproj_kernel(x_ref, conv_ref, att_ref, g_ref, b_ref, w1_ref, w2_ref, o_ref):
    c = conv_ref[...]
    mu = jnp.mean(c, axis=-1, keepdims=True)
    cc = c - mu
    var = jnp.mean(cc * cc, axis=-1, keepdims=True)
    hn = cc * lax.rsqrt(var + EPS) * g_ref[...] + b_ref[...]
    a = hn * jax.nn.sigmoid(hn)
    y = jnp.dot(a.astype(BF16), w1_ref[...], preferred_element_type=F32)
    y = y + jnp.dot(att_ref[...].astype(BF16), w2_ref[...], preferred_element_type=F32)
    o_ref[...] = x_ref[...] + y


def _out_projection(x2d, conv2d, att2d, ln_g, ln_b, w_out_bf16, tm=512):
    m, d = x2d.shape
    row = lambda i: (i, 0)
    fixed = lambda i: (0, 0)
    return pl.pallas_call(
        _outproj_kernel,
        grid=(m // tm,),
        in_specs=[pl.BlockSpec((tm, d), row),
                  pl.BlockSpec((tm, CONV_CH), row),
                  pl.BlockSpec((tm, ATT_WIDTH), row),
                  pl.BlockSpec((1, CONV_CH), fixed),
                  pl.BlockSpec((1, CONV_CH), fixed),
                  pl.BlockSpec((CONV_CH, d), fixed),
                  pl.BlockSpec((ATT_WIDTH, d), lambda i: (1, 0))],
        out_specs=pl.BlockSpec((tm, d), row),
        out_shape=jax.ShapeDtypeStruct((m, d), F32),
        compiler_params=_params(("parallel",), 32),
        name="ln_out_proj",
    )(x2d, conv2d, att2d, ln_g.reshape(1, CONV_CH), ln_b.reshape(1, CONV_CH), w_out_bf16, w_out_bf16)


def _even_mixer(x, mix_g, w_in, conv_w, conv_b, ln_g, ln_b, q_norm, k_norm, w_out):
    b, s, d = x.shape
    x2d = x.reshape(b * s, d)
    proj = _norm_matmul(x2d, mix_g, w_in.astype(BF16))
    proj3 = proj.reshape(b, s, -1)
    conv = _conformer_conv(proj3, conv_w, conv_b)
    att = _dilated_attention(proj3, q_norm, k_norm)
    out = _out_projection(x2d, conv.reshape(b * s, CONV_CH), att.reshape(b * s, ATT_WIDTH),
                          ln_g, ln_b, w_out.astype(BF16))
    return out.reshape(b, s, d)


def _router_kernel(x_ref, g_ref, wr_ref, br_ref, h_ref, aff_ref):
    h = _rmsnorm(x_ref[0], g_ref[...])
    logits = jnp.dot(h, wr_ref[...], precision=lax.Precision.HIGHEST,
                     preferred_element_type=F32) + br_ref[...]
    lt = logits.T[:N_EXPERTS, :]
    mx = jnp.max(lt, axis=0, keepdims=True)
    ex = jnp.exp(lt - mx)
    aff_ref[0] = ex / jnp.sum(ex, axis=0, keepdims=True)
    h_ref[0] = h.astype(BF16)


def _router(x, g, w_router, b_router, tm=512):
    b, s, d = x.shape
    wr = jnp.zeros((d, LANES), F32).at[:, :N_EXPERTS].set(w_router)
    br = jnp.zeros((1, LANES), F32).at[0, :N_EXPERTS].set(b_router)
    return pl.pallas_call(
        _router_kernel,
        grid=(b, s // tm),
        in_specs=[pl.BlockSpec((1, tm, d), lambda i, j: (i, j, 0)),
                  pl.BlockSpec((1, d), lambda i, j: (0, 0)),
                  pl.BlockSpec((d, LANES), lambda i, j: (0, 0)),
                  pl.BlockSpec((1, LANES), lambda i, j: (0, 0))],
        out_specs=[pl.BlockSpec((1, tm, d), lambda i, j: (i, j, 0)),
                   pl.BlockSpec((1, N_EXPERTS, tm), lambda i, j: (i, 0, j))],
        out_shape=[jax.ShapeDtypeStruct((b, s, d), BF16),
                   jax.ShapeDtypeStruct((b, N_EXPERTS, s), F32)],
        compiler_params=_params(("parallel", "parallel"), 32),
        name="moe_router",
    )(x, g.reshape(1, d), wr, br)


def _select_kernel(aff_ref, pos_ref, gsel_ref, post_ref, starts_ref, cs_ref, *, cap):
    a = aff_ref[0]
    s = a.shape[1]
    capf = float(cap)

    def count(mask):
        return jnp.sum(jnp.where(mask, 1.0, 0.0), axis=-1, keepdims=True)

    def enough(cand):
        return count(a >= cand) >= capf

    tiny = float(2.0 ** F32_MIN_EXP)
    thr = jnp.full((N_EXPERTS, 1), tiny, F32)
    normal = enough(thr)
    for j in range(6, -1, -1):
        cand = thr * float(2.0 ** (2 ** j))
        thr = jnp.where(enough(cand), cand, thr)
    step = thr * 0.5
    for _ in range(F32_MANTISSA_BITS):
        cand = thr + step
        thr = jnp.where(enough(cand), cand, thr)
        step = step * 0.5
    thr = jnp.where(normal, thr, 0.0)

    ri = lax.broadcasted_iota(I32, (LANES, LANES), 0)
    ci = lax.broadcasted_iota(I32, (LANES, LANES), 1)
    tri = jnp.where(ri < ci, 1.0, 0.0).astype(BF16)

    lane = lax.broadcasted_iota(I32, (1, LANES), 1)
    chunks_per_tile = MOE_SCATTER_TOKENS // LANES

    def exclusive_cumsum(mask):
        ind = jnp.where(mask, 1.0, 0.0)
        off = jnp.zeros((N_EXPERTS, 1), F32)
        starts = jnp.zeros((N_EXPERTS, LANES), F32)
        for j in range(s // LANES):
            if j % chunks_per_tile == 0:
                starts = jnp.where(lane == j // chunks_per_tile, off, starts)
            xc = ind[:, j * LANES:(j + 1) * LANES]
            cs_ref[:, j * LANES:(j + 1) * LANES] = (
                jnp.dot(xc.astype(BF16), tri, preferred_element_type=F32) + off)
            off = off + jnp.sum(xc, axis=-1, keepdims=True)
        starts = jnp.where(lane == s // MOE_SCATTER_TOKENS, off, starts)
        return cs_ref[...], starts

    gt = a > thr
    eq = a == thr
    need = capf - count(gt)
    tie_rank, _ = exclusive_cumsum(eq)
    sel = gt | (eq & (tie_rank < need))
    slot, starts = exclusive_cumsum(sel)
    slot = jnp.where(sel, slot, -1.0)
    pos_ref[0] = slot.astype(I32)
    gsel_ref[0] = jnp.where(sel, a, 0.0)
    starts_ref[0] = starts.astype(I32)
    fill = jnp.zeros((LANES - N_EXPERTS, s), F32)
    post_ref[0] = jnp.concatenate([slot, fill], axis=0).T


def _select(aff, cap):
    b, e, s = aff.shape
    assert s // MOE_SCATTER_TOKENS < LANES
    blk = pl.BlockSpec((1, e, s), lambda i: (i, 0, 0))
    return pl.pallas_call(
        functools.partial(_select_kernel, cap=cap),
        grid=(b,),
        in_specs=[blk],
        out_specs=[blk, blk,
                   pl.BlockSpec((1, s, LANES), lambda i: (i, 0, 0)),
                   pl.BlockSpec((1, e, LANES), lambda i: (i, 0, 0))],
        out_shape=[jax.ShapeDtypeStruct((b, e, s), I32),
                   jax.ShapeDtypeStruct((b, e, s), F32),
                   jax.ShapeDtypeStruct((b, s, LANES), F32),
                   jax.ShapeDtypeStruct((b, e, LANES), I32)],
        scratch_shapes=[pltpu.VMEM((e, s), F32)],
        compiler_params=_params(("parallel",), 32),
        name="moe_select",
    )(aff)


def _expert_kernel(h_ref, pos_ref, gsel_ref, wg_ref, wu_ref, wd_ref, y_ref, *, cap):
    s = h_ref.shape[1]
    slot = lax.broadcasted_iota(I32, (cap, MOE_GATHER_TOKENS), 0)
    x = None
    gate = None
    for k in range(s // MOE_GATHER_TOKENS):
        tok = slice(k * MOE_GATHER_TOKENS, (k + 1) * MOE_GATHER_TOKENS)
        hit = pos_ref[0, 0, :, tok] == slot
        part = jnp.dot(jnp.where(hit, 1.0, 0.0).astype(BF16), h_ref[0, tok, :],
                       preferred_element_type=F32)
        gpart = jnp.sum(jnp.where(hit, gsel_ref[0, 0, :, tok], 0.0), axis=-1, keepdims=True)
        x = part if x is None else x + part
        gate = gpart if gate is None else gate + gpart
    x = x.astype(BF16)
    g = jnp.dot(x, wg_ref[0, 0].astype(BF16), preferred_element_type=F32)
    u = jnp.dot(x, wu_ref[0, 0].astype(BF16), preferred_element_type=F32)
    a = (g * jax.nn.sigmoid(g) * u).astype(BF16)
    y = jnp.dot(a, wd_ref[0, 0].astype(BF16), preferred_element_type=F32)
    y_ref[0, 0] = (y * gate).astype(BF16)


def _experts(h, pos4, gsel4, w_gate, w_up, w_down, layer, cap):
    b, s, d = h.shape
    _, e, _, f = w_gate.shape
    return pl.pallas_call(
        functools.partial(_expert_kernel, cap=cap),
        grid=(b, e),
        in_specs=[pl.BlockSpec((1, s, d), lambda i, j: (i, 0, 0), pipeline_mode=pl.Buffered(1)),
                  pl.BlockSpec((1, 1, 1, s), lambda i, j: (i, j, 0, 0)),
                  pl.BlockSpec((1, 1, 1, s), lambda i, j: (i, j, 0, 0)),
                  pl.BlockSpec((1, 1, d, f), lambda i, j: (layer, j, 0, 0)),
                  pl.BlockSpec((1, 1, d, f), lambda i, j: (layer, j, 0, 0)),
                  pl.BlockSpec((1, 1, f, d), lambda i, j: (layer, j, 0, 0))],
        out_specs=pl.BlockSpec((1, 1, cap, d), lambda i, j: (i, j, 0, 0)),
        out_shape=jax.ShapeDtypeStruct((b, e, cap, d), BF16),
        compiler_params=_params(("parallel", "parallel"), 56),
        name="moe_experts",
    )(h, pos4, gsel4, w_gate, w_up, w_down)


def _scatter_kernel(starts_ref, x_ref, y_ref, post_ref, o_ref, *, cap):
    bi = pl.program_id(0)
    ti = pl.program_id(1)
    win = MOE_SCATTER_TOKENS
    lane = lax.broadcasted_iota(I32, (1, LANES), 1)
    wslot = lax.broadcasted_iota(I32, (1, win), 1).astype(F32)
    post = post_ref[0]

    def slot_column(e):
        return jnp.sum(jnp.where(lane == e, post, 0.0), axis=-1, keepdims=True)

    def window_start(e):
        first = starts_ref[bi, e, ti]
        off = jnp.minimum(lax.shift_left(lax.shift_right_logical(first, 4), 4), cap - win)
        return pl.multiple_of(off, 16)

    acc = x_ref[0]
    for e in range(N_EXPERTS):
        off = window_start(e)
        onehot = jnp.where(slot_column(e) - off.astype(F32) == wslot, 1.0, 0.0).astype(BF16)
        acc = acc + jnp.dot(onehot, y_ref[0, e, pl.ds(off, win), :], preferred_element_type=F32)
    o_ref[0] = acc

    tail = cap - win
    for e in range(N_EXPERTS):
        covered = window_start(e) + win

        @pl.when(starts_ref[bi, e, ti + 1] > covered)
        def _(e=e, covered=covered):
            pcol = slot_column(e)
            hit = (pcol - float(tail) == wslot) & (pcol >= covered.astype(F32))
            o_ref[0] += jnp.dot(jnp.where(hit, 1.0, 0.0).astype(BF16),
                                y_ref[0, e, tail:cap, :], preferred_element_type=F32)


def _scatter(x, y, post, starts, cap):
    b, s, d = x.shape
    e = y.shape[1]
    tn = MOE_SCATTER_TOKENS
    assert cap % 16 == 0 and tn <= cap <= 2 * tn
    grid_spec = pltpu.PrefetchScalarGridSpec(
        num_scalar_prefetch=1,
        grid=(b, s // tn),
        in_specs=[pl.BlockSpec((1, tn, d), lambda i, j, st: (i, j, 0)),
                  pl.BlockSpec((1, e, cap, d), lambda i, j, st: (i, 0, 0, 0),
                               pipeline_mode=pl.Buffered(1)),
                  pl.BlockSpec((1, tn, LANES), lambda i, j, st: (i, j, 0))],
        out_specs=pl.BlockSpec((1, tn, d), lambda i, j, st: (i, j, 0)),
    )
    return pl.pallas_call(
        functools.partial(_scatter_kernel, cap=cap),
        grid_spec=grid_spec,
        out_shape=jax.ShapeDtypeStruct((b, s, d), F32),
        compiler_params=_params(("parallel", "parallel"), 48),
        name="moe_scatter",
    )(starts, x, y, post)


def _moe(x, ffn_g, w_router, b_router, w_gate, w_up, w_down, layer):
    b, s, d = x.shape
    cap = CAPACITY_FACTOR * s // N_EXPERTS
    h, aff = _router(x, ffn_g, w_router, b_router)
    pos, gsel, post, starts = _select(aff, cap)
    y = _experts(h, pos.reshape(b, N_EXPERTS, 1, s), gsel.reshape(b, N_EXPERTS, 1, s),
                 w_gate, w_up, w_down, layer, cap)
    return _scatter(x, y, post, starts, cap)


SSM_CHUNK = 16


SSM_SLAB_GROUPS = LANES // SSM_GROUP
SSM_CHUNK_LANES = SSM_CHUNK * SSM_GROUP
SSM_HALF_STEPS = LANES // SSM_GROUP


def _transpose_lane_groups(xs):
    lgrp = lax.broadcasted_iota(I32, (1, LANES), 1) // SSM_GROUP
    xs = list(xs)
    assert len(xs) == SSM_SLAB_GROUPS
    h = SSM_SLAB_GROUPS // 2
    while h:
        upper = (lgrp & h) != 0
        for a0 in range(SSM_SLAB_GROUPS):
            if a0 & h:
                continue
            x0, x1 = xs[a0], xs[a0 + h]
            xs[a0] = jnp.where(upper, pltpu.roll(x1, h * SSM_GROUP, axis=1), x0)
            xs[a0 + h] = jnp.where(upper, x1, pltpu.roll(x0, LANES - h * SSM_GROUP, axis=1))
        h //= 2
    return xs


def _norm_flatten_kernel(x_ref, g_ref, ug_ref, u_scr):
    tm, d = x_ref.shape
    nch = tm // SSM_CHUNK
    u = _rmsnorm(x_ref[...], g_ref[...])
    lgrp = lax.broadcasted_iota(I32, (1, LANES), 1) // SSM_GROUP
    for slab in range(d // LANES):
        u_scr[slab] = u[:, slab * LANES:(slab + 1) * LANES]
        steps = [u_scr[slab, pl.ds(t, nch, stride=SSM_CHUNK), :] for t in range(SSM_CHUNK)]
        for gi in range(SSM_SLAB_GROUPS):
            for half in range(SSM_CHUNK_LANES // LANES):
                acc = None
                for tt in range(SSM_HALF_STEPS):
                    shift = ((tt - gi) * SSM_GROUP) % LANES
                    piece = steps[half * SSM_HALF_STEPS + tt]
                    if shift:
                        piece = pltpu.roll(piece, shift, axis=1)
                    acc = piece if acc is None else jnp.where(lgrp == tt, piece, acc)
                ug_ref[slab * SSM_SLAB_GROUPS + gi, :, half * LANES:(half + 1) * LANES] = acc.astype(BF16)


def _norm_flatten(x2d, g, tm=512):
    m, d = x2d.shape
    ngrp = d // SSM_GROUP
    nch = tm // SSM_CHUNK
    return pl.pallas_call(
        _norm_flatten_kernel,
        grid=(m // tm,),
        in_specs=[pl.BlockSpec((tm, d), lambda i: (i, 0)),
                  pl.BlockSpec((1, d), lambda i: (0, 0))],
        out_specs=pl.BlockSpec((ngrp, nch, SSM_CHUNK_LANES), lambda i: (0, i, 0)),
        out_shape=jax.ShapeDtypeStruct((ngrp, m // SSM_CHUNK, SSM_CHUNK_LANES), BF16),
        scratch_shapes=[pltpu.VMEM((d // LANES, tm, LANES), F32)],
        compiler_params=_params(("parallel",), 32),
        name="s5_norm_flatten",
    )(x2d, g.reshape(1, d))


def _s5_core_kernel(u_ref, m_ref, w_ref, v_ref, a_ref, y_ref, *, nc):
    u = u_ref[0]
    n_rows = u.shape[0]
    y = jnp.dot(u, m_ref[0], preferred_element_type=F32)
    x = jnp.dot(u, w_ref[0], preferred_element_type=F32)
    hr = x[:, :LANES]
    hi = x[:, LANES:]
    ar = a_ref[0, 0:1, :]
    ai = a_ref[0, 1:2, :]
    lane = lax.broadcasted_iota(I32, (1, LANES), 1)
    fwd = lane < SSM_STATE
    rown = lax.broadcasted_iota(I32, (n_rows, 1), 0) % nc

    def shifted(val, step):
        down = jnp.where(rown >= step, pltpu.roll(val, step, axis=0), 0.0)
        up = jnp.where(rown < nc - step, pltpu.roll(val, n_rows - step, axis=0), 0.0)
        return jnp.where(fwd, down, up)

    step = 1
    while step < nc:
        pr = shifted(hr, step)
        pi = shifted(hi, step)
        hr, hi = hr + ar * pr - ai * pi, hi + ar * pi + ai * pr
        ar, ai = ar * ar - ai * ai, 2.0 * ar * ai
        step *= 2
    hin_r = shifted(hr, 1).astype(BF16)
    hin_i = shifted(hi, 1).astype(BF16)
    y = y + jnp.dot(hin_r, v_ref[0, :LANES, :], preferred_element_type=F32)
    y = y + jnp.dot(hin_i, v_ref[0, LANES:, :], preferred_element_type=F32)
    y_ref[0] = y


def _s5_core(ug, mmat, wmat, vmat, amat, nc):
    g, n, k = ug.shape
    blk = lambda r, c: pl.BlockSpec((1, r, c), lambda i: (i, 0, 0))
    return pl.pallas_call(
        functools.partial(_s5_core_kernel, nc=nc),
        grid=(g,),
        in_specs=[blk(n, k), blk(k, k), blk(k, 2 * LANES), blk(2 * LANES, k), blk(8, LANES)],
        out_specs=blk(n, k),
        out_shape=jax.ShapeDtypeStruct((g, n, k), F32),
        compiler_params=_params(("parallel",), 48),
        name="s5_core",
    )(ug, mmat, wmat, vmat, amat)


def _s5_post_kernel(x_ref, g_ref, yg_first_ref, yg_next_ref, d_ref, w_ref, o_ref, y_scr):
    step = pl.program_id(0)
    x = x_ref[...]
    tm, d = x.shape
    nch = tm // SSM_CHUNK

    def unflatten(yg_ref, slot):
        for slab in range(d // LANES):
            for half in range(SSM_CHUNK_LANES // LANES):
                grps = [yg_ref[slab * SSM_SLAB_GROUPS + gi, :, half * LANES:(half + 1) * LANES]
                        for gi in range(SSM_SLAB_GROUPS)]
                for tt, rows in enumerate(_transpose_lane_groups(grps)):
                    y_scr[slot, slab, pl.ds(half * SSM_HALF_STEPS + tt, nch, stride=SSM_CHUNK), :] = rows

    @pl.when(step == 0)
    def _():
        unflatten(yg_first_ref, 0)

    cur = step % 2
    y = jnp.concatenate([y_scr[cur, slab] for slab in range(d // LANES)], axis=1)
    unflatten(yg_next_ref, 1 - cur)
    hf = _rmsnorm(x, g_ref[...])
    z = jax.nn.gelu(y + d_ref[...] * hf)
    vg = jnp.dot(z.astype(BF16), w_ref[...], preferred_element_type=F32)
    o_ref[...] = x + vg[:, :d] * jax.nn.sigmoid(vg[:, d:])


def _s5_post(x2d, g, yg, d_skip, w_glu_bf16, tm=512):
    m, d = x2d.shape
    ngrp = d // SSM_GROUP
    nch = tm // SSM_CHUNK
    last = m // tm - 1
    row = lambda i: (i, 0)
    fixed = lambda i: (0, 0)
    return pl.pallas_call(
        _s5_post_kernel,
        grid=(m // tm,),
        in_specs=[pl.BlockSpec((tm, d), row), pl.BlockSpec((1, d), fixed),
                  pl.BlockSpec((ngrp, nch, SSM_CHUNK_LANES), lambda i: (0, 0, 0)),
                  pl.BlockSpec((ngrp, nch, SSM_CHUNK_LANES), lambda i: (0, jnp.minimum(i + 1, last), 0)),
                  pl.BlockSpec((1, d), fixed),
                  pl.BlockSpec((d, 2 * d), fixed)],
        out_specs=pl.BlockSpec((tm, d), row),
        out_shape=jax.ShapeDtypeStruct((m, d), F32),
        scratch_shapes=[pltpu.VMEM((2, d // LANES, tm, LANES), F32)],
        compiler_params=_params(("arbitrary",), 48),
        name="s5_gelu_glu",
    )(x2d, g.reshape(1, d), yg, yg, d_skip.reshape(1, d), w_glu_bf16)


def _s5_matrices(lam_re, lam_im, log_dt, b_re, b_im, c_re, c_im):
    t = SSM_CHUNK
    g, p = lam_re.shape[1:]
    c = SSM_GROUP
    dt = jnp.exp(log_dt)[..., None]
    den = lam_re * lam_re + lam_im * lam_im
    mag = jnp.exp(lam_re * dt)
    ang = lam_im * dt
    lb_re = mag * jnp.cos(ang)
    lb_im = mag * jnp.sin(ang)
    nr = lb_re - 1.0
    f_re = (nr * lam_re + lb_im * lam_im) / den
    f_im = (lb_im * lam_re - nr * lam_im) / den
    bb_re = f_re[..., None] * b_re - f_im[..., None] * b_im
    bb_im = f_re[..., None] * b_im + f_im[..., None] * b_re
    lag = jnp.arange(t + 1, dtype=F32)[:, None, None, None]
    pmag = jnp.exp(lag * (lam_re * dt))
    pw_re = pmag * jnp.cos(lag * ang)
    pw_im = pmag * jnp.sin(lag * ang)
    cl_re = c_re[None] * pw_re[..., None, :] - c_im[None] * pw_im[..., None, :]
    cl_im = c_re[None] * pw_im[..., None, :] + c_im[None] * pw_re[..., None, :]
    kern = (jnp.einsum('ldgcp,dgpe->ldgce', cl_re[:t], bb_re)
            - jnp.einsum('ldgcp,dgpe->ldgce', cl_im[:t], bb_im))
    tt = jnp.arange(t)
    by_lag = jnp.concatenate([kern[:0:-1, 1], (kern[0, 0] + kern[0, 1])[None], kern[1:, 0]], axis=0)
    toeplitz = jnp.stack([by_lag[t - 1 - s:2 * t - 1 - s] for s in range(t)], axis=0)
    mmat = jnp.transpose(toeplitz, (2, 0, 4, 1, 3)).reshape(g, t * c, t * c)
    wf_re = pw_re[t - 1 - tt, 0][..., None] * bb_re[0][None]
    wf_im = pw_im[t - 1 - tt, 0][..., None] * bb_re[0][None]
    wf_re = wf_re - pw_im[t - 1 - tt, 0][..., None] * bb_im[0][None]
    wf_im = wf_im + pw_re[t - 1 - tt, 0][..., None] * bb_im[0][None]
    wb_re = pw_re[tt, 1][..., None] * bb_re[1][None] - pw_im[tt, 1][..., None] * bb_im[1][None]
    wb_im = pw_im[tt, 1][..., None] * bb_re[1][None] + pw_re[tt, 1][..., None] * bb_im[1][None]
    wmat = jnp.concatenate([wf_re, wb_re, wf_im, wb_im], axis=2)
    wmat = jnp.transpose(wmat, (1, 0, 3, 2)).reshape(g, t * c, 4 * p)
    vf_re = cl_re[tt + 1, 0]
    vf_im = -cl_im[tt + 1, 0]
    vb_re = cl_re[t - tt, 1]
    vb_im = -cl_im[t - tt, 1]
    vmat = jnp.concatenate([vf_re, vb_re, vf_im, vb_im], axis=3)
    vmat = jnp.transpose(vmat, (1, 3, 0, 2)).reshape(g, 4 * p, t * c)
    a_re = jnp.concatenate([pw_re[t, 0], pw_re[t, 1]], axis=-1)
    a_im = jnp.concatenate([pw_im[t, 0], pw_im[t, 1]], axis=-1)
    amat = jnp.zeros((g, 8, 2 * p), F32).at[:, 0].set(a_re).at[:, 1].set(a_im)
    return mmat.astype(BF16), wmat.astype(BF16), vmat.astype(BF16), amat


def _s5_mixer(x, mix_g, lam_re, lam_im, log_dt, b_re, b_im, c_re, c_im, d_skip, w_glu):
    b, s, d = x.shape
    t = SSM_CHUNK
    g = d // SSM_GROUP
    nc = s // t
    x2d = x.reshape(b * s, d)
    ug = _norm_flatten(x2d, mix_g)
    mmat, wmat, vmat, amat = _s5_matrices(lam_re, lam_im, log_dt, b_re, b_im, c_re, c_im)
    yg = _s5_core(ug, mmat, wmat, vmat, amat, nc)
    out = _s5_post(x2d, mix_g, yg, d_skip, w_glu.astype(BF16))
    return out.reshape(b, s, d)


def kernel(x, mix_norm_even, w_in, conv_w, conv_b, conv_ln_g, conv_ln_b, q_norm, k_norm, w_out, mix_norm_odd, ssm_lam_re, ssm_lam_im, ssm_log_dt, ssm_b_re, ssm_b_im, ssm_c_re, ssm_c_im, ssm_d, w_glu, ffn_norm, w_router, b_router, w_e_gate, w_e_up, w_e_down):
    depth = ffn_norm.shape[0]
    for layer in range(depth):
        i = layer // 2
        if layer % 2 == 0:
            x = _even_mixer(x, mix_norm_even[i], w_in[i], conv_w[i], conv_b[i], conv_ln_g[i],
                            conv_ln_b[i], q_norm[i], k_norm[i], w_out[i])
        else:
            x = _s5_mixer(x, mix_norm_odd[i], ssm_lam_re[i], ssm_lam_im[i], ssm_log_dt[i],
                          ssm_b_re[i], ssm_b_im[i], ssm_c_re[i], ssm_c_im[i], ssm_d[i], w_glu[i])
        x = _moe(x, ffn_norm[layer], w_router[layer], b_router[layer],
                 w_e_gate, w_e_up, w_e_down, layer)
    return x
```

```python
import functools
import math

import jax
import jax.numpy as jnp
from jax import lax
from jax.experimental import pallas as pl
from jax.experimental.pallas import tpu as pltpu

F32 = jnp.float32
BF16 = jnp.bfloat16
I32 = jnp.int32

EPS = 1e-6
NEG_INF = -1e30
LOG2_E = math.log2(math.e)

CONV_CH = 512
CONV_WIDTH = 31
ATT_HEADS = 8
HEAD_DIM = 64
ATT_WIDTH = ATT_HEADS * HEAD_DIM
DILATED = ((128, 1), (512, 4), (2048, 16))
SSM_GROUP = 16
SSM_STATE = 64
N_EXPERTS = 16
CAPACITY_FACTOR = 2
MOE_GATHER_TOKENS = 1024
MOE_SCATTER_TOKENS = 256

LANES = 128
F32_MIN_EXP = -126
F32_MANTISSA_BITS = 23
MIB = 1024 * 1024


def _params(sem, vmem_mib):
    return pltpu.CompilerParams(dimension_semantics=sem, vmem_limit_bytes=vmem_mib * MIB)


def _rmsnorm(x, g):
    return x * lax.rsqrt(jnp.mean(x * x, axis=-1, keepdims=True) + EPS) * g


def _norm_matmul_kernel(x_ref, g_ref, w_ref, o_ref):
    y = _rmsnorm(x_ref[...], g_ref[...])
    o_ref[...] = jnp.dot(y.astype(BF16), w_ref[...], preferred_element_type=F32)


def _norm_matmul(x2d, g, w_bf16, tm=512):
    m, d = x2d.shape
    n = w_bf16.shape[1]
    return pl.pallas_call(
        _norm_matmul_kernel,
        grid=(m // tm,),
        in_specs=[pl.BlockSpec((tm, d), lambda i: (i, 0)),
                  pl.BlockSpec((1, d), lambda i: (0, 0)),
                  pl.BlockSpec((d, n), lambda i: (0, 0))],
        out_specs=pl.BlockSpec((tm, n), lambda i: (i, 0)),
        out_shape=jax.ShapeDtypeStruct((m, n), F32),
        compiler_params=_params(("parallel",), 48),
        name="norm_in_proj",
    )(x2d, g.reshape(1, d), w_bf16)


CONV_HALO = 16
CONV_ROWS = 256


def _conv_kernel(val_ref, gate_ref, w_ref, b_ref, o_ref, hp_ref):
    s = val_ref.shape[1]
    zeros = jnp.zeros((CONV_HALO, LANES), F32)
    hp_ref[0:CONV_HALO, :] = zeros
    hp_ref[CONV_HALO + s:2 * CONV_HALO + s, :] = zeros
    hp_ref[CONV_HALO:CONV_HALO + s, :] = val_ref[0] * jax.nn.sigmoid(gate_ref[0])
    first = CONV_HALO - CONV_WIDTH // 2
    for c in range(s // CONV_ROWS):
        r0 = c * CONV_ROWS
        acc = jnp.broadcast_to(b_ref[...], (CONV_ROWS, LANES))
        for k in range(CONV_WIDTH):
            acc = acc + w_ref[k:k + 1, :] * hp_ref[r0 + first + k:r0 + first + k + CONV_ROWS, :]
        o_ref[0, r0:r0 + CONV_ROWS, :] = acc


def _conformer_conv(proj3, conv_w, conv_b):
    b, s, _ = proj3.shape
    nct = CONV_CH // LANES
    return pl.pallas_call(
        _conv_kernel,
        grid=(b, nct),
        in_specs=[pl.BlockSpec((1, s, LANES), lambda i, j: (i, 0, j)),
                  pl.BlockSpec((1, s, LANES), lambda i, j: (i, 0, nct + j)),
                  pl.BlockSpec((CONV_WIDTH, LANES), lambda i, j: (0, j)),
                  pl.BlockSpec((1, LANES), lambda i, j: (0, j))],
        out_specs=pl.BlockSpec((1, s, LANES), lambda i, j: (i, 0, j)),
        out_shape=jax.ShapeDtypeStruct((b, s, CONV_CH), F32),
        scratch_shapes=[pltpu.VMEM((s + 2 * CONV_HALO, LANES), F32)],
        compiler_params=_params(("parallel", "parallel"), 32),
        name="glu_dwconv",
    )(proj3, proj3, conv_w, conv_b.reshape(1, CONV_CH))


ATT_QB = 128
ATT_RAD = 64
ATT_KW = ATT_QB + 2 * ATT_RAD
ATT_PREP_ROWS = 512
ATT_UNROLL = 8


def _attn_kernel(slopes_ref, q_ref, k_ref, v_ref, qg_ref, kg_ref, o_ref,
                 qs_ref, kp_ref, vp_ref, ob_ref, mb_ref, *, seq, pad):
    pair = pl.program_id(1)
    lane = lax.broadcasted_iota(I32, (1, LANES), 1)
    lo = lane < HEAD_DIM
    scale = LOG2_E / math.sqrt(HEAD_DIM)
    spare = (HEAD_DIM, 0)

    zpad = jnp.zeros((pad, LANES), F32)
    for hh in range(2):
        kpad = jnp.broadcast_to(jnp.where(lane == spare[hh], NEG_INF, 0.0), (pad, LANES))
        kp_ref[hh, 0:pad, :] = kpad
        kp_ref[hh, pad + seq:2 * pad + seq, :] = kpad
        vp_ref[hh, 0:pad, :] = zpad
        vp_ref[hh, pad + seq:2 * pad + seq, :] = zpad

    def head_rms(x, g):
        x2 = x * x
        s_lo = jnp.sum(jnp.where(lo, x2, 0.0), axis=-1, keepdims=True)
        s_hi = jnp.sum(jnp.where(lo, 0.0, x2), axis=-1, keepdims=True)
        ms = jnp.where(lo, s_lo, s_hi) * (1.0 / HEAD_DIM)
        return x * lax.rsqrt(ms + EPS) * g

    def prep(i, carry):
        r = pl.multiple_of(i * ATT_PREP_ROWS, ATT_PREP_ROWS)
        rows = pl.ds(r, ATT_PREP_ROWS)
        prows = pl.ds(pad + r, ATT_PREP_ROWS)
        qn = head_rms(q_ref[0, rows, :], qg_ref[...]) * scale
        qs_ref[0, rows, :] = jnp.where(lo, qn, jnp.where(lane == spare[0], 1.0, 0.0))
        qs_ref[1, rows, :] = jnp.where(lo, jnp.where(lane == spare[1], 1.0, 0.0), qn)
        kn = head_rms(k_ref[0, rows, :], kg_ref[...])
        kp_ref[0, prows, :] = jnp.where(lo, kn, 0.0)
        kp_ref[1, prows, :] = jnp.where(lo, 0.0, kn)
        v = v_ref[0, rows, :]
        vp_ref[0, prows, :] = jnp.where(lo, v, jnp.where(lane == spare[0], 1.0, 0.0))
        vp_ref[1, prows, :] = jnp.where(lo, jnp.where(lane == spare[1], 1.0, 0.0), v)
        return carry

    lax.fori_loop(0, seq // ATT_PREP_ROWS, prep, 0)

    rowi = lax.broadcasted_iota(I32, (ATT_QB, ATT_KW), 0)
    coli = lax.broadcasted_iota(I32, (ATT_QB, ATT_KW), 1)
    arel = jnp.abs(coli - ATT_RAD - rowi)

    for hh in range(2):
        slope = slopes_ref[pair * 2 + hh] * LOG2_E
        for bi, (window, dil) in enumerate(DILATED):
            assert window // (2 * dil) == ATT_RAD
            nbc = seq // dil // ATT_QB
            bias = jnp.where(arel <= ATT_RAD, -(slope * dil) * arel.astype(F32), NEG_INF)

            def blk(j, carry, bi=bi, dil=dil, nbc=nbc, bias=bias, hh=hh):
                c = j // nbc
                n = j - c * nbc
                qstart = c + dil * ATT_QB * n
                kstart = pad + c + dil * (ATT_QB * n - ATT_RAD)
                if dil == 1:
                    qsl = pl.ds(qstart, ATT_QB)
                    ksl = pl.ds(kstart, ATT_KW)
                else:
                    qsl = pl.ds(qstart, ATT_QB, stride=dil)
                    ksl = pl.ds(kstart, ATT_KW, stride=dil)
                qb = qs_ref[hh, qsl, :].astype(BF16)
                kw = kp_ref[hh, ksl, :].astype(BF16)
                vw = vp_ref[hh, ksl, :].astype(BF16)
                sc = lax.dot_general(qb, kw, (((1,), (1,)), ((), ())),
                                     preferred_element_type=F32) + bias
                m = jnp.max(sc, axis=-1, keepdims=True)
                p = jnp.exp2(sc - m)
                ob_ref[bi, qsl, :] = jnp.dot(p.astype(BF16), vw, preferred_element_type=F32)
                mb_ref[bi, qsl, :] = jnp.broadcast_to(m, (ATT_QB, LANES))
                return carry

            lax.fori_loop(0, seq // ATT_QB, blk, 0, unroll=ATT_UNROLL)

        def combine(i, carry, hh=hh):
            r = pl.multiple_of(i * ATT_PREP_ROWS, ATT_PREP_ROWS)
            rows = pl.ds(r, ATT_PREP_ROWS)
            nbr = len(DILATED)
            ms = [mb_ref[bi, rows, :] for bi in range(nbr)]
            mx = functools.reduce(jnp.maximum, ms)
            acc = sum(jnp.exp2(ms[bi] - mx) * ob_ref[bi, rows, :] for bi in range(nbr))
            den = jnp.sum(jnp.where(lane == spare[hh], acc, 0.0), axis=-1, keepdims=True)
            res = acc / den
            if hh == 0:
                o_ref[0, rows, :] = res
            else:
                o_ref[0, rows, :] = jnp.where(lo, o_ref[0, rows, :], res)
            return carry

        lax.fori_loop(0, seq // ATT_PREP_ROWS, combine, 0)


def _dilated_attention(proj3, q_norm, k_norm):
    b, s, _ = proj3.shape
    pad = ATT_RAD * max(d for _, d in DILATED)
    npair = ATT_HEADS // 2
    qoff = 2 * CONV_CH // LANES
    slopes = jnp.asarray([2.0 ** (-8.0 * (h + 1) / ATT_HEADS) for h in range(ATT_HEADS)], F32)
    qg = jnp.tile(q_norm.reshape(1, HEAD_DIM), (1, 2))
    kg = jnp.tile(k_norm.reshape(1, HEAD_DIM), (1, 2))
    kernel = functools.partial(_attn_kernel, seq=s, pad=pad)
    return pl.pallas_call(
        kernel,
        grid=(b, npair),
        in_specs=[pl.BlockSpec(memory_space=pltpu.SMEM),
                  pl.BlockSpec((1, s, LANES), lambda i, j: (i, 0, qoff + j)),
                  pl.BlockSpec((1, s, LANES), lambda i, j: (i, 0, qoff + npair + j)),
                  pl.BlockSpec((1, s, LANES), lambda i, j: (i, 0, qoff + 2 * npair + j)),
                  pl.BlockSpec((1, LANES), lambda i, j: (0, 0)),
                  pl.BlockSpec((1, LANES), lambda i, j: (0, 0))],
        out_specs=pl.BlockSpec((1, s, LANES), lambda i, j: (i, 0, j)),
        out_shape=jax.ShapeDtypeStruct((b, s, ATT_WIDTH), F32),
        scratch_shapes=[pltpu.VMEM((2, s, LANES), F32),
                        pltpu.VMEM((2, s + 2 * pad, LANES), F32),
                        pltpu.VMEM((2, s + 2 * pad, LANES), F32),
                        pltpu.VMEM((len(DILATED), s, LANES), F32),
                        pltpu.VMEM((len(DILATED), s, LANES), F32)],
        compiler_params=_params(("parallel", "parallel"), 56),
        name="dilated_attn",
    )(slopes, proj3, proj3, proj3, qg, kg)


def _outproj_kernel(x_ref, conv_ref, att_ref, g_ref, b_ref, w1_ref, w2_ref, o_ref):
    c = conv_ref[...]
    mu = jnp.mean(c, axis=-1, keepdims=True)
    cc = c - mu
    var = jnp.mean(cc * cc, axis=-1, keepdims=True)
    hn = cc * lax.rsqrt(var + EPS) * g_ref[...] + b_ref[...]
    a = hn * jax.nn.sigmoid(hn)
    y = jnp.dot(a.astype(BF16), w1_ref[...], preferred_element_type=F32)
    y = y + jnp.dot(att_ref[...].astype(BF16), w2_ref[...], preferred_element_type=F32)
    o_ref[...] = x_ref[...] + y


def _out_projection(x2d, conv2d, att2d, ln_g, ln_b, w_out_bf16, tm=512):
    m, d = x2d.shape
    row = lambda i: (i, 0)
    fixed = lambda i: (0, 0)
    return pl.pallas_call(
        _outproj_kernel,
        grid=(m // tm,),
        in_specs=[pl.BlockSpec((tm, d), row),
                  pl.BlockSpec((tm, CONV_CH), row),
                  pl.BlockSpec((tm, ATT_WIDTH), row),
                  pl.BlockSpec((1, CONV_CH), fixed),
                  pl.BlockSpec((1, CONV_CH), fixed),
                  pl.BlockSpec((CONV_CH, d), fixed),
                  pl.BlockSpec((ATT_WIDTH, d), lambda i: (1, 0))],
        out_specs=pl.BlockSpec((tm, d), row),
        out_shape=jax.ShapeDtypeStruct((m, d), F32),
        compiler_params=_params(("parallel",), 32),
        name="ln_out_proj",
    )(x2d, conv2d, att2d, ln_g.reshape(1, CONV_CH), ln_b.reshape(1, CONV_CH), w_out_bf16, w_out_bf16)


def _even_mixer(x, mix_g, w_in, conv_w, conv_b, ln_g, ln_b, q_norm, k_norm, w_out):
    b, s, d = x.shape
    x2d = x.reshape(b * s, d)
    proj = _norm_matmul(x2d, mix_g, w_in.astype(BF16))
    proj3 = proj.reshape(b, s, -1)
    conv = _conformer_conv(proj3, conv_w, conv_b)
    att = _dilated_attention(proj3, q_norm, k_norm)
    out = _out_projection(x2d, conv.reshape(b * s, CONV_CH), att.reshape(b * s, ATT_WIDTH),
                          ln_g, ln_b, w_out.astype(BF16))
    return out.reshape(b, s, d)


def _router_kernel(x_ref, g_ref, wr_ref, br_ref, h_ref, aff_ref):
    h = _rmsnorm(x_ref[0], g_ref[...])
    logits = jnp.dot(h, wr_ref[...], precision=lax.Precision.HIGHEST,
                     preferred_element_type=F32) + br_ref[...]
    lt = logits.T[:N_EXPERTS, :]
    mx = jnp.max(lt, axis=0, keepdims=True)
    ex = jnp.exp(lt - mx)
    aff_ref[0] = ex / jnp.sum(ex, axis=0, keepdims=True)
    h_ref[0] = h


def _router(x, g, w_router, b_router, tm=512):
    b, s, d = x.shape
    wr = jnp.zeros((d, LANES), F32).at[:, :N_EXPERTS].set(w_router)
    br = jnp.zeros((1, LANES), F32).at[0, :N_EXPERTS].set(b_router)
    return pl.pallas_call(
        _router_kernel,
        grid=(b, s // tm),
        in_specs=[pl.BlockSpec((1, tm, d), lambda i, j: (i, j, 0)),
                  pl.BlockSpec((1, d), lambda i, j: (0, 0)),
                  pl.BlockSpec((d, LANES), lambda i, j: (0, 0)),
                  pl.BlockSpec((1, LANES), lambda i, j: (0, 0))],
        out_specs=[pl.BlockSpec((1, tm, d), lambda i, j: (i, j, 0)),
                   pl.BlockSpec((1, N_EXPERTS, tm), lambda i, j: (i, 0, j))],
        out_shape=[jax.ShapeDtypeStruct((b, s, d), F32),
                   jax.ShapeDtypeStruct((b, N_EXPERTS, s), F32)],
        compiler_params=_params(("parallel", "parallel"), 32),
        name="moe_router",
    )(x, g.reshape(1, d), wr, br)


def _select_kernel(aff_ref, pos_ref, gsel_ref, post_ref, starts_ref, idx_ref, cs_ref, *, cap):
    a = aff_ref[0]
    s = a.shape[1]
    capf = float(cap)

    def count(mask):
        return jnp.sum(jnp.where(mask, 1.0, 0.0), axis=-1, keepdims=True)

    def enough(cand):
        return count(a >= cand) >= capf

    tiny = float(2.0 ** F32_MIN_EXP)
    thr = jnp.full((N_EXPERTS, 1), tiny, F32)
    normal = enough(thr)
    for j in range(6, -1, -1):
        cand = thr * float(2.0 ** (2 ** j))
        thr = jnp.where(enough(cand), cand, thr)
    step = thr * 0.5
    for _ in range(F32_MANTISSA_BITS):
        cand = thr + step
        thr = jnp.where(enough(cand), cand, thr)
        step = step * 0.5
    thr = jnp.where(normal, thr, 0.0)

    ri = lax.broadcasted_iota(I32, (LANES, LANES), 0)
    ci = lax.broadcasted_iota(I32, (LANES, LANES), 1)
    tri = jnp.where(ri < ci, 1.0, 0.0).astype(BF16)

    lane = lax.broadcasted_iota(I32, (1, LANES), 1)
    chunks_per_tile = MOE_SCATTER_TOKENS // LANES

    def exclusive_cumsum(mask):
        ind = jnp.where(mask, 1.0, 0.0)
        off = jnp.zeros((N_EXPERTS, 1), F32)
        starts = jnp.zeros((N_EXPERTS, LANES), F32)
        for j in range(s // LANES):
            if j % chunks_per_tile == 0:
                starts = jnp.where(lane == j // chunks_per_tile, off, starts)
            xc = ind[:, j * LANES:(j + 1) * LANES]
            cs_ref[:, j * LANES:(j + 1) * LANES] = (
                jnp.dot(xc.astype(BF16), tri, preferred_element_type=F32) + off)
            off = off + jnp.sum(xc, axis=-1, keepdims=True)
        starts = jnp.where(lane == s // MOE_SCATTER_TOKENS, off, starts)
        return cs_ref[...], starts

    gt = a > thr
    eq = a == thr
    need = capf - count(gt)
    tie_rank, _ = exclusive_cumsum(eq)
    sel = gt | (eq & (tie_rank < need))
    slot, starts = exclusive_cumsum(sel)
    slot = jnp.where(sel, slot, -1.0)
    pos_ref[0] = slot.astype(I32)
    gsel_ref[0] = jnp.where(sel, a, 0.0)
    starts_ref[0] = starts.astype(I32)
    fill = jnp.zeros((LANES - N_EXPERTS, s), F32)
    post_ref[0] = jnp.concatenate([slot, fill], axis=0).T

    radix = float(s)
    tok = lax.broadcasted_iota(I32, (N_EXPERTS, s), 1).astype(F32)
    packed = jnp.where(sel, (tok - slot) * radix + tok, -1.0)
    for k in range(s.bit_length() - 1):
        dist = jnp.floor(packed * (1.0 / radix))
        bit = jnp.floor(dist * 0.5 ** k) - 2.0 * jnp.floor(dist * 0.5 ** (k + 1))
        moving = jnp.where(packed >= 0.0, bit, 0.0) == 1.0
        arrived = pltpu.roll(jnp.where(moving, packed, -1.0), s - 2 ** k, axis=1)
        packed = jnp.where(arrived >= 0.0, arrived, jnp.where(moving, -1.0, packed))
    head = packed[:, :cap]
    idx_ref[0] = (head - radix * jnp.floor(head * (1.0 / radix))).astype(I32)


def _select(aff, cap):
    b, e, s = aff.shape
    assert s // MOE_SCATTER_TOKENS < LANES
    assert s & (s - 1) == 0 and s * s <= 2 ** 24
    blk = pl.BlockSpec((1, e, s), lambda i: (i, 0, 0))
    return pl.pallas_call(
        functools.partial(_select_kernel, cap=cap),
        grid=(b,),
        in_specs=[blk],
        out_specs=[blk, blk,
                   pl.BlockSpec((1, s, LANES), lambda i: (i, 0, 0)),
                   pl.BlockSpec((1, e, LANES), lambda i: (i, 0, 0)),
                   pl.BlockSpec((1, e, cap), lambda i: (i, 0, 0))],
        out_shape=[jax.ShapeDtypeStruct((b, e, s), I32),
                   jax.ShapeDtypeStruct((b, e, s), F32),
                   jax.ShapeDtypeStruct((b, s, LANES), F32),
                   jax.ShapeDtypeStruct((b, e, LANES), I32),
                   jax.ShapeDtypeStruct((b, e, cap), I32)],
        scratch_shapes=[pltpu.VMEM((e, s), F32)],
        compiler_params=_params(("parallel",), 32),
        name="moe_select",
    )(aff)


def _gather_kernel(idx_ref, h_ref, x_ref, rows_scr, *, cap):
    group = 16

    def copy_group(gidx, carry):
        base = pl.multiple_of(gidx * group, group)
        for r in range(group):
            tok = idx_ref[0, 0, 0, base + r]
            rows_scr[pl.ds(base + r, 1), :] = h_ref[0, pl.ds(tok, 1), :]
        return carry

    lax.fori_loop(0, cap // group, copy_group, 0)
    x_ref[0, 0] = rows_scr[...].astype(BF16)


def _gather_rows(h, idx, cap):
    b, s, d = h.shape
    e = idx.shape[1]
    return pl.pallas_call(
        functools.partial(_gather_kernel, cap=cap),
        grid=(b, e),
        in_specs=[pl.BlockSpec((1, 1, 1, cap), lambda i, j: (i, j, 0, 0), memory_space=pltpu.SMEM),
                  pl.BlockSpec((1, s, d), lambda i, j: (i, 0, 0), pipeline_mode=pl.Buffered(1))],
        out_specs=pl.BlockSpec((1, 1, cap, d), lambda i, j: (i, j, 0, 0)),
        out_shape=jax.ShapeDtypeStruct((b, e, cap, d), BF16),
        scratch_shapes=[pltpu.VMEM((cap, d), F32)],
        compiler_params=_params(("parallel", "parallel"), 40),
        name="moe_gather",
    )(idx.reshape(b, e, 1, cap), h)


def _expert_kernel(x_ref, pos_ref, gsel_ref, wg_ref, wu_ref, wd_ref, y_ref, *, cap):
    s = pos_ref.shape[3]
    slot = lax.broadcasted_iota(I32, (cap, MOE_GATHER_TOKENS), 0)
    gate = None
    for k in range(s // MOE_GATHER_TOKENS):
        tok = slice(k * MOE_GATHER_TOKENS, (k + 1) * MOE_GATHER_TOKENS)
        hit = pos_ref[0, 0, :, tok] == slot
        gpart = jnp.sum(jnp.where(hit, gsel_ref[0, 0, :, tok], 0.0), axis=-1, keepdims=True)
        gate = gpart if gate is None else gate + gpart
    x = x_ref[0, 0]
    g = jnp.dot(x, wg_ref[0, 0].astype(BF16), preferred_element_type=F32)
    u = jnp.dot(x, wu_ref[0, 0].astype(BF16), preferred_element_type=F32)
    a = (g * jax.nn.sigmoid(g) * u).astype(BF16)
    y = jnp.dot(a, wd_ref[0, 0].astype(BF16), preferred_element_type=F32)
    y_ref[0, 0] = (y * gate).astype(BF16)


def _experts(xin, pos4, gsel4, w_gate, w_up, w_down, layer, cap):
    b, e, _, d = xin.shape
    s = pos4.shape[3]
    f = w_gate.shape[3]
    return pl.pallas_call(
        functools.partial(_expert_kernel, cap=cap),
        grid=(b, e),
        in_specs=[pl.BlockSpec((1, 1, cap, d), lambda i, j: (i, j, 0, 0)),
                  pl.BlockSpec((1, 1, 1, s), lambda i, j: (i, j, 0, 0)),
                  pl.BlockSpec((1, 1, 1, s), lambda i, j: (i, j, 0, 0)),
                  pl.BlockSpec((1, 1, d, f), lambda i, j: (layer, j, 0, 0)),
                  pl.BlockSpec((1, 1, d, f), lambda i, j: (layer, j, 0, 0)),
                  pl.BlockSpec((1, 1, f, d), lambda i, j: (layer, j, 0, 0))],
        out_specs=pl.BlockSpec((1, 1, cap, d), lambda i, j: (i, j, 0, 0)),
        out_shape=jax.ShapeDtypeStruct((b, e, cap, d), BF16),
        compiler_params=_params(("parallel", "parallel"), 56),
        name="moe_experts",
    )(xin, pos4, gsel4, w_gate, w_up, w_down)


def _scatter_kernel(starts_ref, x_ref, y_ref, post_ref, o_ref, *, cap):
    bi = pl.program_id(0)
    ti = pl.program_id(1)
    win = MOE_SCATTER_TOKENS
    lane = lax.broadcasted_iota(I32, (1, LANES), 1)
    wslot = lax.broadcasted_iota(I32, (1, win), 1).astype(F32)
    post = post_ref[0]

    def slot_column(e):
        return jnp.sum(jnp.where(lane == e, post, 0.0), axis=-1, keepdims=True)

    def window_start(e):
        first = starts_ref[bi, e, ti]
        off = jnp.minimum(lax.shift_left(lax.shift_right_logical(first, 4), 4), cap - win)
        return pl.multiple_of(off, 16)

    acc = x_ref[0]
    for e in range(N_EXPERTS):
        off = window_start(e)
        onehot = jnp.where(slot_column(e) - off.astype(F32) == wslot, 1.0, 0.0).astype(BF16)
        acc = acc + jnp.dot(onehot, y_ref[0, e, pl.ds(off, win), :], preferred_element_type=F32)
    o_ref[0] = acc

    tail = cap - win
    for e in range(N_EXPERTS):
        covered = window_start(e) + win

        @pl.when(starts_ref[bi, e, ti + 1] > covered)
        def _(e=e, covered=covered):
            pcol = slot_column(e)
            hit = (pcol - float(tail) == wslot) & (pcol >= covered.astype(F32))
            o_ref[0] += jnp.dot(jnp.where(hit, 1.0, 0.0).astype(BF16),
                                y_ref[0, e, tail:cap, :], preferred_element_type=F32)


def _scatter(x, y, post, starts, cap):
    b, s, d = x.shape
    e = y.shape[1]
    tn = MOE_SCATTER_TOKENS
    assert cap % 16 == 0 and tn <= cap <= 2 * tn
    grid_spec = pltpu.PrefetchScalarGridSpec(
        num_scalar_prefetch=1,
        grid=(b, s // tn),
        in_specs=[pl.BlockSpec((1, tn, d), lambda i, j, st: (i, j, 0)),
                  pl.BlockSpec((1, e, cap, d), lambda i, j, st: (i, 0, 0, 0),
                               pipeline_mode=pl.Buffered(1)),
                  pl.BlockSpec((1, tn, LANES), lambda i, j, st: (i, j, 0))],
        out_specs=pl.BlockSpec((1, tn, d), lambda i, j, st: (i, j, 0)),
    )
    return pl.pallas_call(
        functools.partial(_scatter_kernel, cap=cap),
        grid_spec=grid_spec,
        out_shape=jax.ShapeDtypeStruct((b, s, d), F32),
        compiler_params=_params(("parallel", "parallel"), 48),
        name="moe_scatter",
    )(starts, x, y, post)


def _moe(x, ffn_g, w_router, b_router, w_gate, w_up, w_down, layer):
    b, s, d = x.shape
    cap = CAPACITY_FACTOR * s // N_EXPERTS
    h, aff = _router(x, ffn_g, w_router, b_router)
    pos, gsel, post, starts, idx = _select(aff, cap)
    xin = _gather_rows(h, idx, cap)
    y = _experts(xin, pos.reshape(b, N_EXPERTS, 1, s), gsel.reshape(b, N_EXPERTS, 1, s),
                 w_gate, w_up, w_down, layer, cap)
    return _scatter(x, y, post, starts, cap)


SSM_CHUNK = 16


SSM_SLAB_GROUPS = LANES // SSM_GROUP
SSM_CHUNK_LANES = SSM_CHUNK * SSM_GROUP
SSM_HALF_STEPS = LANES // SSM_GROUP


def _transpose_lane_groups(xs):
    lgrp = lax.broadcasted_iota(I32, (1, LANES), 1) // SSM_GROUP
    xs = list(xs)
    assert len(xs) == SSM_SLAB_GROUPS
    h = SSM_SLAB_GROUPS // 2
    while h:
        upper = (lgrp & h) != 0
        for a0 in range(SSM_SLAB_GROUPS):
            if a0 & h:
                continue
            x0, x1 = xs[a0], xs[a0 + h]
            xs[a0] = jnp.where(upper, pltpu.roll(x1, h * SSM_GROUP, axis=1), x0)
            xs[a0 + h] = jnp.where(upper, x1, pltpu.roll(x0, LANES - h * SSM_GROUP, axis=1))
        h //= 2
    return xs


def _norm_flatten_kernel(x_ref, g_ref, ug_ref, u_scr):
    tm, d = x_ref.shape
    nch = tm // SSM_CHUNK
    u = _rmsnorm(x_ref[...], g_ref[...])
    lgrp = lax.broadcasted_iota(I32, (1, LANES), 1) // SSM_GROUP
    for slab in range(d // LANES):
        u_scr[slab] = u[:, slab * LANES:(slab + 1) * LANES]
        steps = [u_scr[slab, pl.ds(t, nch, stride=SSM_CHUNK), :] for t in range(SSM_CHUNK)]
        for gi in range(SSM_SLAB_GROUPS):
            for half in range(SSM_CHUNK_LANES // LANES):
                acc = None
                for tt in range(SSM_HALF_STEPS):
                    shift = ((tt - gi) * SSM_GROUP) % LANES
                    piece = steps[half * SSM_HALF_STEPS + tt]
                    if shift:
                        piece = pltpu.roll(piece, shift, axis=1)
                    acc = piece if acc is None else jnp.where(lgrp == tt, piece, acc)
                ug_ref[slab * SSM_SLAB_GROUPS + gi, :, half * LANES:(half + 1) * LANES] = acc.astype(BF16)


def _norm_flatten(x2d, g, tm=512):
    m, d = x2d.shape
    ngrp = d // SSM_GROUP
    nch = tm // SSM_CHUNK
    return pl.pallas_call(
        _norm_flatten_kernel,
        grid=(m // tm,),
        in_specs=[pl.BlockSpec((tm, d), lambda i: (i, 0)),
                  pl.BlockSpec((1, d), lambda i: (0, 0))],
        out_specs=pl.BlockSpec((ngrp, nch, SSM_CHUNK_LANES), lambda i: (0, i, 0)),
        out_shape=jax.ShapeDtypeStruct((ngrp, m // SSM_CHUNK, SSM_CHUNK_LANES), BF16),
        scratch_shapes=[pltpu.VMEM((d // LANES, tm, LANES), F32)],
        compiler_params=_params(("parallel",), 32),
        name="s5_norm_flatten",
    )(x2d, g.reshape(1, d))


def _s5_core_kernel(u_ref, m_ref, w_ref, v_ref, a_ref, y_ref, *, nc):
    u = u_ref[0]
    n_rows = u.shape[0]
    y = jnp.dot(u, m_ref[0], preferred_element_type=F32)
    x = jnp.dot(u, w_ref[0], preferred_element_type=F32)
    hr = x[:, :LANES]
    hi = x[:, LANES:]
    ar = a_ref[0, 0:1, :]
    ai = a_ref[0, 1:2, :]
    lane = lax.broadcasted_iota(I32, (1, LANES), 1)
    fwd = lane < SSM_STATE
    rown = lax.broadcasted_iota(I32, (n_rows, 1), 0) % nc

    def shifted(val, step):
        down = jnp.where(rown >= step, pltpu.roll(val, step, axis=0), 0.0)
        up = jnp.where(rown < nc - step, pltpu.roll(val, n_rows - step, axis=0), 0.0)
        return jnp.where(fwd, down, up)

    step = 1
    while step < nc:
        pr = shifted(hr, step)
        pi = shifted(hi, step)
        hr, hi = hr + ar * pr - ai * pi, hi + ar * pi + ai * pr
        ar, ai = ar * ar - ai * ai, 2.0 * ar * ai
        step *= 2
    hin_r = shifted(hr, 1).astype(BF16)
    hin_i = shifted(hi, 1).astype(BF16)
    y = y + jnp.dot(hin_r, v_ref[0, :LANES, :], preferred_element_type=F32)
    y = y + jnp.dot(hin_i, v_ref[0, LANES:, :], preferred_element_type=F32)
    y_ref[0] = y


def _s5_core(ug, mmat, wmat, vmat, amat, nc):
    g, n, k = ug.shape
    blk = lambda r, c: pl.BlockSpec((1, r, c), lambda i: (i, 0, 0))
    return pl.pallas_call(
        functools.partial(_s5_core_kernel, nc=nc),
        grid=(g,),
        in_specs=[blk(n, k), blk(k, k), blk(k, 2 * LANES), blk(2 * LANES, k), blk(8, LANES)],
        out_specs=blk(n, k),
        out_shape=jax.ShapeDtypeStruct((g, n, k), F32),
        compiler_params=_params(("parallel",), 48),
        name="s5_core",
    )(ug, mmat, wmat, vmat, amat)


def _s5_post_kernel(x_ref, g_ref, yg_first_ref, yg_next_ref, d_ref, w_ref, o_ref, y_scr):
    step = pl.program_id(0)
    x = x_ref[...]
    tm, d = x.shape
    nch = tm // SSM_CHUNK

    def unflatten(yg_ref, slot):
        for slab in range(d // LANES):
            for half in range(SSM_CHUNK_LANES // LANES):
                grps = [yg_ref[slab * SSM_SLAB_GROUPS + gi, :, half * LANES:(half + 1) * LANES]
                        for gi in range(SSM_SLAB_GROUPS)]
                for tt, rows in enumerate(_transpose_lane_groups(grps)):
                    y_scr[slot, slab, pl.ds(half * SSM_HALF_STEPS + tt, nch, stride=SSM_CHUNK), :] = rows

    @pl.when(step == 0)
    def _():
        unflatten(yg_first_ref, 0)

    cur = step % 2
    y = jnp.concatenate([y_scr[cur, slab] for slab in range(d // LANES)], axis=1)
    unflatten(yg_next_ref, 1 - cur)
    hf = _rmsnorm(x, g_ref[...])
    z = jax.nn.gelu(y + d_ref[...] * hf)
    vg = jnp.dot(z.astype(BF16), w_ref[...], preferred_element_type=F32)
    o_ref[...] = x + vg[:, :d] * jax.nn.sigmoid(vg[:, d:])


def _s5_post(x2d, g, yg, d_skip, w_glu_bf16, tm=512):
    m, d = x2d.shape
    ngrp = d // SSM_GROUP
    nch = tm // SSM_CHUNK
    last = m // tm - 1
    row = lambda i: (i, 0)
    fixed = lambda i: (0, 0)
    return pl.pallas_call(
        _s5_post_kernel,
        grid=(m // tm,),
        in_specs=[pl.BlockSpec((tm, d), row), pl.BlockSpec((1, d), fixed),
                  pl.BlockSpec((ngrp, nch, SSM_CHUNK_LANES), lambda i: (0, 0, 0)),
                  pl.BlockSpec((ngrp, nch, SSM_CHUNK_LANES), lambda i: (0, jnp.minimum(i + 1, last), 0)),
                  pl.BlockSpec((1, d), fixed),
                  pl.BlockSpec((d, 2 * d), fixed)],
        out_specs=pl.BlockSpec((tm, d), row),
        out_shape=jax.ShapeDtypeStruct((m, d), F32),
        scratch_shapes=[pltpu.VMEM((2, d // LANES, tm, LANES), F32)],
        compiler_params=_params(("arbitrary",), 48),
        name="s5_gelu_glu",
    )(x2d, g.reshape(1, d), yg, yg, d_skip.reshape(1, d), w_glu_bf16)


def _s5_matrices(lam_re, lam_im, log_dt, b_re, b_im, c_re, c_im):
    t = SSM_CHUNK
    g, p = lam_re.shape[1:]
    c = SSM_GROUP
    dt = jnp.exp(log_dt)[..., None]
    den = lam_re * lam_re + lam_im * lam_im
    mag = jnp.exp(lam_re * dt)
    ang = lam_im * dt
    lb_re = mag * jnp.cos(ang)
    lb_im = mag * jnp.sin(ang)
    nr = lb_re - 1.0
    f_re = (nr * lam_re + lb_im * lam_im) / den
    f_im = (lb_im * lam_re - nr * lam_im) / den
    bb_re = f_re[..., None] * b_re - f_im[..., None] * b_im
    bb_im = f_re[..., None] * b_im + f_im[..., None] * b_re
    lag = jnp.arange(t + 1, dtype=F32)[:, None, None, None]
    pmag = jnp.exp(lag * (lam_re * dt))
    pw_re = pmag * jnp.cos(lag * ang)
    pw_im = pmag * jnp.sin(lag * ang)
    cl_re = c_re[None] * pw_re[..., None, :] - c_im[None] * pw_im[..., None, :]
    cl_im = c_re[None] * pw_im[..., None, :] + c_im[None] * pw_re[..., None, :]
    kern = (jnp.einsum('ldgcp,dgpe->ldgce', cl_re[:t], bb_re)
            - jnp.einsum('ldgcp,dgpe->ldgce', cl_im[:t], bb_im))
    tt = jnp.arange(t)
    by_lag = jnp.concatenate([kern[:0:-1, 1], (kern[0, 0] + kern[0, 1])[None], kern[1:, 0]], axis=0)
    toeplitz = jnp.stack([by_lag[t - 1 - s:2 * t - 1 - s] for s in range(t)], axis=0)
    mmat = jnp.transpose(toeplitz, (2, 0, 4, 1, 3)).reshape(g, t * c, t * c)
    wf_re = pw_re[t - 1 - tt, 0][..., None] * bb_re[0][None]
    wf_im = pw_im[t - 1 - tt, 0][..., None] * bb_re[0][None]
    wf_re = wf_re - pw_im[t - 1 - tt, 0][..., None] * bb_im[0][None]
    wf_im = wf_im + pw_re[t - 1 - tt, 0][..., None] * bb_im[0][None]
    wb_re = pw_re[tt, 1][..., None] * bb_re[1][None] - pw_im[tt, 1][..., None] * bb_im[1][None]
    wb_im = pw_im[tt, 1][..., None] * bb_re[1][None] + pw_re[tt, 1][..., None] * bb_im[1][None]
    wmat = jnp.concatenate([wf_re, wb_re, wf_im, wb_im], axis=2)
    wmat = jnp.transpose(wmat, (1, 0, 3, 2)).reshape(g, t * c, 4 * p)
    vf_re = cl_re[tt + 1, 0]
    vf_im = -cl_im[tt + 1, 0]
    vb_re = cl_re[t - tt, 1]
    vb_im = -cl_im[t - tt, 1]
    vmat = jnp.concatenate([vf_re, vb_re, vf_im, vb_im], axis=3)
    vmat = jnp.transpose(vmat, (1, 3, 0, 2)).reshape(g, 4 * p, t * c)
    a_re = jnp.concatenate([pw_re[t, 0], pw_re[t, 1]], axis=-1)
    a_im = jnp.concatenate([pw_im[t, 0], pw_im[t, 1]], axis=-1)
    amat = jnp.zeros((g, 8, 2 * p), F32).at[:, 0].set(a_re).at[:, 1].set(a_im)
    return mmat.astype(BF16), wmat.astype(BF16), vmat.astype(BF16), amat


def _s5_mixer(x, mix_g, lam_re, lam_im, log_dt, b_re, b_im, c_re, c_im, d_skip, w_glu):
    b, s, d = x.shape
    t = SSM_CHUNK
    g = d // SSM_GROUP
    nc = s // t
    x2d = x.reshape(b * s, d)
    ug = _norm_flatten(x2d, mix_g)
    mmat, wmat, vmat, amat = _s5_matrices(lam_re, lam_im, log_dt, b_re, b_im, c_re, c_im)
    yg = _s5_core(ug, mmat, wmat, vmat, amat, nc)
    out = _s5_post(x2d, mix_g, yg, d_skip, w_glu.astype(BF16))
    return out.reshape(b, s, d)


def kernel(x, mix_norm_even, w_in, conv_w, conv_b, conv_ln_g, conv_ln_b, q_norm, k_norm, w_out, mix_norm_odd, ssm_lam_re, ssm_lam_im, ssm_log_dt, ssm_b_re, ssm_b_im, ssm_c_re, ssm_c_im, ssm_d, w_glu, ffn_norm, w_router, b_router, w_e_gate, w_e_up, w_e_down):
    depth = ffn_norm.shape[0]
    for layer in range(depth):
        i = layer // 2
        if layer % 2 == 0:
            x = _even_mixer(x, mix_norm_even[i], w_in[i], conv_w[i], conv_b[i], conv_ln_g[i],
                            conv_ln_b[i], q_norm[i], k_norm[i], w_out[i])
        else:
            x = _s5_mixer(x, mix_norm_odd[i], ssm_lam_re[i], ssm_lam_im[i], ssm_log_dt[i],
                          ssm_b_re[i], ssm_b_im[i], ssm_c_re[i], ssm_c_im[i], ssm_d[i], w_glu[i])
        x = _moe(x, ffn_norm[layer], w_router[layer], b_router[layer],
                 w_e_gate, w_e_up, w_e_down, layer)
    return x
```

```python
import functools
import math

import jax
import jax.numpy as jnp
from jax import lax
from jax.experimental import pallas as pl
from jax.experimental.pallas import tpu as pltpu

F32 = jnp.float32
BF16 = jnp.bfloat16
I32 = jnp.int32

EPS = 1e-6
NEG_INF = -1e30
LOG2_E = math.log2(math.e)

CONV_CH = 512
CONV_WIDTH = 31
ATT_HEADS = 8
HEAD_DIM = 64
ATT_WIDTH = ATT_HEADS * HEAD_DIM
DILATED = ((128, 1), (512, 4), (2048, 16))
SSM_GROUP = 16
SSM_STATE = 64
N_EXPERTS = 16
CAPACITY_FACTOR = 2
MOE_GATHER_TOKENS = 1024
MOE_GATHER_GROUP = 16
MOE_SCATTER_TOKENS = 256

LANES = 128
F32_MIN_EXP = -126
F32_MANTISSA_BITS = 23
MIB = 1024 * 1024


def _params(sem, vmem_mib):
    return pltpu.CompilerParams(dimension_semantics=sem, vmem_limit_bytes=vmem_mib * MIB)


def _rmsnorm(x, g):
    return x * lax.rsqrt(jnp.mean(x * x, axis=-1, keepdims=True) + EPS) * g


def _norm_matmul_kernel(x_ref, g_ref, w_ref, o_ref):
    y = _rmsnorm(x_ref[...], g_ref[...])
    o_ref[...] = jnp.dot(y.astype(BF16), w_ref[...], preferred_element_type=F32)


def _norm_matmul(x2d, g, w_bf16, tm=512):
    m, d = x2d.shape
    n = w_bf16.shape[1]
    return pl.pallas_call(
        _norm_matmul_kernel,
        grid=(m // tm,),
        in_specs=[pl.BlockSpec((tm, d), lambda i: (i, 0)),
                  pl.BlockSpec((1, d), lambda i: (0, 0)),
                  pl.BlockSpec((d, n), lambda i: (0, 0))],
        out_specs=pl.BlockSpec((tm, n), lambda i: (i, 0)),
        out_shape=jax.ShapeDtypeStruct((m, n), F32),
        compiler_params=_params(("parallel",), 48),
        name="norm_in_proj",
    )(x2d, g.reshape(1, d), w_bf16)


CONV_HALO = 16
CONV_ROWS = 256


def _conv_kernel(val_ref, gate_ref, w_ref, b_ref, o_ref, hp_ref):
    s = val_ref.shape[1]
    zeros = jnp.zeros((CONV_HALO, LANES), F32)
    hp_ref[0:CONV_HALO, :] = zeros
    hp_ref[CONV_HALO + s:2 * CONV_HALO + s, :] = zeros
    hp_ref[CONV_HALO:CONV_HALO + s, :] = val_ref[0] * jax.nn.sigmoid(gate_ref[0])
    first = CONV_HALO - CONV_WIDTH // 2
    for c in range(s // CONV_ROWS):
        r0 = c * CONV_ROWS
        acc = jnp.broadcast_to(b_ref[...], (CONV_ROWS, LANES))
        for k in range(CONV_WIDTH):
            acc = acc + w_ref[k:k + 1, :] * hp_ref[r0 + first + k:r0 + first + k + CONV_ROWS, :]
        o_ref[0, r0:r0 + CONV_ROWS, :] = acc


def _conformer_conv(proj3, conv_w, conv_b):
    b, s, _ = proj3.shape
    nct = CONV_CH // LANES
    return pl.pallas_call(
        _conv_kernel,
        grid=(b, nct),
        in_specs=[pl.BlockSpec((1, s, LANES), lambda i, j: (i, 0, j)),
                  pl.BlockSpec((1, s, LANES), lambda i, j: (i, 0, nct + j)),
                  pl.BlockSpec((CONV_WIDTH, LANES), lambda i, j: (0, j)),
                  pl.BlockSpec((1, LANES), lambda i, j: (0, j))],
        out_specs=pl.BlockSpec((1, s, LANES), lambda i, j: (i, 0, j)),
        out_shape=jax.ShapeDtypeStruct((b, s, CONV_CH), F32),
        scratch_shapes=[pltpu.VMEM((s + 2 * CONV_HALO, LANES), F32)],
        compiler_params=_params(("parallel", "parallel"), 32),
        name="glu_dwconv",
    )(proj3, proj3, conv_w, conv_b.reshape(1, CONV_CH))


ATT_QB = 128
ATT_RAD = 64
ATT_KW = ATT_QB + 2 * ATT_RAD
ATT_PREP_ROWS = 512
ATT_UNROLL = 8


def _attn_kernel(slopes_ref, q_ref, k_ref, v_ref, qg_ref, kg_ref, o_ref,
                 qs_ref, kp_ref, vp_ref, ob_ref, mb_ref, *, seq, pad):
    pair = pl.program_id(1)
    lane = lax.broadcasted_iota(I32, (1, LANES), 1)
    lo = lane < HEAD_DIM
    scale = LOG2_E / math.sqrt(HEAD_DIM)
    spare = (HEAD_DIM, 0)

    zpad = jnp.zeros((pad, LANES), F32)
    for hh in range(2):
        kpad = jnp.broadcast_to(jnp.where(lane == spare[hh], NEG_INF, 0.0), (pad, LANES))
        kp_ref[hh, 0:pad, :] = kpad
        kp_ref[hh, pad + seq:2 * pad + seq, :] = kpad
        vp_ref[hh, 0:pad, :] = zpad
        vp_ref[hh, pad + seq:2 * pad + seq, :] = zpad

    def head_rms(x, g):
        x2 = x * x
        s_lo = jnp.sum(jnp.where(lo, x2, 0.0), axis=-1, keepdims=True)
        s_hi = jnp.sum(jnp.where(lo, 0.0, x2), axis=-1, keepdims=True)
        ms = jnp.where(lo, s_lo, s_hi) * (1.0 / HEAD_DIM)
        return x * lax.rsqrt(ms + EPS) * g

    def prep(i, carry):
        r = pl.multiple_of(i * ATT_PREP_ROWS, ATT_PREP_ROWS)
        rows = pl.ds(r, ATT_PREP_ROWS)
        prows = pl.ds(pad + r, ATT_PREP_ROWS)
        qn = head_rms(q_ref[0, rows, :], qg_ref[...]) * scale
        qs_ref[0, rows, :] = jnp.where(lo, qn, jnp.where(lane == spare[0], 1.0, 0.0))
        qs_ref[1, rows, :] = jnp.where(lo, jnp.where(lane == spare[1], 1.0, 0.0), qn)
        kn = head_rms(k_ref[0, rows, :], kg_ref[...])
        kp_ref[0, prows, :] = jnp.where(lo, kn, 0.0)
        kp_ref[1, prows, :] = jnp.where(lo, 0.0, kn)
        v = v_ref[0, rows, :]
        vp_ref[0, prows, :] = jnp.where(lo, v, jnp.where(lane == spare[0], 1.0, 0.0))
        vp_ref[1, prows, :] = jnp.where(lo, jnp.where(lane == spare[1], 1.0, 0.0), v)
        return carry

    lax.fori_loop(0, seq // ATT_PREP_ROWS, prep, 0)

    rowi = lax.broadcasted_iota(I32, (ATT_QB, ATT_KW), 0)
    coli = lax.broadcasted_iota(I32, (ATT_QB, ATT_KW), 1)
    arel = jnp.abs(coli - ATT_RAD - rowi)

    for hh in range(2):
        slope = slopes_ref[pair * 2 + hh] * LOG2_E
        for bi, (window, dil) in enumerate(DILATED):
            assert window // (2 * dil) == ATT_RAD
            nbc = seq // dil // ATT_QB
            bias = jnp.where(arel <= ATT_RAD, -(slope * dil) * arel.astype(F32), NEG_INF)

            def blk(j, carry, bi=bi, dil=dil, nbc=nbc, bias=bias, hh=hh):
                c = j // nbc
                n = j - c * nbc
                qstart = c + dil * ATT_QB * n
                kstart = pad + c + dil * (ATT_QB * n - ATT_RAD)
                if dil == 1:
                    qsl = pl.ds(qstart, ATT_QB)
                    ksl = pl.ds(kstart, ATT_KW)
                else:
                    qsl = pl.ds(qstart, ATT_QB, stride=dil)
                    ksl = pl.ds(kstart, ATT_KW, stride=dil)
                qb = qs_ref[hh, qsl, :].astype(BF16)
                kw = kp_ref[hh, ksl, :].astype(BF16)
                vw = vp_ref[hh, ksl, :].astype(BF16)
                sc = lax.dot_general(qb, kw, (((1,), (1,)), ((), ())),
                                     preferred_element_type=F32) + bias
                m = jnp.max(sc, axis=-1, keepdims=True)
                p = jnp.exp2(sc - m)
                ob_ref[bi, qsl, :] = jnp.dot(p.astype(BF16), vw, preferred_element_type=F32)
                mb_ref[bi, qsl, :] = jnp.broadcast_to(m, (ATT_QB, LANES))
                return carry

            lax.fori_loop(0, seq // ATT_QB, blk, 0, unroll=ATT_UNROLL)

        def combine(i, carry, hh=hh):
            r = pl.multiple_of(i * ATT_PREP_ROWS, ATT_PREP_ROWS)
            rows = pl.ds(r, ATT_PREP_ROWS)
            nbr = len(DILATED)
            ms = [mb_ref[bi, rows, :] for bi in range(nbr)]
            mx = functools.reduce(jnp.maximum, ms)
            acc = sum(jnp.exp2(ms[bi] - mx) * ob_ref[bi, rows, :] for bi in range(nbr))
            den = jnp.sum(jnp.where(lane == spare[hh], acc, 0.0), axis=-1, keepdims=True)
            res = acc / den
            if hh == 0:
                o_ref[0, rows, :] = res
            else:
                o_ref[0, rows, :] = jnp.where(lo, o_ref[0, rows, :], res)
            return carry

        lax.fori_loop(0, seq // ATT_PREP_ROWS, combine, 0)


def _dilated_attention(proj3, q_norm, k_norm):
    b, s, _ = proj3.shape
    pad = ATT_RAD * max(d for _, d in DILATED)
    npair = ATT_HEADS // 2
    qoff = 2 * CONV_CH // LANES
    slopes = jnp.asarray([2.0 ** (-8.0 * (h + 1) / ATT_HEADS) for h in range(ATT_HEADS)], F32)
    qg = jnp.tile(q_norm.reshape(1, HEAD_DIM), (1, 2))
    kg = jnp.tile(k_norm.reshape(1, HEAD_DIM), (1, 2))
    kernel = functools.partial(_attn_kernel, seq=s, pad=pad)
    return pl.pallas_call(
        kernel,
        grid=(b, npair),
        in_specs=[pl.BlockSpec(memory_space=pltpu.SMEM),
                  pl.BlockSpec((1, s, LANES), lambda i, j: (i, 0, qoff + j)),
                  pl.BlockSpec((1, s, LANES), lambda i, j: (i, 0, qoff + npair + j)),
                  pl.BlockSpec((1, s, LANES), lambda i, j: (i, 0, qoff + 2 * npair + j)),
                  pl.BlockSpec((1, LANES), lambda i, j: (0, 0)),
                  pl.BlockSpec((1, LANES), lambda i, j: (0, 0))],
        out_specs=pl.BlockSpec((1, s, LANES), lambda i, j: (i, 0, j)),
        out_shape=jax.ShapeDtypeStruct((b, s, ATT_WIDTH), F32),
        scratch_shapes=[pltpu.VMEM((2, s, LANES), F32),
                        pltpu.VMEM((2, s + 2 * pad, LANES), F32),
                        pltpu.VMEM((2, s + 2 * pad, LANES), F32),
                        pltpu.VMEM((len(DILATED), s, LANES), F32),
                        pltpu.VMEM((len(DILATED), s, LANES), F32)],
        compiler_params=_params(("parallel", "parallel"), 56),
        name="dilated_attn",
    )(slopes, proj3, proj3, proj3, qg, kg)


def _outproj_kernel(x_ref, conv_ref, att_ref, g_ref, b_ref, w1_ref, w2_ref, o_ref):
    c = conv_ref[...]
    mu = jnp.mean(c, axis=-1, keepdims=True)
    cc = c - mu
    var = jnp.mean(cc * cc, axis=-1, keepdims=True)
    hn = cc * lax.rsqrt(var + EPS) * g_ref[...] + b_ref[...]
    a = hn * jax.nn.sigmoid(hn)
    y = jnp.dot(a.astype(BF16), w1_ref[...], preferred_element_type=F32)
    y = y + jnp.dot(att_ref[...].astype(BF16), w2_ref[...], preferred_element_type=F32)
    o_ref[...] = x_ref[...] + y


def _out_projection(x2d, conv2d, att2d, ln_g, ln_b, w_out_bf16, tm=512):
    m, d = x2d.shape
    row = lambda i: (i, 0)
    fixed = lambda i: (0, 0)
    return pl.pallas_call(
        _outproj_kernel,
        grid=(m // tm,),
        in_specs=[pl.BlockSpec((tm, d), row),
                  pl.BlockSpec((tm, CONV_CH), row),
                  pl.BlockSpec((tm, ATT_WIDTH), row),
                  pl.BlockSpec((1, CONV_CH), fixed),
                  pl.BlockSpec((1, CONV_CH), fixed),
                  pl.BlockSpec((CONV_CH, d), fixed),
                  pl.BlockSpec((ATT_WIDTH, d), lambda i: (1, 0))],
        out_specs=pl.BlockSpec((tm, d), row),
        out_shape=jax.ShapeDtypeStruct((m, d), F32),
        compiler_params=_params(("parallel",), 32),
        name="ln_out_proj",
    )(x2d, conv2d, att2d, ln_g.reshape(1, CONV_CH), ln_b.reshape(1, CONV_CH), w_out_bf16, w_out_bf16)


def _even_mixer(x, mix_g, w_in, conv_w, conv_b, ln_g, ln_b, q_norm, k_norm, w_out):
    b, s, d = x.shape
    x2d = x.reshape(b * s, d)
    proj = _norm_matmul(x2d, mix_g, w_in.astype(BF16))
    proj3 = proj.reshape(b, s, -1)
    conv = _conformer_conv(proj3, conv_w, conv_b)
    att = _dilated_attention(proj3, q_norm, k_norm)
    out = _out_projection(x2d, conv.reshape(b * s, CONV_CH), att.reshape(b * s, ATT_WIDTH),
                          ln_g, ln_b, w_out.astype(BF16))
    return out.reshape(b, s, d)


def _router_kernel(x_ref, g_ref, wr_ref, br_ref, h_ref, aff_ref):
    h = _rmsnorm(x_ref[0], g_ref[...])
    logits = jnp.dot(h, wr_ref[...], precision=lax.Precision.HIGHEST,
                     preferred_element_type=F32) + br_ref[...]
    lt = logits.T[:N_EXPERTS, :]
    mx = jnp.max(lt, axis=0, keepdims=True)
    ex = jnp.exp(lt - mx)
    aff_ref[0] = ex / jnp.sum(ex, axis=0, keepdims=True)
    h_ref[0] = h


def _router(x, g, w_router, b_router, tm=512):
    b, s, d = x.shape
    wr = jnp.zeros((d, LANES), F32).at[:, :N_EXPERTS].set(w_router)
    br = jnp.zeros((1, LANES), F32).at[0, :N_EXPERTS].set(b_router)
    return pl.pallas_call(
        _router_kernel,
        grid=(b, s // tm),
        in_specs=[pl.BlockSpec((1, tm, d), lambda i, j: (i, j, 0)),
                  pl.BlockSpec((1, d), lambda i, j: (0, 0)),
                  pl.BlockSpec((d, LANES), lambda i, j: (0, 0)),
                  pl.BlockSpec((1, LANES), lambda i, j: (0, 0))],
        out_specs=[pl.BlockSpec((1, tm, d), lambda i, j: (i, j, 0)),
                   pl.BlockSpec((1, N_EXPERTS, tm), lambda i, j: (i, 0, j))],
        out_shape=[jax.ShapeDtypeStruct((b, s, d), F32),
                   jax.ShapeDtypeStruct((b, N_EXPERTS, s), F32)],
        compiler_params=_params(("parallel", "parallel"), 32),
        name="moe_router",
    )(x, g.reshape(1, d), wr, br)


def _select_kernel(aff_ref, pos_ref, gsel_ref, post_ref, starts_ref, idx_ref, cs_ref, *, cap):
    a = aff_ref[0]
    s = a.shape[1]
    capf = float(cap)

    def count(mask):
        return jnp.sum(jnp.where(mask, 1.0, 0.0), axis=-1, keepdims=True)

    def enough(cand):
        return count(a >= cand) >= capf

    tiny = float(2.0 ** F32_MIN_EXP)
    thr = jnp.full((N_EXPERTS, 1), tiny, F32)
    normal = enough(thr)
    for j in range(6, -1, -1):
        cand = thr * float(2.0 ** (2 ** j))
        thr = jnp.where(enough(cand), cand, thr)
    step = thr * 0.5
    for _ in range(F32_MANTISSA_BITS):
        cand = thr + step
        thr = jnp.where(enough(cand), cand, thr)
        step = step * 0.5
    thr = jnp.where(normal, thr, 0.0)

    ri = lax.broadcasted_iota(I32, (LANES, LANES), 0)
    ci = lax.broadcasted_iota(I32, (LANES, LANES), 1)
    tri = jnp.where(ri < ci, 1.0, 0.0).astype(BF16)

    lane = lax.broadcasted_iota(I32, (1, LANES), 1)
    chunks_per_tile = MOE_SCATTER_TOKENS // LANES

    def exclusive_cumsum(mask):
        ind = jnp.where(mask, 1.0, 0.0)
        off = jnp.zeros((N_EXPERTS, 1), F32)
        starts = jnp.zeros((N_EXPERTS, LANES), F32)
        for j in range(s // LANES):
            if j % chunks_per_tile == 0:
                starts = jnp.where(lane == j // chunks_per_tile, off, starts)
            xc = ind[:, j * LANES:(j + 1) * LANES]
            cs_ref[:, j * LANES:(j + 1) * LANES] = (
                jnp.dot(xc.astype(BF16), tri, preferred_element_type=F32) + off)
            off = off + jnp.sum(xc, axis=-1, keepdims=True)
        starts = jnp.where(lane == s // MOE_SCATTER_TOKENS, off, starts)
        return cs_ref[...], starts

    gt = a > thr
    eq = a == thr
    need = capf - count(gt)
    tie_rank, _ = exclusive_cumsum(eq)
    sel = gt | (eq & (tie_rank < need))
    slot, starts = exclusive_cumsum(sel)
    slot = jnp.where(sel, slot, -1.0)
    pos_ref[0] = slot.astype(I32)
    gsel_ref[0] = jnp.where(sel, a, 0.0)
    starts_ref[0] = starts.astype(I32)
    fill = jnp.zeros((LANES - N_EXPERTS, s), F32)
    post_ref[0] = jnp.concatenate([slot, fill], axis=0).T

    radix = float(s)
    tok = lax.broadcasted_iota(I32, (N_EXPERTS, s), 1).astype(F32)
    packed = jnp.where(sel, (tok - slot) * radix + tok, -1.0)
    for k in range(s.bit_length() - 1):
        dist = jnp.floor(packed * (1.0 / radix))
        bit = jnp.floor(dist * 0.5 ** k) - 2.0 * jnp.floor(dist * 0.5 ** (k + 1))
        moving = jnp.where(packed >= 0.0, bit, 0.0) == 1.0
        arrived = pltpu.roll(jnp.where(moving, packed, -1.0), s - 2 ** k, axis=1)
        packed = jnp.where(arrived >= 0.0, arrived, jnp.where(moving, -1.0, packed))
    head = packed[:, :cap]
    idx_ref[0] = (head - radix * jnp.floor(head * (1.0 / radix))).astype(I32)


def _select(aff, cap):
    b, e, s = aff.shape
    assert s // MOE_SCATTER_TOKENS < LANES
    assert s & (s - 1) == 0 and s * s <= 2 ** 24
    blk = pl.BlockSpec((1, e, s), lambda i: (i, 0, 0))
    return pl.pallas_call(
        functools.partial(_select_kernel, cap=cap),
        grid=(b,),
        in_specs=[blk],
        out_specs=[blk, blk,
                   pl.BlockSpec((1, s, LANES), lambda i: (i, 0, 0)),
                   pl.BlockSpec((1, e, LANES), lambda i: (i, 0, 0)),
                   pl.BlockSpec((1, e, cap), lambda i: (i, 0, 0))],
        out_shape=[jax.ShapeDtypeStruct((b, e, s), I32),
                   jax.ShapeDtypeStruct((b, e, s), F32),
                   jax.ShapeDtypeStruct((b, s, LANES), F32),
                   jax.ShapeDtypeStruct((b, e, LANES), I32),
                   jax.ShapeDtypeStruct((b, e, cap), I32)],
        scratch_shapes=[pltpu.VMEM((e, s), F32)],
        compiler_params=_params(("parallel",), 32),
        name="moe_select",
    )(aff)


def _gather_kernel(idx_ref, h_ref, x_ref, rows_scr, *, cap):
    group = rows_scr.shape[1]

    def copy_group(gidx, carry):
        base = gidx * group
        for r in range(group):
            tok = idx_ref[0, 0, 0, base + r]
            rows_scr[gidx, pl.ds(r, 1), :] = h_ref[0, pl.ds(tok, 1), :]
        return carry

    lax.fori_loop(0, cap // group, copy_group, 0)
    x_ref[0, 0] = rows_scr[...].reshape(cap, rows_scr.shape[2]).astype(BF16)


def _gather_rows(h, idx, cap):
    b, s, d = h.shape
    e = idx.shape[1]
    return pl.pallas_call(
        functools.partial(_gather_kernel, cap=cap),
        grid=(b, e),
        in_specs=[pl.BlockSpec((1, 1, 1, cap), lambda i, j: (i, j, 0, 0), memory_space=pltpu.SMEM),
                  pl.BlockSpec((1, s, d), lambda i, j: (i, 0, 0), pipeline_mode=pl.Buffered(1))],
        out_specs=pl.BlockSpec((1, 1, cap, d), lambda i, j: (i, j, 0, 0)),
        out_shape=jax.ShapeDtypeStruct((b, e, cap, d), BF16),
        scratch_shapes=[pltpu.VMEM((cap // MOE_GATHER_GROUP, MOE_GATHER_GROUP, d), F32)],
        compiler_params=_params(("parallel", "parallel"), 40),
        name="moe_gather",
    )(idx.reshape(b, e, 1, cap), h)


def _expert_kernel(x_ref, pos_ref, gsel_ref, wg_ref, wu_ref, wd_ref, y_ref, *, cap):
    s = pos_ref.shape[3]
    slot = lax.broadcasted_iota(I32, (cap, MOE_GATHER_TOKENS), 0)
    gate = None
    for k in range(s // MOE_GATHER_TOKENS):
        tok = slice(k * MOE_GATHER_TOKENS, (k + 1) * MOE_GATHER_TOKENS)
        hit = pos_ref[0, 0, :, tok] == slot
        gpart = jnp.sum(jnp.where(hit, gsel_ref[0, 0, :, tok], 0.0), axis=-1, keepdims=True)
        gate = gpart if gate is None else gate + gpart
    x = x_ref[0, 0]
    g = jnp.dot(x, wg_ref[0, 0].astype(BF16), preferred_element_type=F32)
    u = jnp.dot(x, wu_ref[0, 0].astype(BF16), preferred_element_type=F32)
    a = (g * jax.nn.sigmoid(g) * u).astype(BF16)
    y = jnp.dot(a, wd_ref[0, 0].astype(BF16), preferred_element_type=F32)
    y_ref[0, 0] = (y * gate).astype(BF16)


def _experts(xin, pos4, gsel4, w_gate, w_up, w_down, layer, cap):
    b, e, _, d = xin.shape
    s = pos4.shape[3]
    f = w_gate.shape[3]
    return pl.pallas_call(
        functools.partial(_expert_kernel, cap=cap),
        grid=(e, b),
        in_specs=[pl.BlockSpec((1, 1, cap, d), lambda j, i: (i, j, 0, 0)),
                  pl.BlockSpec((1, 1, 1, s), lambda j, i: (i, j, 0, 0)),
                  pl.BlockSpec((1, 1, 1, s), lambda j, i: (i, j, 0, 0)),
                  pl.BlockSpec((1, 1, d, f), lambda j, i: (layer, j, 0, 0)),
                  pl.BlockSpec((1, 1, d, f), lambda j, i: (layer, j, 0, 0)),
                  pl.BlockSpec((1, 1, f, d), lambda j, i: (layer, j, 0, 0))],
        out_specs=pl.BlockSpec((1, 1, cap, d), lambda j, i: (i, j, 0, 0)),
        out_shape=jax.ShapeDtypeStruct((b, e, cap, d), BF16),
        compiler_params=_params(("parallel", "parallel"), 56),
        name="moe_experts",
    )(xin, pos4, gsel4, w_gate, w_up, w_down)


def _scatter_kernel(starts_ref, x_ref, y_ref, post_ref, o_ref, *, cap):
    bi = pl.program_id(0)
    ti = pl.program_id(1)
    win = MOE_SCATTER_TOKENS
    lane = lax.broadcasted_iota(I32, (1, LANES), 1)
    wslot = lax.broadcasted_iota(I32, (1, win), 1).astype(F32)
    post = post_ref[0]

    def slot_column(e):
        return jnp.sum(jnp.where(lane == e, post, 0.0), axis=-1, keepdims=True)

    def window_start(e):
        first = starts_ref[bi, e, ti]
        off = jnp.minimum(lax.shift_left(lax.shift_right_logical(first, 4), 4), cap - win)
        return pl.multiple_of(off, 16)

    acc = x_ref[0]
    for e in range(N_EXPERTS):
        off = window_start(e)
        onehot = jnp.where(slot_column(e) - off.astype(F32) == wslot, 1.0, 0.0).astype(BF16)
        acc = acc + jnp.dot(onehot, y_ref[0, e, pl.ds(off, win), :], preferred_element_type=F32)
    o_ref[0] = acc

    tail = cap - win
    for e in range(N_EXPERTS):
        covered = window_start(e) + win

        @pl.when(starts_ref[bi, e, ti + 1] > covered)
        def _(e=e, covered=covered):
            pcol = slot_column(e)
            hit = (pcol - float(tail) == wslot) & (pcol >= covered.astype(F32))
            o_ref[0] += jnp.dot(jnp.where(hit, 1.0, 0.0).astype(BF16),
                                y_ref[0, e, tail:cap, :], preferred_element_type=F32)


def _scatter(x, y, post, starts, cap):
    b, s, d = x.shape
    e = y.shape[1]
    tn = MOE_SCATTER_TOKENS
    assert cap % 16 == 0 and tn <= cap <= 2 * tn
    grid_spec = pltpu.PrefetchScalarGridSpec(
        num_scalar_prefetch=1,
        grid=(b, s // tn),
        in_specs=[pl.BlockSpec((1, tn, d), lambda i, j, st: (i, j, 0)),
                  pl.BlockSpec((1, e, cap, d), lambda i, j, st: (i, 0, 0, 0),
                               pipeline_mode=pl.Buffered(1)),
                  pl.BlockSpec((1, tn, LANES), lambda i, j, st: (i, j, 0))],
        out_specs=pl.BlockSpec((1, tn, d), lambda i, j, st: (i, j, 0)),
    )
    return pl.pallas_call(
        functools.partial(_scatter_kernel, cap=cap),
        grid_spec=grid_spec,
        out_shape=jax.ShapeDtypeStruct((b, s, d), F32),
        compiler_params=_params(("parallel", "parallel"), 48),
        name="moe_scatter",
    )(starts, x, y, post)


def _moe(x, ffn_g, w_router, b_router, w_gate, w_up, w_down, layer):
    b, s, d = x.shape
    cap = CAPACITY_FACTOR * s // N_EXPERTS
    h, aff = _router(x, ffn_g, w_router, b_router)
    pos, gsel, post, starts, idx = _select(aff, cap)
    xin = _gather_rows(h, idx, cap)
    y = _experts(xin, pos.reshape(b, N_EXPERTS, 1, s), gsel.reshape(b, N_EXPERTS, 1, s),
                 w_gate, w_up, w_down, layer, cap)
    return _scatter(x, y, post, starts, cap)


SSM_CHUNK = 16


SSM_SLAB_GROUPS = LANES // SSM_GROUP
SSM_CHUNK_LANES = SSM_CHUNK * SSM_GROUP
SSM_HALF_STEPS = LANES // SSM_GROUP


def _transpose_lane_groups(xs):
    lgrp = lax.broadcasted_iota(I32, (1, LANES), 1) // SSM_GROUP
    xs = list(xs)
    assert len(xs) == SSM_SLAB_GROUPS
    h = SSM_SLAB_GROUPS // 2
    while h:
        upper = (lgrp & h) != 0
        for a0 in range(SSM_SLAB_GROUPS):
            if a0 & h:
                continue
            x0, x1 = xs[a0], xs[a0 + h]
            xs[a0] = jnp.where(upper, pltpu.roll(x1, h * SSM_GROUP, axis=1), x0)
            xs[a0 + h] = jnp.where(upper, x1, pltpu.roll(x0, LANES - h * SSM_GROUP, axis=1))
        h //= 2
    return xs


def _norm_flatten_kernel(x_ref, g_ref, ug_ref, u_scr):
    tm, d = x_ref.shape
    nch = tm // SSM_CHUNK
    u = _rmsnorm(x_ref[...], g_ref[...])
    lgrp = lax.broadcasted_iota(I32, (1, LANES), 1) // SSM_GROUP
    for slab in range(d // LANES):
        u_scr[slab] = u[:, slab * LANES:(slab + 1) * LANES]
        steps = [u_scr[slab, pl.ds(t, nch, stride=SSM_CHUNK), :] for t in range(SSM_CHUNK)]
        for gi in range(SSM_SLAB_GROUPS):
            for half in range(SSM_CHUNK_LANES // LANES):
                acc = None
                for tt in range(SSM_HALF_STEPS):
                    shift = ((tt - gi) * SSM_GROUP) % LANES
                    piece = steps[half * SSM_HALF_STEPS + tt]
                    if shift:
                        piece = pltpu.roll(piece, shift, axis=1)
                    acc = piece if acc is None else jnp.where(lgrp == tt, piece, acc)
                ug_ref[slab * SSM_SLAB_GROUPS + gi, :, half * LANES:(half + 1) * LANES] = acc.astype(BF16)


def _norm_flatten(x2d, g, tm=512):
    m, d = x2d.shape
    ngrp = d // SSM_GROUP
    nch = tm // SSM_CHUNK
    return pl.pallas_call(
        _norm_flatten_kernel,
        grid=(m // tm,),
        in_specs=[pl.BlockSpec((tm, d), lambda i: (i, 0)),
                  pl.BlockSpec((1, d), lambda i: (0, 0))],
        out_specs=pl.BlockSpec((ngrp, nch, SSM_CHUNK_LANES), lambda i: (0, i, 0)),
        out_shape=jax.ShapeDtypeStruct((ngrp, m // SSM_CHUNK, SSM_CHUNK_LANES), BF16),
        scratch_shapes=[pltpu.VMEM((d // LANES, tm, LANES), F32)],
        compiler_params=_params(("parallel",), 32),
        name="s5_norm_flatten",
    )(x2d, g.reshape(1, d))


def _s5_core_kernel(u_ref, m_ref, w_ref, v_ref, a_ref, y_ref, *, nc):
    u = u_ref[0]
    n_rows = u.shape[0]
    y = jnp.dot(u, m_ref[0], preferred_element_type=F32)
    x = jnp.dot(u, w_ref[0], preferred_element_type=F32)
    hr = x[:, :LANES]
    hi = x[:, LANES:]
    ar = a_ref[0, 0:1, :]
    ai = a_ref[0, 1:2, :]
    lane = lax.broadcasted_iota(I32, (1, LANES), 1)
    fwd = lane < SSM_STATE
    rown = lax.broadcasted_iota(I32, (n_rows, 1), 0) % nc

    def shifted(val, step):
        down = jnp.where(rown >= step, pltpu.roll(val, step, axis=0), 0.0)
        up = jnp.where(rown < nc - step, pltpu.roll(val, n_rows - step, axis=0), 0.0)
        return jnp.where(fwd, down, up)

    step = 1
    while step < nc:
        pr = shifted(hr, step)
        pi = shifted(hi, step)
        hr, hi = hr + ar * pr - ai * pi, hi + ar * pi + ai * pr
        ar, ai = ar * ar - ai * ai, 2.0 * ar * ai
        step *= 2
    hin_r = shifted(hr, 1).astype(BF16)
    hin_i = shifted(hi, 1).astype(BF16)
    y = y + jnp.dot(hin_r, v_ref[0, :LANES, :], preferred_element_type=F32)
    y = y + jnp.dot(hin_i, v_ref[0, LANES:, :], preferred_element_type=F32)
    y_ref[0] = y


def _s5_core(ug, mmat, wmat, vmat, amat, nc):
    g, n, k = ug.shape
    blk = lambda r, c: pl.BlockSpec((1, r, c), lambda i: (i, 0, 0))
    return pl.pallas_call(
        functools.partial(_s5_core_kernel, nc=nc),
        grid=(g,),
        in_specs=[blk(n, k), blk(k, k), blk(k, 2 * LANES), blk(2 * LANES, k), blk(8, LANES)],
        out_specs=blk(n, k),
        out_shape=jax.ShapeDtypeStruct((g, n, k), F32),
        compiler_params=_params(("parallel",), 48),
        name="s5_core",
    )(ug, mmat, wmat, vmat, amat)


def _s5_post_kernel(x_ref, g_ref, yg_first_ref, yg_next_ref, d_ref, w_ref, o_ref, y_scr):
    step = pl.program_id(0)
    x = x_ref[...]
    tm, d = x.shape
    nch = tm // SSM_CHUNK

    def unflatten(yg_ref, slot):
        for slab in range(d // LANES):
            for half in range(SSM_CHUNK_LANES // LANES):
                grps = [yg_ref[slab * SSM_SLAB_GROUPS + gi, :, half * LANES:(half + 1) * LANES]
                        for gi in range(SSM_SLAB_GROUPS)]
                for tt, rows in enumerate(_transpose_lane_groups(grps)):
                    y_scr[slot, slab, pl.ds(half * SSM_HALF_STEPS + tt, nch, stride=SSM_CHUNK), :] = rows

    @pl.when(step == 0)
    def _():
        unflatten(yg_first_ref, 0)

    cur = step % 2
    y = jnp.concatenate([y_scr[cur, slab] for slab in range(d // LANES)], axis=1)
    unflatten(yg_next_ref, 1 - cur)
    hf = _rmsnorm(x, g_ref[...])
    z = jax.nn.gelu(y + d_ref[...] * hf)
    vg = jnp.dot(z.astype(BF16), w_ref[...], preferred_element_type=F32)
    o_ref[...] = x + vg[:, :d] * jax.nn.sigmoid(vg[:, d:])


def _s5_post(x2d, g, yg, d_skip, w_glu_bf16, tm=512):
    m, d = x2d.shape
    ngrp = d // SSM_GROUP
    nch = tm // SSM_CHUNK
    last = m // tm - 1
    row = lambda i: (i, 0)
    fixed = lambda i: (0, 0)
    return pl.pallas_call(
        _s5_post_kernel,
        grid=(m // tm,),
        in_specs=[pl.BlockSpec((tm, d), row), pl.BlockSpec((1, d), fixed),
                  pl.BlockSpec((ngrp, nch, SSM_CHUNK_LANES), lambda i: (0, 0, 0)),
                  pl.BlockSpec((ngrp, nch, SSM_CHUNK_LANES), lambda i: (0, jnp.minimum(i + 1, last), 0)),
                  pl.BlockSpec((1, d), fixed),
                  pl.BlockSpec((d, 2 * d), fixed)],
        out_specs=pl.BlockSpec((tm, d), row),
        out_shape=jax.ShapeDtypeStruct((m, d), F32),
        scratch_shapes=[pltpu.VMEM((2, d // LANES, tm, LANES), F32)],
        compiler_params=_params(("arbitrary",), 48),
        name="s5_gelu_glu",
    )(x2d, g.reshape(1, d), yg, yg, d_skip.reshape(1, d), w_glu_bf16)


def _s5_matrices(lam_re, lam_im, log_dt, b_re, b_im, c_re, c_im):
    t = SSM_CHUNK
    g, p = lam_re.shape[1:]
    c = SSM_GROUP
    dt = jnp.exp(log_dt)[..., None]
    den = lam_re * lam_re + lam_im * lam_im
    mag = jnp.exp(lam_re * dt)
    ang = lam_im * dt
    lb_re = mag * jnp.cos(ang)
    lb_im = mag * jnp.sin(ang)
    nr = lb_re - 1.0
    f_re = (nr * lam_re + lb_im * lam_im) / den
    f_im = (lb_im * lam_re - nr * lam_im) / den
    bb_re = f_re[..., None] * b_re - f_im[..., None] * b_im
    bb_im = f_re[..., None] * b_im + f_im[..., None] * b_re
    lag = jnp.arange(t + 1, dtype=F32)[:, None, None, None]
    pmag = jnp.exp(lag * (lam_re * dt))
    pw_re = pmag * jnp.cos(lag * ang)
    pw_im = pmag * jnp.sin(lag * ang)
    cl_re = c_re[None] * pw_re[..., None, :] - c_im[None] * pw_im[..., None, :]
    cl_im = c_re[None] * pw_im[..., None, :] + c_im[None] * pw_re[..., None, :]
    kern = (jnp.einsum('ldgcp,dgpe->ldgce', cl_re[:t], bb_re)
            - jnp.einsum('ldgcp,dgpe->ldgce', cl_im[:t], bb_im))
    tt = jnp.arange(t)
    by_lag = jnp.concatenate([kern[:0:-1, 1], (kern[0, 0] + kern[0, 1])[None], kern[1:, 0]], axis=0)
    toeplitz = jnp.stack([by_lag[t - 1 - s:2 * t - 1 - s] for s in range(t)], axis=0)
    mmat = jnp.transpose(toeplitz, (2, 0, 4, 1, 3)).reshape(g, t * c, t * c)
    wf_re = pw_re[t - 1 - tt, 0][..., None] * bb_re[0][None]
    wf_im = pw_im[t - 1 - tt, 0][..., None] * bb_re[0][None]
    wf_re = wf_re - pw_im[t - 1 - tt, 0][..., None] * bb_im[0][None]
    wf_im = wf_im + pw_re[t - 1 - tt, 0][..., None] * bb_im[0][None]
    wb_re = pw_re[tt, 1][..., None] * bb_re[1][None] - pw_im[tt, 1][..., None] * bb_im[1][None]
    wb_im = pw_im[tt, 1][..., None] * bb_re[1][None] + pw_re[tt, 1][..., None] * bb_im[1][None]
    wmat = jnp.concatenate([wf_re, wb_re, wf_im, wb_im], axis=2)
    wmat = jnp.transpose(wmat, (1, 0, 3, 2)).reshape(g, t * c, 4 * p)
    vf_re = cl_re[tt + 1, 0]
    vf_im = -cl_im[tt + 1, 0]
    vb_re = cl_re[t - tt, 1]
    vb_im = -cl_im[t - tt, 1]
    vmat = jnp.concatenate([vf_re, vb_re, vf_im, vb_im], axis=3)
    vmat = jnp.transpose(vmat, (1, 3, 0, 2)).reshape(g, 4 * p, t * c)
    a_re = jnp.concatenate([pw_re[t, 0], pw_re[t, 1]], axis=-1)
    a_im = jnp.concatenate([pw_im[t, 0], pw_im[t, 1]], axis=-1)
    amat = jnp.zeros((g, 8, 2 * p), F32).at[:, 0].set(a_re).at[:, 1].set(a_im)
    return mmat.astype(BF16), wmat.astype(BF16), vmat.astype(BF16), amat


def _s5_mixer(x, mix_g, lam_re, lam_im, log_dt, b_re, b_im, c_re, c_im, d_skip, w_glu):
    b, s, d = x.shape
    t = SSM_CHUNK
    g = d // SSM_GROUP
    nc = s // t
    x2d = x.reshape(b * s, d)
    ug = _norm_flatten(x2d, mix_g)
    mmat, wmat, vmat, amat = _s5_matrices(lam_re, lam_im, log_dt, b_re, b_im, c_re, c_im)
    yg = _s5_core(ug, mmat, wmat, vmat, amat, nc)
    out = _s5_post(x2d, mix_g, yg, d_skip, w_glu.astype(BF16))
    return out.reshape(b, s, d)


def kernel(x, mix_norm_even, w_in, conv_w, conv_b, conv_ln_g, conv_ln_b, q_norm, k_norm, w_out, mix_norm_odd, ssm_lam_re, ssm_lam_im, ssm_log_dt, ssm_b_re, ssm_b_im, ssm_c_re, ssm_c_im, ssm_d, w_glu, ffn_norm, w_router, b_router, w_e_gate, w_e_up, w_e_down):
    depth = ffn_norm.shape[0]
    for layer in range(depth):
        i = layer // 2
        if layer % 2 == 0:
            x = _even_mixer(x, mix_norm_even[i], w_in[i], conv_w[i], conv_b[i], conv_ln_g[i],
                            conv_ln_b[i], q_norm[i], k_norm[i], w_out[i])
        else:
            x = _s5_mixer(x, mix_norm_odd[i], ssm_lam_re[i], ssm_lam_im[i], ssm_log_dt[i],
                          ssm_b_re[i], ssm_b_im[i], ssm_c_re[i], ssm_c_im[i], ssm_d[i], w_glu[i])
        x = _moe(x, ffn_norm[layer], w_router[layer], b_router[layer],
                 w_e_gate, w_e_up, w_e_down, layer)
    return x
```

```python
import functools
import math

import jax
import jax.numpy as jnp
from jax import lax
from jax.experimental import pallas as pl
from jax.experimental.pallas import tpu as pltpu

F32 = jnp.float32
BF16 = jnp.bfloat16
I32 = jnp.int32

EPS = 1e-6
NEG_INF = -1e30
LOG2_E = math.log2(math.e)

CONV_CH = 512
CONV_WIDTH = 31
ATT_HEADS = 8
HEAD_DIM = 64
ATT_WIDTH = ATT_HEADS * HEAD_DIM
DILATED = ((128, 1), (512, 4), (2048, 16))
SSM_GROUP = 16
SSM_STATE = 64
N_EXPERTS = 16
CAPACITY_FACTOR = 2
MOE_GATHER_TOKENS = 1024
MOE_GATHER_GROUP = 16
MOE_SCATTER_TOKENS = 256

LANES = 128
F32_MIN_EXP = -126
F32_MANTISSA_BITS = 23
MIB = 1024 * 1024


def _params(sem, vmem_mib):
    return pltpu.CompilerParams(dimension_semantics=sem, vmem_limit_bytes=vmem_mib * MIB)


def _rmsnorm(x, g):
    return x * lax.rsqrt(jnp.mean(x * x, axis=-1, keepdims=True) + EPS) * g


def _norm_matmul_kernel(x_ref, g_ref, w_ref, o_ref):
    y = _rmsnorm(x_ref[...], g_ref[...])
    o_ref[...] = jnp.dot(y.astype(BF16), w_ref[...], preferred_element_type=F32)


def _norm_matmul(x2d, g, w_bf16, tm=512):
    m, d = x2d.shape
    n = w_bf16.shape[1]
    return pl.pallas_call(
        _norm_matmul_kernel,
        grid=(m // tm,),
        in_specs=[pl.BlockSpec((tm, d), lambda i: (i, 0)),
                  pl.BlockSpec((1, d), lambda i: (0, 0)),
                  pl.BlockSpec((d, n), lambda i: (0, 0))],
        out_specs=pl.BlockSpec((tm, n), lambda i: (i, 0)),
        out_shape=jax.ShapeDtypeStruct((m, n), F32),
        compiler_params=_params(("parallel",), 48),
        name="norm_in_proj",
    )(x2d, g.reshape(1, d), w_bf16)


CONV_HALO = 16
CONV_ROWS = 256


def _conv_kernel(val_ref, gate_ref, w_ref, b_ref, o_ref, hp_ref):
    s = val_ref.shape[1]
    zeros = jnp.zeros((CONV_HALO, LANES), F32)
    hp_ref[0:CONV_HALO, :] = zeros
    hp_ref[CONV_HALO + s:2 * CONV_HALO + s, :] = zeros
    hp_ref[CONV_HALO:CONV_HALO + s, :] = val_ref[0] * jax.nn.sigmoid(gate_ref[0])
    first = CONV_HALO - CONV_WIDTH // 2
    for c in range(s // CONV_ROWS):
        r0 = c * CONV_ROWS
        acc = jnp.broadcast_to(b_ref[...], (CONV_ROWS, LANES))
        for k in range(CONV_WIDTH):
            acc = acc + w_ref[k:k + 1, :] * hp_ref[r0 + first + k:r0 + first + k + CONV_ROWS, :]
        o_ref[0, r0:r0 + CONV_ROWS, :] = acc


def _conformer_conv(proj3, conv_w, conv_b):
    b, s, _ = proj3.shape
    nct = CONV_CH // LANES
    return pl.pallas_call(
        _conv_kernel,
        grid=(b, nct),
        in_specs=[pl.BlockSpec((1, s, LANES), lambda i, j: (i, 0, j)),
                  pl.BlockSpec((1, s, LANES), lambda i, j: (i, 0, nct + j)),
                  pl.BlockSpec((CONV_WIDTH, LANES), lambda i, j: (0, j)),
                  pl.BlockSpec((1, LANES), lambda i, j: (0, j))],
        out_specs=pl.BlockSpec((1, s, LANES), lambda i, j: (i, 0, j)),
        out_shape=jax.ShapeDtypeStruct((b, s, CONV_CH), F32),
        scratch_shapes=[pltpu.VMEM((s + 2 * CONV_HALO, LANES), F32)],
        compiler_params=_params(("parallel", "parallel"), 32),
        name="glu_dwconv",
    )(proj3, proj3, conv_w, conv_b.reshape(1, CONV_CH))


ATT_QB = 128
ATT_RAD = 64
ATT_KW = ATT_QB + 2 * ATT_RAD
ATT_PREP_ROWS = 512
ATT_UNROLL = 32


def _attn_kernel(slopes_ref, q_ref, k_ref, v_ref, qg_ref, kg_ref, o_ref,
                 qs_ref, kp_ref, vp_ref, ob_ref, mb_ref, *, seq, pad):
    pair = pl.program_id(1)
    lane = lax.broadcasted_iota(I32, (1, LANES), 1)
    lo = lane < HEAD_DIM
    scale = LOG2_E / math.sqrt(HEAD_DIM)
    spare = (HEAD_DIM, 0)

    zpad = jnp.zeros((pad, LANES), F32)
    for hh in range(2):
        kpad = jnp.broadcast_to(jnp.where(lane == spare[hh], NEG_INF, 0.0), (pad, LANES))
        kp_ref[hh, 0:pad, :] = kpad
        kp_ref[hh, pad + seq:2 * pad + seq, :] = kpad
        vp_ref[hh, 0:pad, :] = zpad
        vp_ref[hh, pad + seq:2 * pad + seq, :] = zpad

    def head_rms(x, g):
        x2 = x * x
        s_lo = jnp.sum(jnp.where(lo, x2, 0.0), axis=-1, keepdims=True)
        s_hi = jnp.sum(jnp.where(lo, 0.0, x2), axis=-1, keepdims=True)
        ms = jnp.where(lo, s_lo, s_hi) * (1.0 / HEAD_DIM)
        return x * lax.rsqrt(ms + EPS) * g

    def prep(i, carry):
        r = pl.multiple_of(i * ATT_PREP_ROWS, ATT_PREP_ROWS)
        rows = pl.ds(r, ATT_PREP_ROWS)
        prows = pl.ds(pad + r, ATT_PREP_ROWS)
        qn = head_rms(q_ref[0, rows, :], qg_ref[...]) * scale
        qs_ref[0, rows, :] = jnp.where(lo, qn, jnp.where(lane == spare[0], 1.0, 0.0))
        qs_ref[1, rows, :] = jnp.where(lo, jnp.where(lane == spare[1], 1.0, 0.0), qn)
        kn = head_rms(k_ref[0, rows, :], kg_ref[...])
        kp_ref[0, prows, :] = jnp.where(lo, kn, 0.0)
        kp_ref[1, prows, :] = jnp.where(lo, 0.0, kn)
        v = v_ref[0, rows, :]
        vp_ref[0, prows, :] = jnp.where(lo, v, jnp.where(lane == spare[0], 1.0, 0.0))
        vp_ref[1, prows, :] = jnp.where(lo, jnp.where(lane == spare[1], 1.0, 0.0), v)
        return carry

    lax.fori_loop(0, seq // ATT_PREP_ROWS, prep, 0)

    rowi = lax.broadcasted_iota(I32, (ATT_QB, ATT_KW), 0)
    coli = lax.broadcasted_iota(I32, (ATT_QB, ATT_KW), 1)
    arel = jnp.abs(coli - ATT_RAD - rowi)

    for hh in range(2):
        slope = slopes_ref[pair * 2 + hh] * LOG2_E
        for bi, (window, dil) in enumerate(DILATED):
            assert window // (2 * dil) == ATT_RAD
            nbc = seq // dil // ATT_QB
            bias = jnp.where(arel <= ATT_RAD, -(slope * dil) * arel.astype(F32), NEG_INF)

            def blk(j, carry, bi=bi, dil=dil, nbc=nbc, bias=bias, hh=hh):
                c = j // nbc
                n = j - c * nbc
                qstart = c + dil * ATT_QB * n
                kstart = pad + c + dil * (ATT_QB * n - ATT_RAD)
                if dil == 1:
                    qsl = pl.ds(qstart, ATT_QB)
                    ksl = pl.ds(kstart, ATT_KW)
                else:
                    qsl = pl.ds(qstart, ATT_QB, stride=dil)
                    ksl = pl.ds(kstart, ATT_KW, stride=dil)
                qb = qs_ref[hh, qsl, :].astype(BF16)
                kw = kp_ref[hh, ksl, :].astype(BF16)
                vw = vp_ref[hh, ksl, :].astype(BF16)
                sc = lax.dot_general(qb, kw, (((1,), (1,)), ((), ())),
                                     preferred_element_type=F32) + bias
                m = jnp.max(sc, axis=-1, keepdims=True)
                p = jnp.exp2(sc - m)
                ob_ref[bi, qsl, :] = jnp.dot(p.astype(BF16), vw, preferred_element_type=F32)
                mb_ref[bi, qsl, :] = jnp.broadcast_to(m, (ATT_QB, LANES))
                return carry

            lax.fori_loop(0, seq // ATT_QB, blk, 0, unroll=ATT_UNROLL)

        def combine(i, carry, hh=hh):
            r = pl.multiple_of(i * ATT_PREP_ROWS, ATT_PREP_ROWS)
            rows = pl.ds(r, ATT_PREP_ROWS)
            nbr = len(DILATED)
            ms = [mb_ref[bi, rows, :] for bi in range(nbr)]
            mx = functools.reduce(jnp.maximum, ms)
            acc = sum(jnp.exp2(ms[bi] - mx) * ob_ref[bi, rows, :] for bi in range(nbr))
            den = jnp.sum(jnp.where(lane == spare[hh], acc, 0.0), axis=-1, keepdims=True)
            res = acc / den
            if hh == 0:
                o_ref[0, rows, :] = res
            else:
                o_ref[0, rows, :] = jnp.where(lo, o_ref[0, rows, :], res)
            return carry

        lax.fori_loop(0, seq // ATT_PREP_ROWS, combine, 0)


def _dilated_attention(proj3, q_norm, k_norm):
    b, s, _ = proj3.shape
    pad = ATT_RAD * max(d for _, d in DILATED)
    npair = ATT_HEADS // 2
    qoff = 2 * CONV_CH // LANES
    slopes = jnp.asarray([2.0 ** (-8.0 * (h + 1) / ATT_HEADS) for h in range(ATT_HEADS)], F32)
    qg = jnp.tile(q_norm.reshape(1, HEAD_DIM), (1, 2))
    kg = jnp.tile(k_norm.reshape(1, HEAD_DIM), (1, 2))
    kernel = functools.partial(_attn_kernel, seq=s, pad=pad)
    return pl.pallas_call(
        kernel,
        grid=(b, npair),
        in_specs=[pl.BlockSpec(memory_space=pltpu.SMEM),
                  pl.BlockSpec((1, s, LANES), lambda i, j: (i, 0, qoff + j)),
                  pl.BlockSpec((1, s, LANES), lambda i, j: (i, 0, qoff + npair + j)),
                  pl.BlockSpec((1, s, LANES), lambda i, j: (i, 0, qoff + 2 * npair + j)),
                  pl.BlockSpec((1, LANES), lambda i, j: (0, 0)),
                  pl.BlockSpec((1, LANES), lambda i, j: (0, 0))],
        out_specs=pl.BlockSpec((1, s, LANES), lambda i, j: (i, 0, j)),
        out_shape=jax.ShapeDtypeStruct((b, s, ATT_WIDTH), F32),
        scratch_shapes=[pltpu.VMEM((2, s, LANES), F32),
                        pltpu.VMEM((2, s + 2 * pad, LANES), F32),
                        pltpu.VMEM((2, s + 2 * pad, LANES), F32),
                        pltpu.VMEM((len(DILATED), s, LANES), F32),
                        pltpu.VMEM((len(DILATED), s, LANES), F32)],
        compiler_params=_params(("parallel", "parallel"), 56),
        name="dilated_attn",
    )(slopes, proj3, proj3, proj3, qg, kg)


def _outproj_kernel(x_ref, conv_ref, att_ref, g_ref, b_ref, w1_ref, w2_ref, o_ref):
    c = conv_ref[...]
    mu = jnp.mean(c, axis=-1, keepdims=True)
    cc = c - mu
    var = jnp.mean(cc * cc, axis=-1, keepdims=True)
    hn = cc * lax.rsqrt(var + EPS) * g_ref[...] + b_ref[...]
    a = hn * jax.nn.sigmoid(hn)
    y = jnp.dot(a.astype(BF16), w1_ref[...], preferred_element_type=F32)
    y = y + jnp.dot(att_ref[...].astype(BF16), w2_ref[...], preferred_element_type=F32)
    o_ref[...] = x_ref[...] + y


def _out_projection(x2d, conv2d, att2d, ln_g, ln_b, w_out_bf16, tm=512):
    m, d = x2d.shape
    row = lambda i: (i, 0)
    fixed = lambda i: (0, 0)
    return pl.pallas_call(
        _outproj_kernel,
        grid=(m // tm,),
        in_specs=[pl.BlockSpec((tm, d), row),
                  pl.BlockSpec((tm, CONV_CH), row),
                  pl.BlockSpec((tm, ATT_WIDTH), row),
                  pl.BlockSpec((1, CONV_CH), fixed),
                  pl.BlockSpec((1, CONV_CH), fixed),
                  pl.BlockSpec((CONV_CH, d), fixed),
                  pl.BlockSpec((ATT_WIDTH, d), lambda i: (1, 0))],
        out_specs=pl.BlockSpec((tm, d), row),
        out_shape=jax.ShapeDtypeStruct((m, d), F32),
        compiler_params=_params(("parallel",), 32),
        name="ln_out_proj",
    )(x2d, conv2d, att2d, ln_g.reshape(1, CONV_CH), ln_b.reshape(1, CONV_CH), w_out_bf16, w_out_bf16)


def _even_mixer(x, mix_g, w_in, conv_w, conv_b, ln_g, ln_b, q_norm, k_norm, w_out):
    b, s, d = x.shape
    x2d = x.reshape(b * s, d)
    proj = _norm_matmul(x2d, mix_g, w_in.astype(BF16))
    proj3 = proj.reshape(b, s, -1)
    conv = _conformer_conv(proj3, conv_w, conv_b)
    att = _dilated_attention(proj3, q_norm, k_norm)
    out = _out_projection(x2d, conv.reshape(b * s, CONV_CH), att.reshape(b * s, ATT_WIDTH),
                          ln_g, ln_b, w_out.astype(BF16))
    return out.reshape(b, s, d)


def _router_kernel(x_ref, g_ref, wr_ref, br_ref, h_ref, aff_ref):
    h = _rmsnorm(x_ref[0], g_ref[...])
    logits = jnp.dot(h, wr_ref[...], precision=lax.Precision.HIGHEST,
                     preferred_element_type=F32) + br_ref[...]
    lt = logits.T[:N_EXPERTS, :]
    mx = jnp.max(lt, axis=0, keepdims=True)
    ex = jnp.exp(lt - mx)
    aff_ref[0] = ex / jnp.sum(ex, axis=0, keepdims=True)
    h_ref[0] = h


def _router(x, g, w_router, b_router, tm=512):
    b, s, d = x.shape
    wr = jnp.zeros((d, LANES), F32).at[:, :N_EXPERTS].set(w_router)
    br = jnp.zeros((1, LANES), F32).at[0, :N_EXPERTS].set(b_router)
    return pl.pallas_call(
        _router_kernel,
        grid=(b, s // tm),
        in_specs=[pl.BlockSpec((1, tm, d), lambda i, j: (i, j, 0)),
                  pl.BlockSpec((1, d), lambda i, j: (0, 0)),
                  pl.BlockSpec((d, LANES), lambda i, j: (0, 0)),
                  pl.BlockSpec((1, LANES), lambda i, j: (0, 0))],
        out_specs=[pl.BlockSpec((1, tm, d), lambda i, j: (i, j, 0)),
                   pl.BlockSpec((1, N_EXPERTS, tm), lambda i, j: (i, 0, j))],
        out_shape=[jax.ShapeDtypeStruct((b, s, d), F32),
                   jax.ShapeDtypeStruct((b, N_EXPERTS, s), F32)],
        compiler_params=_params(("parallel", "parallel"), 32),
        name="moe_router",
    )(x, g.reshape(1, d), wr, br)


def _select_kernel(aff_ref, post_ref, starts_ref, idx_ref, gatet_ref, cs_ref, *, cap):
    a = aff_ref[0]
    s = a.shape[1]
    capf = float(cap)

    def count(mask):
        return jnp.sum(jnp.where(mask, 1.0, 0.0), axis=-1, keepdims=True)

    def enough(cand):
        return count(a >= cand) >= capf

    tiny = float(2.0 ** F32_MIN_EXP)
    thr = jnp.full((N_EXPERTS, 1), tiny, F32)
    normal = enough(thr)
    for j in range(6, -1, -1):
        cand = thr * float(2.0 ** (2 ** j))
        thr = jnp.where(enough(cand), cand, thr)
    step = thr * 0.5
    for _ in range(F32_MANTISSA_BITS):
        cand = thr + step
        thr = jnp.where(enough(cand), cand, thr)
        step = step * 0.5
    thr = jnp.where(normal, thr, 0.0)

    ri = lax.broadcasted_iota(I32, (LANES, LANES), 0)
    ci = lax.broadcasted_iota(I32, (LANES, LANES), 1)
    tri = jnp.where(ri < ci, 1.0, 0.0).astype(BF16)

    lane = lax.broadcasted_iota(I32, (1, LANES), 1)
    chunks_per_tile = MOE_SCATTER_TOKENS // LANES

    def exclusive_cumsum(mask):
        ind = jnp.where(mask, 1.0, 0.0)
        off = jnp.zeros((N_EXPERTS, 1), F32)
        starts = jnp.zeros((N_EXPERTS, LANES), F32)
        for j in range(s // LANES):
            if j % chunks_per_tile == 0:
                starts = jnp.where(lane == j // chunks_per_tile, off, starts)
            xc = ind[:, j * LANES:(j + 1) * LANES]
            cs_ref[:, j * LANES:(j + 1) * LANES] = (
                jnp.dot(xc.astype(BF16), tri, preferred_element_type=F32) + off)
            off = off + jnp.sum(xc, axis=-1, keepdims=True)
        starts = jnp.where(lane == s // MOE_SCATTER_TOKENS, off, starts)
        return cs_ref[...], starts

    gt = a > thr
    eq = a == thr
    need = capf - count(gt)
    tie_rank, _ = exclusive_cumsum(eq)
    sel = gt | (eq & (tie_rank < need))
    slot, starts = exclusive_cumsum(sel)
    slot = jnp.where(sel, slot, -1.0)
    starts_ref[0] = starts.astype(I32)
    fill = jnp.zeros((LANES - N_EXPERTS, s), F32)
    post_ref[0] = jnp.concatenate([slot, fill], axis=0).T

    radix = float(s)
    tok = lax.broadcasted_iota(I32, (N_EXPERTS, s), 1).astype(F32)
    packed = jnp.where(sel, (tok - slot) * radix + tok, -1.0)
    gate = jnp.where(sel, a, 0.0)
    for k in range(s.bit_length() - 1):
        dist = jnp.floor(packed * (1.0 / radix))
        bit = jnp.floor(dist * 0.5 ** k) - 2.0 * jnp.floor(dist * 0.5 ** (k + 1))
        moving = jnp.where(packed >= 0.0, bit, 0.0) == 1.0
        arrived = pltpu.roll(jnp.where(moving, packed, -1.0), s - 2 ** k, axis=1)
        arrived_gate = pltpu.roll(jnp.where(moving, gate, 0.0), s - 2 ** k, axis=1)
        gate = jnp.where(arrived >= 0.0, arrived_gate, jnp.where(moving, 0.0, gate))
        packed = jnp.where(arrived >= 0.0, arrived, jnp.where(moving, -1.0, packed))
    head = packed[:, :cap]
    idx_ref[0] = (head - radix * jnp.floor(head * (1.0 / radix))).astype(I32)
    gatet_ref[0] = jnp.concatenate([gate[:, :cap], jnp.zeros((LANES - N_EXPERTS, cap), F32)], axis=0).T


def _select(aff, cap):
    b, e, s = aff.shape
    assert s // MOE_SCATTER_TOKENS < LANES
    assert s & (s - 1) == 0 and s * s <= 2 ** 24
    blk = pl.BlockSpec((1, e, s), lambda i: (i, 0, 0))
    return pl.pallas_call(
        functools.partial(_select_kernel, cap=cap),
        grid=(b,),
        in_specs=[blk],
        out_specs=[pl.BlockSpec((1, s, LANES), lambda i: (i, 0, 0)),
                   pl.BlockSpec((1, e, LANES), lambda i: (i, 0, 0)),
                   pl.BlockSpec((1, e, cap), lambda i: (i, 0, 0)),
                   pl.BlockSpec((1, cap, LANES), lambda i: (i, 0, 0))],
        out_shape=[jax.ShapeDtypeStruct((b, s, LANES), F32),
                   jax.ShapeDtypeStruct((b, e, LANES), I32),
                   jax.ShapeDtypeStruct((b, e, cap), I32),
                   jax.ShapeDtypeStruct((b, cap, LANES), F32)],
        scratch_shapes=[pltpu.VMEM((e, s), F32)],
        compiler_params=_params(("parallel",), 32),
        name="moe_select",
    )(aff)


def _gather_kernel(idx_ref, h_ref, x_ref, rows_scr, *, cap):
    group = rows_scr.shape[1]

    def copy_group(gidx, carry):
        base = gidx * group
        for r in range(group):
            tok = idx_ref[0, 0, 0, base + r]
            rows_scr[gidx, pl.ds(r, 1), :] = h_ref[0, pl.ds(tok, 1), :]
        return carry

    lax.fori_loop(0, cap // group, copy_group, 0)
    x_ref[0, 0] = rows_scr[...].reshape(cap, rows_scr.shape[2]).astype(BF16)


def _gather_rows(h, idx, cap):
    b, s, d = h.shape
    e = idx.shape[1]
    return pl.pallas_call(
        functools.partial(_gather_kernel, cap=cap),
        grid=(b, e),
        in_specs=[pl.BlockSpec((1, 1, 1, cap), lambda i, j: (i, j, 0, 0), memory_space=pltpu.SMEM),
                  pl.BlockSpec((1, s, d), lambda i, j: (i, 0, 0), pipeline_mode=pl.Buffered(1))],
        out_specs=pl.BlockSpec((1, 1, cap, d), lambda i, j: (i, j, 0, 0)),
        out_shape=jax.ShapeDtypeStruct((b, e, cap, d), BF16),
        scratch_shapes=[pltpu.VMEM((cap // MOE_GATHER_GROUP, MOE_GATHER_GROUP, d), F32)],
        compiler_params=_params(("parallel", "parallel"), 40),
        name="moe_gather",
    )(idx.reshape(b, e, 1, cap), h)


def _expert_kernel(x_ref, gate_ref, wg_ref, wu_ref, wd_ref, y_ref, wg_s, wu_s, wd_s):
    ei = pl.program_id(0)

    @pl.when(pl.program_id(1) == 0)
    def _():
        wg_s[...] = wg_ref[0, 0].astype(BF16)
        wu_s[...] = wu_ref[0, 0].astype(BF16)
        wd_s[...] = wd_ref[0, 0].astype(BF16)

    lane = lax.broadcasted_iota(I32, (1, LANES), 1)
    gate = jnp.sum(jnp.where(lane == ei, gate_ref[0], 0.0), axis=-1, keepdims=True)
    x = x_ref[0, 0]
    g = jnp.dot(x, wg_s[...], preferred_element_type=F32)
    u = jnp.dot(x, wu_s[...], preferred_element_type=F32)
    a = (g * jax.nn.sigmoid(g) * u).astype(BF16)
    y = jnp.dot(a, wd_s[...], preferred_element_type=F32)
    y_ref[0, 0] = (y * gate).astype(BF16)


def _experts(xin, gate_t, w_gate, w_up, w_down, layer):
    b, e, cap, d = xin.shape
    f = w_gate.shape[3]
    return pl.pallas_call(
        _expert_kernel,
        grid=(e, b),
        in_specs=[pl.BlockSpec((1, 1, cap, d), lambda j, i: (i, j, 0, 0)),
                  pl.BlockSpec((1, cap, LANES), lambda j, i: (i, 0, 0)),
                  pl.BlockSpec((1, 1, d, f), lambda j, i: (layer, j, 0, 0)),
                  pl.BlockSpec((1, 1, d, f), lambda j, i: (layer, j, 0, 0)),
                  pl.BlockSpec((1, 1, f, d), lambda j, i: (layer, j, 0, 0))],
        out_specs=pl.BlockSpec((1, 1, cap, d), lambda j, i: (i, j, 0, 0)),
        out_shape=jax.ShapeDtypeStruct((b, e, cap, d), BF16),
        scratch_shapes=[pltpu.VMEM((d, f), BF16), pltpu.VMEM((d, f), BF16), pltpu.VMEM((f, d), BF16)],
        compiler_params=_params(("parallel", "arbitrary"), 56),
        name="moe_experts",
    )(xin, gate_t, w_gate, w_up, w_down)


def _scatter_kernel(starts_ref, x_ref, y_ref, post_ref, o_ref, *, cap):
    bi = pl.program_id(0)
    ti = pl.program_id(1)
    win = MOE_SCATTER_TOKENS
    lane = lax.broadcasted_iota(I32, (1, LANES), 1)
    wslot = lax.broadcasted_iota(I32, (1, win), 1).astype(F32)
    post = post_ref[0]

    def slot_column(e):
        return jnp.sum(jnp.where(lane == e, post, 0.0), axis=-1, keepdims=True)

    def window_start(e):
        first = starts_ref[bi, e, ti]
        off = jnp.minimum(lax.shift_left(lax.shift_right_logical(first, 4), 4), cap - win)
        return pl.multiple_of(off, 16)

    acc = x_ref[0]
    for e in range(N_EXPERTS):
        off = window_start(e)
        onehot = jnp.where(slot_column(e) - off.astype(F32) == wslot, 1.0, 0.0).astype(BF16)
        acc = acc + jnp.dot(onehot, y_ref[0, e, pl.ds(off, win), :], preferred_element_type=F32)
    o_ref[0] = acc

    tail = cap - win
    for e in range(N_EXPERTS):
        covered = window_start(e) + win

        @pl.when(starts_ref[bi, e, ti + 1] > covered)
        def _(e=e, covered=covered):
            pcol = slot_column(e)
            hit = (pcol - float(tail) == wslot) & (pcol >= covered.astype(F32))
            o_ref[0] += jnp.dot(jnp.where(hit, 1.0, 0.0).astype(BF16),
                                y_ref[0, e, tail:cap, :], preferred_element_type=F32)


def _scatter(x, y, post, starts, cap):
    b, s, d = x.shape
    e = y.shape[1]
    tn = MOE_SCATTER_TOKENS
    assert cap % 16 == 0 and tn <= cap <= 2 * tn
    grid_spec = pltpu.PrefetchScalarGridSpec(
        num_scalar_prefetch=1,
        grid=(b, s // tn),
        in_specs=[pl.BlockSpec((1, tn, d), lambda i, j, st: (i, j, 0)),
                  pl.BlockSpec((1, e, cap, d), lambda i, j, st: (i, 0, 0, 0),
                               pipeline_mode=pl.Buffered(1)),
                  pl.BlockSpec((1, tn, LANES), lambda i, j, st: (i, j, 0))],
        out_specs=pl.BlockSpec((1, tn, d), lambda i, j, st: (i, j, 0)),
    )
    return pl.pallas_call(
        functools.partial(_scatter_kernel, cap=cap),
        grid_spec=grid_spec,
        out_shape=jax.ShapeDtypeStruct((b, s, d), F32),
        compiler_params=_params(("parallel", "parallel"), 48),
        name="moe_scatter",
    )(starts, x, y, post)


def _moe(x, ffn_g, w_router, b_router, w_gate, w_up, w_down, layer):
    b, s, d = x.shape
    cap = CAPACITY_FACTOR * s // N_EXPERTS
    h, aff = _router(x, ffn_g, w_router, b_router)
    post, starts, idx, gate_t = _select(aff, cap)
    xin = _gather_rows(h, idx, cap)
    y = _experts(xin, gate_t, w_gate, w_up, w_down, layer)
    return _scatter(x, y, post, starts, cap)


SSM_CHUNK = 16


SSM_SLAB_GROUPS = LANES // SSM_GROUP
SSM_CHUNK_LANES = SSM_CHUNK * SSM_GROUP
SSM_HALF_STEPS = LANES // SSM_GROUP


def _transpose_lane_groups(xs):
    lgrp = lax.broadcasted_iota(I32, (1, LANES), 1) // SSM_GROUP
    xs = list(xs)
    assert len(xs) == SSM_SLAB_GROUPS
    h = SSM_SLAB_GROUPS // 2
    while h:
        upper = (lgrp & h) != 0
        for a0 in range(SSM_SLAB_GROUPS):
            if a0 & h:
                continue
            x0, x1 = xs[a0], xs[a0 + h]
            xs[a0] = jnp.where(upper, pltpu.roll(x1, h * SSM_GROUP, axis=1), x0)
            xs[a0 + h] = jnp.where(upper, x1, pltpu.roll(x0, LANES - h * SSM_GROUP, axis=1))
        h //= 2
    return xs


def _norm_flatten_kernel(x_ref, g_ref, ug_ref, u_scr):
    tm, d = x_ref.shape
    nch = tm // SSM_CHUNK
    u = _rmsnorm(x_ref[...], g_ref[...])
    lgrp = lax.broadcasted_iota(I32, (1, LANES), 1) // SSM_GROUP
    for slab in range(d // LANES):
        u_scr[slab] = u[:, slab * LANES:(slab + 1) * LANES]
        steps = [u_scr[slab, pl.ds(t, nch, stride=SSM_CHUNK), :] for t in range(SSM_CHUNK)]
        for gi in range(SSM_SLAB_GROUPS):
            for half in range(SSM_CHUNK_LANES // LANES):
                acc = None
                for tt in range(SSM_HALF_STEPS):
                    shift = ((tt - gi) * SSM_GROUP) % LANES
                    piece = steps[half * SSM_HALF_STEPS + tt]
                    if shift:
                        piece = pltpu.roll(piece, shift, axis=1)
                    acc = piece if acc is None else jnp.where(lgrp == tt, piece, acc)
                ug_ref[slab * SSM_SLAB_GROUPS + gi, :, half * LANES:(half + 1) * LANES] = acc.astype(BF16)


def _norm_flatten(x2d, g, tm=512):
    m, d = x2d.shape
    ngrp = d // SSM_GROUP
    nch = tm // SSM_CHUNK
    return pl.pallas_call(
        _norm_flatten_kernel,
        grid=(m // tm,),
        in_specs=[pl.BlockSpec((tm, d), lambda i: (i, 0)),
                  pl.BlockSpec((1, d), lambda i: (0, 0))],
        out_specs=pl.BlockSpec((ngrp, nch, SSM_CHUNK_LANES), lambda i: (0, i, 0)),
        out_shape=jax.ShapeDtypeStruct((ngrp, m // SSM_CHUNK, SSM_CHUNK_LANES), BF16),
        scratch_shapes=[pltpu.VMEM((d // LANES, tm, LANES), F32)],
        compiler_params=_params(("parallel",), 32),
        name="s5_norm_flatten",
    )(x2d, g.reshape(1, d))


def _s5_core_kernel(u_ref, m_ref, w_ref, v_ref, a_ref, y_ref, *, nc):
    u = u_ref[0]
    n_rows = u.shape[0]
    y = jnp.dot(u, m_ref[0], preferred_element_type=F32)
    x = jnp.dot(u, w_ref[0], preferred_element_type=F32)
    hr = x[:, :LANES]
    hi = x[:, LANES:]
    ar = a_ref[0, 0:1, :]
    ai = a_ref[0, 1:2, :]
    lane = lax.broadcasted_iota(I32, (1, LANES), 1)
    fwd = lane < SSM_STATE
    rown = lax.broadcasted_iota(I32, (n_rows, 1), 0) % nc

    def shifted(val, step):
        down = jnp.where(rown >= step, pltpu.roll(val, step, axis=0), 0.0)
        up = jnp.where(rown < nc - step, pltpu.roll(val, n_rows - step, axis=0), 0.0)
        return jnp.where(fwd, down, up)

    step = 1
    while step < nc:
        pr = shifted(hr, step)
        pi = shifted(hi, step)
        hr, hi = hr + ar * pr - ai * pi, hi + ar * pi + ai * pr
        ar, ai = ar * ar - ai * ai, 2.0 * ar * ai
        step *= 2
    hin_r = shifted(hr, 1).astype(BF16)
    hin_i = shifted(hi, 1).astype(BF16)
    y = y + jnp.dot(hin_r, v_ref[0, :LANES, :], preferred_element_type=F32)
    y = y + jnp.dot(hin_i, v_ref[0, LANES:, :], preferred_element_type=F32)
    y_ref[0] = y


def _s5_core(ug, mmat, wmat, vmat, amat, nc):
    g, n, k = ug.shape
    blk = lambda r, c: pl.BlockSpec((1, r, c), lambda i: (i, 0, 0))
    return pl.pallas_call(
        functools.partial(_s5_core_kernel, nc=nc),
        grid=(g,),
        in_specs=[blk(n, k), blk(k, k), blk(k, 2 * LANES), blk(2 * LANES, k), blk(8, LANES)],
        out_specs=blk(n, k),
        out_shape=jax.ShapeDtypeStruct((g, n, k), F32),
        compiler_params=_params(("parallel",), 48),
        name="s5_core",
    )(ug, mmat, wmat, vmat, amat)


def _s5_post_kernel(x_ref, g_ref, yg_first_ref, yg_next_ref, d_ref, w_ref, o_ref, y_scr):
    step = pl.program_id(0)
    x = x_ref[...]
    tm, d = x.shape
    nch = tm // SSM_CHUNK

    def unflatten(yg_ref, slot):
        for slab in range(d // LANES):
            for half in range(SSM_CHUNK_LANES // LANES):
                grps = [yg_ref[slab * SSM_SLAB_GROUPS + gi, :, half * LANES:(half + 1) * LANES]
                        for gi in range(SSM_SLAB_GROUPS)]
                for tt, rows in enumerate(_transpose_lane_groups(grps)):
                    y_scr[slot, slab, pl.ds(half * SSM_HALF_STEPS + tt, nch, stride=SSM_CHUNK), :] = rows

    @pl.when(step == 0)
    def _():
        unflatten(yg_first_ref, 0)

    cur = step % 2
    y = jnp.concatenate([y_scr[cur, slab] for slab in range(d // LANES)], axis=1)
    unflatten(yg_next_ref, 1 - cur)
    hf = _rmsnorm(x, g_ref[...])
    z = jax.nn.gelu(y + d_ref[...] * hf)
    vg = jnp.dot(z.astype(BF16), w_ref[...], preferred_element_type=F32)
    o_ref[...] = x + vg[:, :d] * jax.nn.sigmoid(vg[:, d:])


def _s5_post(x2d, g, yg, d_skip, w_glu_bf16, tm=512):
    m, d = x2d.shape
    ngrp = d // SSM_GROUP
    nch = tm // SSM_CHUNK
    last = m // tm - 1
    row = lambda i: (i, 0)
    fixed = lambda i: (0, 0)
    return pl.pallas_call(
        _s5_post_kernel,
        grid=(m // tm,),
        in_specs=[pl.BlockSpec((tm, d), row), pl.BlockSpec((1, d), fixed),
                  pl.BlockSpec((ngrp, nch, SSM_CHUNK_LANES), lambda i: (0, 0, 0)),
                  pl.BlockSpec((ngrp, nch, SSM_CHUNK_LANES), lambda i: (0, jnp.minimum(i + 1, last), 0)),
                  pl.BlockSpec((1, d), fixed),
                  pl.BlockSpec((d, 2 * d), fixed)],
        out_specs=pl.BlockSpec((tm, d), row),
        out_shape=jax.ShapeDtypeStruct((m, d), F32),
        scratch_shapes=[pltpu.VMEM((2, d // LANES, tm, LANES), F32)],
        compiler_params=_params(("arbitrary",), 48),
        name="s5_gelu_glu",
    )(x2d, g.reshape(1, d), yg, yg, d_skip.reshape(1, d), w_glu_bf16)


def _s5_matrices(lam_re, lam_im, log_dt, b_re, b_im, c_re, c_im):
    t = SSM_CHUNK
    g, p = lam_re.shape[1:]
    c = SSM_GROUP
    dt = jnp.exp(log_dt)[..., None]
    den = lam_re * lam_re + lam_im * lam_im
    mag = jnp.exp(lam_re * dt)
    ang = lam_im * dt
    lb_re = mag * jnp.cos(ang)
    lb_im = mag * jnp.sin(ang)
    nr = lb_re - 1.0
    f_re = (nr * lam_re + lb_im * lam_im) / den
    f_im = (lb_im * lam_re - nr * lam_im) / den
    bb_re = f_re[..., None] * b_re - f_im[..., None] * b_im
    bb_im = f_re[..., None] * b_im + f_im[..., None] * b_re
    lag = jnp.arange(t + 1, dtype=F32)[:, None, None, None]
    pmag = jnp.exp(lag * (lam_re * dt))
    pw_re = pmag * jnp.cos(lag * ang)
    pw_im = pmag * jnp.sin(lag * ang)
    cl_re = c_re[None] * pw_re[..., None, :] - c_im[None] * pw_im[..., None, :]
    cl_im = c_re[None] * pw_im[..., None, :] + c_im[None] * pw_re[..., None, :]
    kern = (jnp.einsum('ldgcp,dgpe->ldgce', cl_re[:t], bb_re)
            - jnp.einsum('ldgcp,dgpe->ldgce', cl_im[:t], bb_im))
    tt = jnp.arange(t)
    by_lag = jnp.concatenate([kern[:0:-1, 1], (kern[0, 0] + kern[0, 1])[None], kern[1:, 0]], axis=0)
    toeplitz = jnp.stack([by_lag[t - 1 - s:2 * t - 1 - s] for s in range(t)], axis=0)
    mmat = jnp.transpose(toeplitz, (2, 0, 4, 1, 3)).reshape(g, t * c, t * c)
    wf_re = pw_re[t - 1 - tt, 0][..., None] * bb_re[0][None]
    wf_im = pw_im[t - 1 - tt, 0][..., None] * bb_re[0][None]
    wf_re = wf_re - pw_im[t - 1 - tt, 0][..., None] * bb_im[0][None]
    wf_im = wf_im + pw_re[t - 1 - tt, 0][..., None] * bb_im[0][None]
    wb_re = pw_re[tt, 1][..., None] * bb_re[1][None] - pw_im[tt, 1][..., None] * bb_im[1][None]
    wb_im = pw_im[tt, 1][..., None] * bb_re[1][None] + pw_re[tt, 1][..., None] * bb_im[1][None]
    wmat = jnp.concatenate([wf_re, wb_re, wf_im, wb_im], axis=2)
    wmat = jnp.transpose(wmat, (1, 0, 3, 2)).reshape(g, t * c, 4 * p)
    vf_re = cl_re[tt + 1, 0]
    vf_im = -cl_im[tt + 1, 0]
    vb_re = cl_re[t - tt, 1]
    vb_im = -cl_im[t - tt, 1]
    vmat = jnp.concatenate([vf_re, vb_re, vf_im, vb_im], axis=3)
    vmat = jnp.transpose(vmat, (1, 3, 0, 2)).reshape(g, 4 * p, t * c)
    a_re = jnp.concatenate([pw_re[t, 0], pw_re[t, 1]], axis=-1)
    a_im = jnp.concatenate([pw_im[t, 0], pw_im[t, 1]], axis=-1)
    amat = jnp.zeros((g, 8, 2 * p), F32).at[:, 0].set(a_re).at[:, 1].set(a_im)
    return mmat.astype(BF16), wmat.astype(BF16), vmat.astype(BF16), amat


def _s5_mixer(x, mix_g, lam_re, lam_im, log_dt, b_re, b_im, c_re, c_im, d_skip, w_glu):
    b, s, d = x.shape
    t = SSM_CHUNK
    g = d // SSM_GROUP
    nc = s // t
    x2d = x.reshape(b * s, d)
    ug = _norm_flatten(x2d, mix_g)
    mmat, wmat, vmat, amat = _s5_matrices(lam_re, lam_im, log_dt, b_re, b_im, c_re, c_im)
    yg = _s5_core(ug, mmat, wmat, vmat, amat, nc)
    out = _s5_post(x2d, mix_g, yg, d_skip, w_glu.astype(BF16))
    return out.reshape(b, s, d)


def kernel(x, mix_norm_even, w_in, conv_w, conv_b, conv_ln_g, conv_ln_b, q_norm, k_norm, w_out, mix_norm_odd, ssm_lam_re, ssm_lam_im, ssm_log_dt, ssm_b_re, ssm_b_im, ssm_c_re, ssm_c_im, ssm_d, w_glu, ffn_norm, w_router, b_router, w_e_gate, w_e_up, w_e_down):
    depth = ffn_norm.shape[0]
    for layer in range(depth):
        i = layer // 2
        if layer % 2 == 0:
            x = _even_mixer(x, mix_norm_even[i], w_in[i], conv_w[i], conv_b[i], conv_ln_g[i],
                            conv_ln_b[i], q_norm[i], k_norm[i], w_out[i])
        else:
            x = _s5_mixer(x, mix_norm_odd[i], ssm_lam_re[i], ssm_lam_im[i], ssm_log_dt[i],
                          ssm_b_re[i], ssm_b_im[i], ssm_c_re[i], ssm_c_im[i], ssm_d[i], w_glu[i])
        x = _moe(x, ffn_norm[layer], w_router[layer], b_router[layer],
                 w_e_gate, w_e_up, w_e_down, layer)
    return x
```

```python
import functools
import math

import jax
import jax.numpy as jnp
from jax import lax
from jax.experimental import pallas as pl
from jax.experimental.pallas import tpu as pltpu

F32 = jnp.float32
BF16 = jnp.bfloat16
I32 = jnp.int32

EPS = 1e-6
NEG_INF = -1e30
LOG2_E = math.log2(math.e)

CONV_CH = 512
CONV_WIDTH = 31
ATT_HEADS = 8
HEAD_DIM = 64
ATT_WIDTH = ATT_HEADS * HEAD_DIM
DILATED = ((128, 1), (512, 4), (2048, 16))
SSM_GROUP = 16
SSM_STATE = 64
N_EXPERTS = 16
CAPACITY_FACTOR = 2
MOE_GATHER_TOKENS = 1024
MOE_GATHER_GROUP = 16
MOE_SCATTER_TOKENS = 256

LANES = 128
F32_MIN_EXP = -126
F32_MANTISSA_BITS = 23
MIB = 1024 * 1024


def _params(sem, vmem_mib):
    return pltpu.CompilerParams(dimension_semantics=sem, vmem_limit_bytes=vmem_mib * MIB)


def _rmsnorm(x, g):
    return x * lax.rsqrt(jnp.mean(x * x, axis=-1, keepdims=True) + EPS) * g


def _norm_matmul_kernel(x_ref, g_ref, w_ref, o_ref):
    y = _rmsnorm(x_ref[...], g_ref[...])
    o_ref[...] = jnp.dot(y.astype(BF16), w_ref[...], preferred_element_type=F32)


def _norm_matmul(x2d, g, w_bf16, tm=512):
    m, d = x2d.shape
    n = w_bf16.shape[1]
    return pl.pallas_call(
        _norm_matmul_kernel,
        grid=(m // tm,),
        in_specs=[pl.BlockSpec((tm, d), lambda i: (i, 0)),
                  pl.BlockSpec((1, d), lambda i: (0, 0)),
                  pl.BlockSpec((d, n), lambda i: (0, 0))],
        out_specs=pl.BlockSpec((tm, n), lambda i: (i, 0)),
        out_shape=jax.ShapeDtypeStruct((m, n), F32),
        compiler_params=_params(("parallel",), 48),
        name="norm_in_proj",
    )(x2d, g.reshape(1, d), w_bf16)


CONV_HALO = 16
CONV_ROWS = 256


def _conv_kernel(val_ref, gate_ref, w_ref, b_ref, o_ref, hp_ref):
    s = val_ref.shape[1]
    zeros = jnp.zeros((CONV_HALO, LANES), F32)
    hp_ref[0:CONV_HALO, :] = zeros
    hp_ref[CONV_HALO + s:2 * CONV_HALO + s, :] = zeros
    hp_ref[CONV_HALO:CONV_HALO + s, :] = val_ref[0] * jax.nn.sigmoid(gate_ref[0])
    first = CONV_HALO - CONV_WIDTH // 2
    for c in range(s // CONV_ROWS):
        r0 = c * CONV_ROWS
        acc = jnp.broadcast_to(b_ref[...], (CONV_ROWS, LANES))
        for k in range(CONV_WIDTH):
            acc = acc + w_ref[k:k + 1, :] * hp_ref[r0 + first + k:r0 + first + k + CONV_ROWS, :]
        o_ref[0, r0:r0 + CONV_ROWS, :] = acc


def _conformer_conv(proj3, conv_w, conv_b):
    b, s, _ = proj3.shape
    nct = CONV_CH // LANES
    return pl.pallas_call(
        _conv_kernel,
        grid=(b, nct),
        in_specs=[pl.BlockSpec((1, s, LANES), lambda i, j: (i, 0, j)),
                  pl.BlockSpec((1, s, LANES), lambda i, j: (i, 0, nct + j)),
                  pl.BlockSpec((CONV_WIDTH, LANES), lambda i, j: (0, j)),
                  pl.BlockSpec((1, LANES), lambda i, j: (0, j))],
        out_specs=pl.BlockSpec((1, s, LANES), lambda i, j: (i, 0, j)),
        out_shape=jax.ShapeDtypeStruct((b, s, CONV_CH), F32),
        scratch_shapes=[pltpu.VMEM((s + 2 * CONV_HALO, LANES), F32)],
        compiler_params=_params(("parallel", "parallel"), 32),
        name="glu_dwconv",
    )(proj3, proj3, conv_w, conv_b.reshape(1, CONV_CH))


ATT_QB = 128
ATT_RAD = 64
ATT_KW = ATT_QB + 2 * ATT_RAD
ATT_PREP_ROWS = 512
ATT_UNROLL = 32


def _attn_kernel(slopes_ref, q_ref, k_ref, v_ref, qg_ref, kg_ref, o_ref,
                 qs_ref, kp_ref, vp_ref, ob_ref, mb_ref, *, seq, pad):
    pair = pl.program_id(1)
    lane = lax.broadcasted_iota(I32, (1, LANES), 1)
    lo = lane < HEAD_DIM
    scale = LOG2_E / math.sqrt(HEAD_DIM)
    spare = (HEAD_DIM, 0)

    zpad = jnp.zeros((pad, LANES), F32)
    for hh in range(2):
        kpad = jnp.broadcast_to(jnp.where(lane == spare[hh], NEG_INF, 0.0), (pad, LANES))
        kp_ref[hh, 0:pad, :] = kpad
        kp_ref[hh, pad + seq:2 * pad + seq, :] = kpad
        vp_ref[hh, 0:pad, :] = zpad
        vp_ref[hh, pad + seq:2 * pad + seq, :] = zpad

    def head_rms(x, g):
        x2 = x * x
        s_lo = jnp.sum(jnp.where(lo, x2, 0.0), axis=-1, keepdims=True)
        s_hi = jnp.sum(jnp.where(lo, 0.0, x2), axis=-1, keepdims=True)
        ms = jnp.where(lo, s_lo, s_hi) * (1.0 / HEAD_DIM)
        return x * lax.rsqrt(ms + EPS) * g

    def prep(i, carry):
        r = pl.multiple_of(i * ATT_PREP_ROWS, ATT_PREP_ROWS)
        rows = pl.ds(r, ATT_PREP_ROWS)
        prows = pl.ds(pad + r, ATT_PREP_ROWS)
        qn = head_rms(q_ref[0, rows, :], qg_ref[...]) * scale
        qs_ref[0, rows, :] = jnp.where(lo, qn, jnp.where(lane == spare[0], 1.0, 0.0))
        qs_ref[1, rows, :] = jnp.where(lo, jnp.where(lane == spare[1], 1.0, 0.0), qn)
        kn = head_rms(k_ref[0, rows, :], kg_ref[...])
        kp_ref[0, prows, :] = jnp.where(lo, kn, 0.0)
        kp_ref[1, prows, :] = jnp.where(lo, 0.0, kn)
        v = v_ref[0, rows, :]
        vp_ref[0, prows, :] = jnp.where(lo, v, jnp.where(lane == spare[0], 1.0, 0.0))
        vp_ref[1, prows, :] = jnp.where(lo, jnp.where(lane == spare[1], 1.0, 0.0), v)
        return carry

    lax.fori_loop(0, seq // ATT_PREP_ROWS, prep, 0)

    rowi = lax.broadcasted_iota(I32, (ATT_QB, ATT_KW), 0)
    coli = lax.broadcasted_iota(I32, (ATT_QB, ATT_KW), 1)
    arel = jnp.abs(coli - ATT_RAD - rowi)

    for hh in range(2):
        slope = slopes_ref[pair * 2 + hh] * LOG2_E
        for bi, (window, dil) in enumerate(DILATED):
            assert window // (2 * dil) == ATT_RAD
            nbc = seq // dil // ATT_QB
            bias = jnp.where(arel <= ATT_RAD, -(slope * dil) * arel.astype(F32), NEG_INF)

            def blk(j, carry, bi=bi, dil=dil, nbc=nbc, bias=bias, hh=hh):
                c = j // nbc
                n = j - c * nbc
                qstart = c + dil * ATT_QB * n
                kstart = pad + c + dil * (ATT_QB * n - ATT_RAD)
                if dil == 1:
                    qsl = pl.ds(qstart, ATT_QB)
                    ksl = pl.ds(kstart, ATT_KW)
                else:
                    qsl = pl.ds(qstart, ATT_QB, stride=dil)
                    ksl = pl.ds(kstart, ATT_KW, stride=dil)
                qb = qs_ref[hh, qsl, :].astype(BF16)
                kw = kp_ref[hh, ksl, :].astype(BF16)
                vw = vp_ref[hh, ksl, :].astype(BF16)
                sc = lax.dot_general(qb, kw, (((1,), (1,)), ((), ())),
                                     preferred_element_type=F32) + bias
                m = jnp.max(sc, axis=-1, keepdims=True)
                p = jnp.exp2(sc - m)
                ob_ref[bi, qsl, :] = jnp.dot(p.astype(BF16), vw, preferred_element_type=F32)
                mb_ref[bi, qsl, :] = jnp.broadcast_to(m, (ATT_QB, LANES))
                return carry

            lax.fori_loop(0, seq // ATT_QB, blk, 0, unroll=ATT_UNROLL)

        def combine(i, carry, hh=hh):
            r = pl.multiple_of(i * ATT_PREP_ROWS, ATT_PREP_ROWS)
            rows = pl.ds(r, ATT_PREP_ROWS)
            nbr = len(DILATED)
            ms = [mb_ref[bi, rows, :] for bi in range(nbr)]
            mx = functools.reduce(jnp.maximum, ms)
            acc = sum(jnp.exp2(ms[bi] - mx) * ob_ref[bi, rows, :] for bi in range(nbr))
            den = jnp.sum(jnp.where(lane == spare[hh], acc, 0.0), axis=-1, keepdims=True)
            res = acc / den
            if hh == 0:
                o_ref[0, rows, :] = res
            else:
                o_ref[0, rows, :] = jnp.where(lo, o_ref[0, rows, :], res)
            return carry

        lax.fori_loop(0, seq // ATT_PREP_ROWS, combine, 0)


def _dilated_attention(proj3, q_norm, k_norm):
    b, s, _ = proj3.shape
    pad = ATT_RAD * max(d for _, d in DILATED)
    npair = ATT_HEADS // 2
    qoff = 2 * CONV_CH // LANES
    slopes = jnp.asarray([2.0 ** (-8.0 * (h + 1) / ATT_HEADS) for h in range(ATT_HEADS)], F32)
    qg = jnp.tile(q_norm.reshape(1, HEAD_DIM), (1, 2))
    kg = jnp.tile(k_norm.reshape(1, HEAD_DIM), (1, 2))
    kernel = functools.partial(_attn_kernel, seq=s, pad=pad)
    return pl.pallas_call(
        kernel,
        grid=(b, npair),
        in_specs=[pl.BlockSpec(memory_space=pltpu.SMEM),
                  pl.BlockSpec((1, s, LANES), lambda i, j: (i, 0, qoff + j)),
                  pl.BlockSpec((1, s, LANES), lambda i, j: (i, 0, qoff + npair + j)),
                  pl.BlockSpec((1, s, LANES), lambda i, j: (i, 0, qoff + 2 * npair + j)),
                  pl.BlockSpec((1, LANES), lambda i, j: (0, 0)),
                  pl.BlockSpec((1, LANES), lambda i, j: (0, 0))],
        out_specs=pl.BlockSpec((1, s, LANES), lambda i, j: (i, 0, j)),
        out_shape=jax.ShapeDtypeStruct((b, s, ATT_WIDTH), F32),
        scratch_shapes=[pltpu.VMEM((2, s, LANES), F32),
                        pltpu.VMEM((2, s + 2 * pad, LANES), F32),
                        pltpu.VMEM((2, s + 2 * pad, LANES), F32),
                        pltpu.VMEM((len(DILATED), s, LANES), F32),
                        pltpu.VMEM((len(DILATED), s, LANES), F32)],
        compiler_params=_params(("parallel", "parallel"), 56),
        name="dilated_attn",
    )(slopes, proj3, proj3, proj3, qg, kg)


def _outproj_kernel(x_ref, conv_ref, att_ref, g_ref, b_ref, w1_ref, w2_ref, o_ref):
    c = conv_ref[...]
    mu = jnp.mean(c, axis=-1, keepdims=True)
    cc = c - mu
    var = jnp.mean(cc * cc, axis=-1, keepdims=True)
    hn = cc * lax.rsqrt(var + EPS) * g_ref[...] + b_ref[...]
    a = hn * jax.nn.sigmoid(hn)
    y = jnp.dot(a.astype(BF16), w1_ref[...], preferred_element_type=F32)
    y = y + jnp.dot(att_ref[...].astype(BF16), w2_ref[...], preferred_element_type=F32)
    o_ref[...] = x_ref[...] + y


def _out_projection(x2d, conv2d, att2d, ln_g, ln_b, w_out_bf16, tm=512):
    m, d = x2d.shape
    row = lambda i: (i, 0)
    fixed = lambda i: (0, 0)
    return pl.pallas_call(
        _outproj_kernel,
        grid=(m // tm,),
        in_specs=[pl.BlockSpec((tm, d), row),
                  pl.BlockSpec((tm, CONV_CH), row),
                  pl.BlockSpec((tm, ATT_WIDTH), row),
                  pl.BlockSpec((1, CONV_CH), fixed),
                  pl.BlockSpec((1, CONV_CH), fixed),
                  pl.BlockSpec((CONV_CH, d), fixed),
                  pl.BlockSpec((ATT_WIDTH, d), lambda i: (1, 0))],
        out_specs=pl.BlockSpec((tm, d), row),
        out_shape=jax.ShapeDtypeStruct((m, d), F32),
        compiler_params=_params(("parallel",), 32),
        name="ln_out_proj",
    )(x2d, conv2d, att2d, ln_g.reshape(1, CONV_CH), ln_b.reshape(1, CONV_CH), w_out_bf16, w_out_bf16)


def _even_mixer(x, mix_g, w_in, conv_w, conv_b, ln_g, ln_b, q_norm, k_norm, w_out):
    b, s, d = x.shape
    x2d = x.reshape(b * s, d)
    proj = _norm_matmul(x2d, mix_g, w_in.astype(BF16))
    proj3 = proj.reshape(b, s, -1)
    conv = _conformer_conv(proj3, conv_w, conv_b)
    att = _dilated_attention(proj3, q_norm, k_norm)
    out = _out_projection(x2d, conv.reshape(b * s, CONV_CH), att.reshape(b * s, ATT_WIDTH),
                          ln_g, ln_b, w_out.astype(BF16))
    return out.reshape(b, s, d)


def _router_kernel(x_ref, g_ref, wr_ref, br_ref, h_ref, aff_ref):
    h = _rmsnorm(x_ref[0], g_ref[...])
    logits = jnp.dot(h, wr_ref[...], precision=lax.Precision.HIGHEST,
                     preferred_element_type=F32) + br_ref[...]
    lt = logits.T[:N_EXPERTS, :]
    mx = jnp.max(lt, axis=0, keepdims=True)
    ex = jnp.exp(lt - mx)
    aff_ref[0] = ex / jnp.sum(ex, axis=0, keepdims=True)
    h_ref[0] = h


def _router(x, g, w_router, b_router, tm=512):
    b, s, d = x.shape
    wr = jnp.zeros((d, LANES), F32).at[:, :N_EXPERTS].set(w_router)
    br = jnp.zeros((1, LANES), F32).at[0, :N_EXPERTS].set(b_router)
    return pl.pallas_call(
        _router_kernel,
        grid=(b, s // tm),
        in_specs=[pl.BlockSpec((1, tm, d), lambda i, j: (i, j, 0)),
                  pl.BlockSpec((1, d), lambda i, j: (0, 0)),
                  pl.BlockSpec((d, LANES), lambda i, j: (0, 0)),
                  pl.BlockSpec((1, LANES), lambda i, j: (0, 0))],
        out_specs=[pl.BlockSpec((1, tm, d), lambda i, j: (i, j, 0)),
                   pl.BlockSpec((1, N_EXPERTS, tm), lambda i, j: (i, 0, j))],
        out_shape=[jax.ShapeDtypeStruct((b, s, d), F32),
                   jax.ShapeDtypeStruct((b, N_EXPERTS, s), F32)],
        compiler_params=_params(("parallel", "parallel"), 32),
        name="moe_router",
    )(x, g.reshape(1, d), wr, br)


def _select_kernel(aff_ref, post_ref, starts_ref, idx_ref, gatet_ref, cs_ref, *, cap):
    a = aff_ref[0]
    s = a.shape[1]
    capf = float(cap)

    def count(mask):
        return jnp.sum(jnp.where(mask, 1.0, 0.0), axis=-1, keepdims=True)

    def enough(cand):
        return count(a >= cand) >= capf

    tiny = float(2.0 ** F32_MIN_EXP)
    thr = jnp.full((N_EXPERTS, 1), tiny, F32)
    normal = enough(thr)
    for j in range(6, -1, -1):
        cand = thr * float(2.0 ** (2 ** j))
        thr = jnp.where(enough(cand), cand, thr)
    step = thr * 0.5
    for _ in range(F32_MANTISSA_BITS):
        cand = thr + step
        thr = jnp.where(enough(cand), cand, thr)
        step = step * 0.5
    thr = jnp.where(normal, thr, 0.0)

    ri = lax.broadcasted_iota(I32, (LANES, LANES), 0)
    ci = lax.broadcasted_iota(I32, (LANES, LANES), 1)
    tri = jnp.where(ri < ci, 1.0, 0.0).astype(BF16)

    lane = lax.broadcasted_iota(I32, (1, LANES), 1)
    chunks_per_tile = MOE_SCATTER_TOKENS // LANES

    def exclusive_cumsum(mask):
        ind = jnp.where(mask, 1.0, 0.0)
        off = jnp.zeros((N_EXPERTS, 1), F32)
        starts = jnp.zeros((N_EXPERTS, LANES), F32)
        for j in range(s // LANES):
            if j % chunks_per_tile == 0:
                starts = jnp.where(lane == j // chunks_per_tile, off, starts)
            xc = ind[:, j * LANES:(j + 1) * LANES]
            cs_ref[:, j * LANES:(j + 1) * LANES] = (
                jnp.dot(xc.astype(BF16), tri, preferred_element_type=F32) + off)
            off = off + jnp.sum(xc, axis=-1, keepdims=True)
        starts = jnp.where(lane == s // MOE_SCATTER_TOKENS, off, starts)
        return cs_ref[...], starts

    gt = a > thr
    eq = a == thr
    need = capf - count(gt)
    tie_rank, _ = exclusive_cumsum(eq)
    sel = gt | (eq & (tie_rank < need))
    slot, starts = exclusive_cumsum(sel)
    slot = jnp.where(sel, slot, -1.0)
    starts_ref[0] = starts.astype(I32)
    fill = jnp.zeros((LANES - N_EXPERTS, s), F32)
    post_ref[0] = jnp.concatenate([slot, fill], axis=0).T

    radix = float(s)
    tok = lax.broadcasted_iota(I32, (N_EXPERTS, s), 1).astype(F32)
    packed = jnp.where(sel, (tok - slot) * radix + tok, -1.0)
    gate = jnp.where(sel, a, 0.0)
    for k in range(s.bit_length() - 1):
        dist = jnp.floor(packed * (1.0 / radix))
        bit = jnp.floor(dist * 0.5 ** k) - 2.0 * jnp.floor(dist * 0.5 ** (k + 1))
        moving = jnp.where(packed >= 0.0, bit, 0.0) == 1.0
        arrived = pltpu.roll(jnp.where(moving, packed, -1.0), s - 2 ** k, axis=1)
        arrived_gate = pltpu.roll(jnp.where(moving, gate, 0.0), s - 2 ** k, axis=1)
        gate = jnp.where(arrived >= 0.0, arrived_gate, jnp.where(moving, 0.0, gate))
        packed = jnp.where(arrived >= 0.0, arrived, jnp.where(moving, -1.0, packed))
    head = packed[:, :cap]
    idx_ref[0] = (head - radix * jnp.floor(head * (1.0 / radix))).astype(I32)
    gatet_ref[0] = jnp.concatenate([gate[:, :cap], jnp.zeros((LANES - N_EXPERTS, cap), F32)], axis=0).T


def _select(aff, cap):
    b, e, s = aff.shape
    assert s // MOE_SCATTER_TOKENS < LANES
    assert s & (s - 1) == 0 and s * s <= 2 ** 24
    blk = pl.BlockSpec((1, e, s), lambda i: (i, 0, 0))
    return pl.pallas_call(
        functools.partial(_select_kernel, cap=cap),
        grid=(b,),
        in_specs=[blk],
        out_specs=[pl.BlockSpec((1, s, LANES), lambda i: (i, 0, 0)),
                   pl.BlockSpec((1, e, LANES), lambda i: (i, 0, 0)),
                   pl.BlockSpec((1, e, cap), lambda i: (i, 0, 0)),
                   pl.BlockSpec((1, cap, LANES), lambda i: (i, 0, 0))],
        out_shape=[jax.ShapeDtypeStruct((b, s, LANES), F32),
                   jax.ShapeDtypeStruct((b, e, LANES), I32),
                   jax.ShapeDtypeStruct((b, e, cap), I32),
                   jax.ShapeDtypeStruct((b, cap, LANES), F32)],
        scratch_shapes=[pltpu.VMEM((e, s), F32)],
        compiler_params=_params(("parallel",), 32),
        name="moe_select",
    )(aff)


def _gather_kernel(idx_ref, h_ref, x_ref, rows_scr, *, cap):
    group = rows_scr.shape[1]

    def copy_group(gidx, carry):
        base = gidx * group
        for r in range(group):
            tok = idx_ref[0, 0, 0, base + r]
            rows_scr[gidx, pl.ds(r, 1), :] = h_ref[0, pl.ds(tok, 1), :]
        return carry

    lax.fori_loop(0, cap // group, copy_group, 0)
    x_ref[0, 0] = rows_scr[...].reshape(cap, rows_scr.shape[2]).astype(BF16)


def _gather_rows(h, idx, cap):
    b, s, d = h.shape
    e = idx.shape[1]
    return pl.pallas_call(
        functools.partial(_gather_kernel, cap=cap),
        grid=(b, e),
        in_specs=[pl.BlockSpec((1, 1, 1, cap), lambda i, j: (i, j, 0, 0), memory_space=pltpu.SMEM),
                  pl.BlockSpec((1, s, d), lambda i, j: (i, 0, 0), pipeline_mode=pl.Buffered(1))],
        out_specs=pl.BlockSpec((1, 1, cap, d), lambda i, j: (i, j, 0, 0)),
        out_shape=jax.ShapeDtypeStruct((b, e, cap, d), BF16),
        scratch_shapes=[pltpu.VMEM((cap // MOE_GATHER_GROUP, MOE_GATHER_GROUP, d), F32)],
        compiler_params=_params(("parallel", "parallel"), 40),
        name="moe_gather",
    )(idx.reshape(b, e, 1, cap), h)


def _expert_kernel(x_ref, gate_ref, wg_ref, wu_ref, wd_ref, y_ref, wg_s, wu_s, wd_s):
    ei = pl.program_id(0)

    @pl.when(pl.program_id(1) == 0)
    def _():
        wg_s[...] = wg_ref[0, 0].astype(BF16)
        wu_s[...] = wu_ref[0, 0].astype(BF16)
        wd_s[...] = wd_ref[0, 0].astype(BF16)

    lane = lax.broadcasted_iota(I32, (1, LANES), 1)
    gate = jnp.sum(jnp.where(lane == ei, gate_ref[0], 0.0), axis=-1, keepdims=True)
    x = x_ref[0, 0]
    g = jnp.dot(x, wg_s[...], preferred_element_type=F32)
    u = jnp.dot(x, wu_s[...], preferred_element_type=F32)
    a = (g * jax.nn.sigmoid(g) * u).astype(BF16)
    y = jnp.dot(a, wd_s[...], preferred_element_type=F32)
    y_ref[0, 0] = (y * gate).astype(BF16)


def _experts(xin, gate_t, w_gate, w_up, w_down, layer):
    b, e, cap, d = xin.shape
    f = w_gate.shape[3]
    return pl.pallas_call(
        _expert_kernel,
        grid=(e, b),
        in_specs=[pl.BlockSpec((1, 1, cap, d), lambda j, i: (i, j, 0, 0)),
                  pl.BlockSpec((1, cap, LANES), lambda j, i: (i, 0, 0)),
                  pl.BlockSpec((1, 1, d, f), lambda j, i: (layer, j, 0, 0)),
                  pl.BlockSpec((1, 1, d, f), lambda j, i: (layer, j, 0, 0)),
                  pl.BlockSpec((1, 1, f, d), lambda j, i: (layer, j, 0, 0))],
        out_specs=pl.BlockSpec((1, 1, cap, d), lambda j, i: (i, j, 0, 0)),
        out_shape=jax.ShapeDtypeStruct((b, e, cap, d), BF16),
        scratch_shapes=[pltpu.VMEM((d, f), BF16), pltpu.VMEM((d, f), BF16), pltpu.VMEM((f, d), BF16)],
        compiler_params=_params(("parallel", "arbitrary"), 56),
        name="moe_experts",
    )(xin, gate_t, w_gate, w_up, w_down)


def _scatter_kernel(starts_ref, x_ref, y_ref, post_ref, o_ref, *, cap):
    bi = pl.program_id(0)
    ti = pl.program_id(1)
    win = MOE_SCATTER_TOKENS
    lane = lax.broadcasted_iota(I32, (1, LANES), 1)
    wslot = lax.broadcasted_iota(I32, (1, win), 1).astype(F32)
    post = post_ref[0]

    def slot_column(e):
        return jnp.sum(jnp.where(lane == e, post, 0.0), axis=-1, keepdims=True)

    def window_start(e):
        first = starts_ref[bi, e, ti]
        off = jnp.minimum(lax.shift_left(lax.shift_right_logical(first, 4), 4), cap - win)
        return pl.multiple_of(off, 16)

    acc = x_ref[0]
    for e in range(N_EXPERTS):
        off = window_start(e)
        onehot = jnp.where(slot_column(e) - off.astype(F32) == wslot, 1.0, 0.0).astype(BF16)
        acc = acc + jnp.dot(onehot, y_ref[0, e, pl.ds(off, win), :], preferred_element_type=F32)
    o_ref[0] = acc

    tail = cap - win
    for e in range(N_EXPERTS):
        covered = window_start(e) + win

        @pl.when(starts_ref[bi, e, ti + 1] > covered)
        def _(e=e, covered=covered):
            pcol = slot_column(e)
            hit = (pcol - float(tail) == wslot) & (pcol >= covered.astype(F32))
            o_ref[0] += jnp.dot(jnp.where(hit, 1.0, 0.0).astype(BF16),
                                y_ref[0, e, tail:cap, :], preferred_element_type=F32)


def _scatter(x, y, post, starts, cap):
    b, s, d = x.shape
    e = y.shape[1]
    tn = MOE_SCATTER_TOKENS
    assert cap % 16 == 0 and tn <= cap <= 2 * tn
    grid_spec = pltpu.PrefetchScalarGridSpec(
        num_scalar_prefetch=1,
        grid=(b, s // tn),
        in_specs=[pl.BlockSpec((1, tn, d), lambda i, j, st: (i, j, 0)),
                  pl.BlockSpec((1, e, cap, d), lambda i, j, st: (i, 0, 0, 0),
                               pipeline_mode=pl.Buffered(1)),
                  pl.BlockSpec((1, tn, LANES), lambda i, j, st: (i, j, 0))],
        out_specs=pl.BlockSpec((1, tn, d), lambda i, j, st: (i, j, 0)),
    )
    return pl.pallas_call(
        functools.partial(_scatter_kernel, cap=cap),
        grid_spec=grid_spec,
        out_shape=jax.ShapeDtypeStruct((b, s, d), F32),
        compiler_params=_params(("parallel", "parallel"), 48),
        name="moe_scatter",
    )(starts, x, y, post)


def _moe(x, ffn_g, w_router, b_router, w_gate, w_up, w_down, layer):
    b, s, d = x.shape
    cap = CAPACITY_FACTOR * s // N_EXPERTS
    h, aff = _router(x, ffn_g, w_router, b_router)
    post, starts, idx, gate_t = _select(aff, cap)
    xin = _gather_rows(h, idx, cap)
    y = _experts(xin, gate_t, w_gate, w_up, w_down, layer)
    return _scatter(x, y, post, starts, cap)


SSM_CHUNK = 16


SSM_SLAB_GROUPS = LANES // SSM_GROUP
SSM_CHUNK_LANES = SSM_CHUNK * SSM_GROUP
SSM_HALF_STEPS = LANES // SSM_GROUP


def _transpose_lane_groups(xs):
    lgrp = lax.broadcasted_iota(I32, (1, LANES), 1) // SSM_GROUP
    xs = list(xs)
    assert len(xs) == SSM_SLAB_GROUPS
    h = SSM_SLAB_GROUPS // 2
    while h:
        upper = (lgrp & h) != 0
        for a0 in range(SSM_SLAB_GROUPS):
            if a0 & h:
                continue
            x0, x1 = xs[a0], xs[a0 + h]
            xs[a0] = jnp.where(upper, pltpu.roll(x1, h * SSM_GROUP, axis=1), x0)
            xs[a0 + h] = jnp.where(upper, x1, pltpu.roll(x0, LANES - h * SSM_GROUP, axis=1))
        h //= 2
    return xs


def _norm_flatten_kernel(x_ref, g_ref, ug_ref, u_scr):
    tm, d = x_ref.shape
    nch = tm // SSM_CHUNK
    u = _rmsnorm(x_ref[...], g_ref[...])
    lgrp = lax.broadcasted_iota(I32, (1, LANES), 1) // SSM_GROUP
    for slab in range(d // LANES):
        u_scr[slab] = u[:, slab * LANES:(slab + 1) * LANES]
        steps = [u_scr[slab, pl.ds(t, nch, stride=SSM_CHUNK), :] for t in range(SSM_CHUNK)]
        for gi in range(SSM_SLAB_GROUPS):
            for half in range(SSM_CHUNK_LANES // LANES):
                acc = None
                for tt in range(SSM_HALF_STEPS):
                    shift = ((tt - gi) * SSM_GROUP) % LANES
                    piece = steps[half * SSM_HALF_STEPS + tt]
                    if shift:
                        piece = pltpu.roll(piece, shift, axis=1)
                    acc = piece if acc is None else jnp.where(lgrp == tt, piece, acc)
                ug_ref[slab * SSM_SLAB_GROUPS + gi, :, half * LANES:(half + 1) * LANES] = acc.astype(BF16)


def _norm_flatten(x2d, g, tm=512):
    m, d = x2d.shape
    ngrp = d // SSM_GROUP
    nch = tm // SSM_CHUNK
    return pl.pallas_call(
        _norm_flatten_kernel,
        grid=(m // tm,),
        in_specs=[pl.BlockSpec((tm, d), lambda i: (i, 0)),
                  pl.BlockSpec((1, d), lambda i: (0, 0))],
        out_specs=pl.BlockSpec((ngrp, nch, SSM_CHUNK_LANES), lambda i: (0, i, 0)),
        out_shape=jax.ShapeDtypeStruct((ngrp, m // SSM_CHUNK, SSM_CHUNK_LANES), BF16),
        scratch_shapes=[pltpu.VMEM((d // LANES, tm, LANES), F32)],
        compiler_params=_params(("parallel",), 32),
        name="s5_norm_flatten",
    )(x2d, g.reshape(1, d))


def _s5_core_kernel(u_ref, m_ref, w_ref, v_ref, a_ref, y_ref, *, nc):
    u = u_ref[0]
    n_rows = u.shape[0]
    y = jnp.dot(u, m_ref[0], preferred_element_type=F32)
    x = jnp.dot(u, w_ref[0], preferred_element_type=F32)
    hr = x[:, :LANES]
    hi = x[:, LANES:]
    ar = a_ref[0, 0:1, :]
    ai = a_ref[0, 1:2, :]
    lane = lax.broadcasted_iota(I32, (1, LANES), 1)
    fwd = lane < SSM_STATE
    rown = lax.broadcasted_iota(I32, (n_rows, 1), 0) % nc

    def shifted(val, step):
        down = jnp.where(rown >= step, pltpu.roll(val, step, axis=0), 0.0)
        up = jnp.where(rown < nc - step, pltpu.roll(val, n_rows - step, axis=0), 0.0)
        return jnp.where(fwd, down, up)

    step = 1
    while step < nc:
        pr = shifted(hr, step)
        pi = shifted(hi, step)
        hr, hi = hr + ar * pr - ai * pi, hi + ar * pi + ai * pr
        ar, ai = ar * ar - ai * ai, 2.0 * ar * ai
        step *= 2
    hin_r = shifted(hr, 1).astype(BF16)
    hin_i = shifted(hi, 1).astype(BF16)
    y = y + jnp.dot(hin_r, v_ref[0, :LANES, :], preferred_element_type=F32)
    y = y + jnp.dot(hin_i, v_ref[0, LANES:, :], preferred_element_type=F32)
    y_ref[0] = y


def _s5_core(ug, mmat, wmat, vmat, amat, nc):
    g, n, k = ug.shape
    blk = lambda r, c: pl.BlockSpec((1, r, c), lambda i: (i, 0, 0))
    return pl.pallas_call(
        functools.partial(_s5_core_kernel, nc=nc),
        grid=(g,),
        in_specs=[blk(n, k), blk(k, k), blk(k, 2 * LANES), blk(2 * LANES, k), blk(8, LANES)],
        out_specs=blk(n, k),
        out_shape=jax.ShapeDtypeStruct((g, n, k), F32),
        compiler_params=_params(("parallel",), 48),
        name="s5_core",
    )(ug, mmat, wmat, vmat, amat)


def _s5_post_kernel(x_ref, g_ref, yg_first_ref, yg_next_ref, d_ref, w_ref, o_ref, y_scr):
    step = pl.program_id(0)
    x = x_ref[...]
    tm, d = x.shape
    nch = tm // SSM_CHUNK

    def unflatten(yg_ref, slot):
        for slab in range(d // LANES):
            for half in range(SSM_CHUNK_LANES // LANES):
                grps = [yg_ref[slab * SSM_SLAB_GROUPS + gi, :, half * LANES:(half + 1) * LANES]
                        for gi in range(SSM_SLAB_GROUPS)]
                for tt, rows in enumerate(_transpose_lane_groups(grps)):
                    y_scr[slot, slab, pl.ds(half * SSM_HALF_STEPS + tt, nch, stride=SSM_CHUNK), :] = rows

    @pl.when(step == 0)
    def _():
        unflatten(yg_first_ref, 0)

    cur = step % 2
    y = jnp.concatenate([y_scr[cur, slab] for slab in range(d // LANES)], axis=1)
    unflatten(yg_next_ref, 1 - cur)
    hf = _rmsnorm(x, g_ref[...])
    z = jax.nn.gelu(y + d_ref[...] * hf)
    vg = jnp.dot(z.astype(BF16), w_ref[...], preferred_element_type=F32)
    o_ref[...] = x + vg[:, :d] * jax.nn.sigmoid(vg[:, d:])


def _s5_post(x2d, g, yg, d_skip, w_glu_bf16, tm=512):
    m, d = x2d.shape
    ngrp = d // SSM_GROUP
    nch = tm // SSM_CHUNK
    last = m // tm - 1
    row = lambda i: (i, 0)
    fixed = lambda i: (0, 0)
    return pl.pallas_call(
        _s5_post_kernel,
        grid=(m // tm,),
        in_specs=[pl.BlockSpec((tm, d), row), pl.BlockSpec((1, d), fixed),
                  pl.BlockSpec((ngrp, nch, SSM_CHUNK_LANES), lambda i: (0, 0, 0)),
                  pl.BlockSpec((ngrp, nch, SSM_CHUNK_LANES), lambda i: (0, jnp.minimum(i + 1, last), 0)),
                  pl.BlockSpec((1, d), fixed),
                  pl.BlockSpec((d, 2 * d), fixed)],
        out_specs=pl.BlockSpec((tm, d), row),
        out_shape=jax.ShapeDtypeStruct((m, d), F32),
        scratch_shapes=[pltpu.VMEM((2, d // LANES, tm, LANES), F32)],
        compiler_params=_params(("arbitrary",), 48),
        name="s5_gelu_glu",
    )(x2d, g.reshape(1, d), yg, yg, d_skip.reshape(1, d), w_glu_bf16)


def _s5_matrices(lam_re, lam_im, log_dt, b_re, b_im, c_re, c_im):
    t = SSM_CHUNK
    g, p = lam_re.shape[1:]
    c = SSM_GROUP
    dt = jnp.exp(log_dt)[..., None]
    den = lam_re * lam_re + lam_im * lam_im
    mag = jnp.exp(lam_re * dt)
    ang = lam_im * dt
    lb_re = mag * jnp.cos(ang)
    lb_im = mag * jnp.sin(ang)
    nr = lb_re - 1.0
    f_re = (nr * lam_re + lb_im * lam_im) / den
    f_im = (lb_im * lam_re - nr * lam_im) / den
    bb_re = f_re[..., None] * b_re - f_im[..., None] * b_im
    bb_im = f_re[..., None] * b_im + f_im[..., None] * b_re
    lag = jnp.arange(t + 1, dtype=F32)[:, None, None, None]
    pmag = jnp.exp(lag * (lam_re * dt))
    pw_re = pmag * jnp.cos(lag * ang)
    pw_im = pmag * jnp.sin(lag * ang)
    cl_re = c_re[None] * pw_re[..., None, :] - c_im[None] * pw_im[..., None, :]
    cl_im = c_re[None] * pw_im[..., None, :] + c_im[None] * pw_re[..., None, :]
    kern = (jnp.einsum('ldgcp,dgpe->ldgce', cl_re[:t], bb_re)
            - jnp.einsum('ldgcp,dgpe->ldgce', cl_im[:t], bb_im))
    by_lag = jnp.concatenate([kern[:0:-1, 1], (kern[0, 0] + kern[0, 1])[None], kern[1:, 0],
                              jnp.zeros_like(kern[:1, 0])], axis=0)
    rows = jnp.tile(by_lag, (t, 1, 1, 1))[:t * (2 * t - 1)].reshape(t, 2 * t - 1, g, c, c)
    toeplitz = rows[:, t - 1:]
    mmat = jnp.transpose(toeplitz, (2, 0, 4, 1, 3)).reshape(g, t * c, t * c)
    pf_re, pf_im = pw_re[t - 1::-1, 0][..., None], pw_im[t - 1::-1, 0][..., None]
    pb_re, pb_im = pw_re[:t, 1][..., None], pw_im[:t, 1][..., None]
    wf_re = pf_re * bb_re[0][None] - pf_im * bb_im[0][None]
    wf_im = pf_im * bb_re[0][None] + pf_re * bb_im[0][None]
    wb_re = pb_re * bb_re[1][None] - pb_im * bb_im[1][None]
    wb_im = pb_im * bb_re[1][None] + pb_re * bb_im[1][None]
    wmat = jnp.concatenate([wf_re, wb_re, wf_im, wb_im], axis=2)
    wmat = jnp.transpose(wmat, (1, 0, 3, 2)).reshape(g, t * c, 4 * p)
    vf_re = cl_re[1:, 0]
    vf_im = -cl_im[1:, 0]
    vb_re = cl_re[:0:-1, 1]
    vb_im = -cl_im[:0:-1, 1]
    vmat = jnp.concatenate([vf_re, vb_re, vf_im, vb_im], axis=3)
    vmat = jnp.transpose(vmat, (1, 3, 0, 2)).reshape(g, 4 * p, t * c)
    a_re = jnp.concatenate([pw_re[t, 0], pw_re[t, 1]], axis=-1)
    a_im = jnp.concatenate([pw_im[t, 0], pw_im[t, 1]], axis=-1)
    amat = jnp.zeros((g, 8, 2 * p), F32).at[:, 0].set(a_re).at[:, 1].set(a_im)
    return mmat.astype(BF16), wmat.astype(BF16), vmat.astype(BF16), amat


def _s5_mixer(x, mix_g, lam_re, lam_im, log_dt, b_re, b_im, c_re, c_im, d_skip, w_glu):
    b, s, d = x.shape
    t = SSM_CHUNK
    g = d // SSM_GROUP
    nc = s // t
    x2d = x.reshape(b * s, d)
    ug = _norm_flatten(x2d, mix_g)
    mmat, wmat, vmat, amat = _s5_matrices(lam_re, lam_im, log_dt, b_re, b_im, c_re, c_im)
    yg = _s5_core(ug, mmat, wmat, vmat, amat, nc)
    out = _s5_post(x2d, mix_g, yg, d_skip, w_glu.astype(BF16))
    return out.reshape(b, s, d)


def kernel(x, mix_norm_even, w_in, conv_w, conv_b, conv_ln_g, conv_ln_b, q_norm, k_norm, w_out, mix_norm_odd, ssm_lam_re, ssm_lam_im, ssm_log_dt, ssm_b_re, ssm_b_im, ssm_c_re, ssm_c_im, ssm_d, w_glu, ffn_norm, w_router, b_router, w_e_gate, w_e_up, w_e_down):
    depth = ffn_norm.shape[0]
    for layer in range(depth):
        i = layer // 2
        if layer % 2 == 0:
            x = _even_mixer(x, mix_norm_even[i], w_in[i], conv_w[i], conv_b[i], conv_ln_g[i],
                            conv_ln_b[i], q_norm[i], k_norm[i], w_out[i])
        else:
            x = _s5_mixer(x, mix_norm_odd[i], ssm_lam_re[i], ssm_lam_im[i], ssm_log_dt[i],
                          ssm_b_re[i], ssm_b_im[i], ssm_c_re[i], ssm_c_im[i], ssm_d[i], w_glu[i])
        x = _moe(x, ffn_norm[layer], w_router[layer], b_router[layer],
                 w_e_gate, w_e_up, w_e_down, layer)
    return x
```

```python
import functools
import math

import jax
import jax.numpy as jnp
from jax import lax
from jax.experimental import pallas as pl
from jax.experimental.pallas import tpu as pltpu

F32 = jnp.float32
BF16 = jnp.bfloat16
I32 = jnp.int32

EPS = 1e-6
NEG_INF = -1e30
LOG2_E = math.log2(math.e)

CONV_CH = 512
CONV_WIDTH = 31
ATT_HEADS = 8
HEAD_DIM = 64
ATT_WIDTH = ATT_HEADS * HEAD_DIM
DILATED = ((128, 1), (512, 4), (2048, 16))
SSM_GROUP = 16
SSM_STATE = 64
N_EXPERTS = 16
CAPACITY_FACTOR = 2
MOE_GATHER_TOKENS = 1024
MOE_GATHER_GROUP = 16
MOE_SCATTER_TOKENS = 256

LANES = 128
F32_MIN_EXP = -126
F32_MANTISSA_BITS = 23
MIB = 1024 * 1024


def _params(sem, vmem_mib):
    return pltpu.CompilerParams(dimension_semantics=sem, vmem_limit_bytes=vmem_mib * MIB)


def _rmsnorm(x, g):
    return x * lax.rsqrt(jnp.mean(x * x, axis=-1, keepdims=True) + EPS) * g


def _norm_matmul_kernel(x_ref, g_ref, w_ref, o_ref):
    y = _rmsnorm(x_ref[...], g_ref[...])
    o_ref[...] = jnp.dot(y.astype(BF16), w_ref[...], preferred_element_type=F32)


def _norm_matmul(x2d, g, w_bf16, tm=512):
    m, d = x2d.shape
    n = w_bf16.shape[1]
    return pl.pallas_call(
        _norm_matmul_kernel,
        grid=(m // tm,),
        in_specs=[pl.BlockSpec((tm, d), lambda i: (i, 0)),
                  pl.BlockSpec((1, d), lambda i: (0, 0)),
                  pl.BlockSpec((d, n), lambda i: (0, 0))],
        out_specs=pl.BlockSpec((tm, n), lambda i: (i, 0)),
        out_shape=jax.ShapeDtypeStruct((m, n), F32),
        compiler_params=_params(("parallel",), 48),
        name="norm_in_proj",
    )(x2d, g.reshape(1, d), w_bf16)


CONV_HALO = 16
CONV_ROWS = 256


def _conv_kernel(val_ref, gate_ref, w_ref, b_ref, o_ref, hp_ref):
    s = val_ref.shape[1]
    zeros = jnp.zeros((CONV_HALO, LANES), F32)
    hp_ref[0:CONV_HALO, :] = zeros
    hp_ref[CONV_HALO + s:2 * CONV_HALO + s, :] = zeros
    hp_ref[CONV_HALO:CONV_HALO + s, :] = val_ref[0] * jax.nn.sigmoid(gate_ref[0])
    first = CONV_HALO - CONV_WIDTH // 2
    for c in range(s // CONV_ROWS):
        r0 = c * CONV_ROWS
        acc = jnp.broadcast_to(b_ref[...], (CONV_ROWS, LANES))
        for k in range(CONV_WIDTH):
            acc = acc + w_ref[k:k + 1, :] * hp_ref[r0 + first + k:r0 + first + k + CONV_ROWS, :]
        o_ref[0, r0:r0 + CONV_ROWS, :] = acc


def _conformer_conv(proj3, conv_w, conv_b):
    b, s, _ = proj3.shape
    nct = CONV_CH // LANES
    return pl.pallas_call(
        _conv_kernel,
        grid=(b, nct),
        in_specs=[pl.BlockSpec((1, s, LANES), lambda i, j: (i, 0, j)),
                  pl.BlockSpec((1, s, LANES), lambda i, j: (i, 0, nct + j)),
                  pl.BlockSpec((CONV_WIDTH, LANES), lambda i, j: (0, j)),
                  pl.BlockSpec((1, LANES), lambda i, j: (0, j))],
        out_specs=pl.BlockSpec((1, s, LANES), lambda i, j: (i, 0, j)),
        out_shape=jax.ShapeDtypeStruct((b, s, CONV_CH), F32),
        scratch_shapes=[pltpu.VMEM((s + 2 * CONV_HALO, LANES), F32)],
        compiler_params=_params(("parallel", "parallel"), 32),
        name="glu_dwconv",
    )(proj3, proj3, conv_w, conv_b.reshape(1, CONV_CH))


ATT_QB = 128
ATT_RAD = 64
ATT_KW = ATT_QB + 2 * ATT_RAD
ATT_PREP_ROWS = 512
ATT_UNROLL = 32


def _attn_kernel(slopes_ref, q_ref, k_ref, v_ref, qg_ref, kg_ref, o_ref,
                 qs_ref, kp_ref, vp_ref, ob_ref, mb_ref, *, seq, pad):
    pair = pl.program_id(1)
    lane = lax.broadcasted_iota(I32, (1, LANES), 1)
    lo = lane < HEAD_DIM
    scale = LOG2_E / math.sqrt(HEAD_DIM)
    spare = (HEAD_DIM, 0)

    zpad = jnp.zeros((pad, LANES), F32)
    for hh in range(2):
        kpad = jnp.broadcast_to(jnp.where(lane == spare[hh], NEG_INF, 0.0), (pad, LANES))
        kp_ref[hh, 0:pad, :] = kpad
        kp_ref[hh, pad + seq:2 * pad + seq, :] = kpad
        vp_ref[hh, 0:pad, :] = zpad
        vp_ref[hh, pad + seq:2 * pad + seq, :] = zpad

    def head_rms(x, g):
        x2 = x * x
        s_lo = jnp.sum(jnp.where(lo, x2, 0.0), axis=-1, keepdims=True)
        s_hi = jnp.sum(jnp.where(lo, 0.0, x2), axis=-1, keepdims=True)
        ms = jnp.where(lo, s_lo, s_hi) * (1.0 / HEAD_DIM)
        return x * lax.rsqrt(ms + EPS) * g

    def prep(i, carry):
        r = pl.multiple_of(i * ATT_PREP_ROWS, ATT_PREP_ROWS)
        rows = pl.ds(r, ATT_PREP_ROWS)
        prows = pl.ds(pad + r, ATT_PREP_ROWS)
        qn = head_rms(q_ref[0, rows, :], qg_ref[...]) * scale
        qs_ref[0, rows, :] = jnp.where(lo, qn, jnp.where(lane == spare[0], 1.0, 0.0))
        qs_ref[1, rows, :] = jnp.where(lo, jnp.where(lane == spare[1], 1.0, 0.0), qn)
        kn = head_rms(k_ref[0, rows, :], kg_ref[...])
        kp_ref[0, prows, :] = jnp.where(lo, kn, 0.0)
        kp_ref[1, prows, :] = jnp.where(lo, 0.0, kn)
        v = v_ref[0, rows, :]
        vp_ref[0, prows, :] = jnp.where(lo, v, jnp.where(lane == spare[0], 1.0, 0.0))
        vp_ref[1, prows, :] = jnp.where(lo, jnp.where(lane == spare[1], 1.0, 0.0), v)
        return carry

    lax.fori_loop(0, seq // ATT_PREP_ROWS, prep, 0)

    rowi = lax.broadcasted_iota(I32, (ATT_QB, ATT_KW), 0)
    coli = lax.broadcasted_iota(I32, (ATT_QB, ATT_KW), 1)
    arel = jnp.abs(coli - ATT_RAD - rowi)

    for hh in range(2):
        slope = slopes_ref[pair * 2 + hh] * LOG2_E
        for bi, (window, dil) in enumerate(DILATED):
            assert window // (2 * dil) == ATT_RAD
            nbc = seq // dil // ATT_QB
            bias = jnp.where(arel <= ATT_RAD, -(slope * dil) * arel.astype(F32), NEG_INF)

            def blk(j, carry, bi=bi, dil=dil, nbc=nbc, bias=bias, hh=hh):
                c = j // nbc
                n = j - c * nbc
                qstart = c + dil * ATT_QB * n
                kstart = pad + c + dil * (ATT_QB * n - ATT_RAD)
                if dil == 1:
                    qsl = pl.ds(qstart, ATT_QB)
                    ksl = pl.ds(kstart, ATT_KW)
                else:
                    qsl = pl.ds(qstart, ATT_QB, stride=dil)
                    ksl = pl.ds(kstart, ATT_KW, stride=dil)
                qb = qs_ref[hh, qsl, :].astype(BF16)
                kw = kp_ref[hh, ksl, :].astype(BF16)
                vw = vp_ref[hh, ksl, :].astype(BF16)
                sc = lax.dot_general(qb, kw, (((1,), (1,)), ((), ())),
                                     preferred_element_type=F32) + bias
                m = jnp.max(sc, axis=-1, keepdims=True)
                p = jnp.exp2(sc - m)
                ob_ref[bi, qsl, :] = jnp.dot(p.astype(BF16), vw, preferred_element_type=F32)
                mb_ref[bi, qsl, :] = jnp.broadcast_to(m, (ATT_QB, LANES))
                return carry

            lax.fori_loop(0, seq // ATT_QB, blk, 0, unroll=ATT_UNROLL)

        def combine(i, carry, hh=hh):
            r = pl.multiple_of(i * ATT_PREP_ROWS, ATT_PREP_ROWS)
            rows = pl.ds(r, ATT_PREP_ROWS)
            nbr = len(DILATED)
            ms = [mb_ref[bi, rows, :] for bi in range(nbr)]
            mx = functools.reduce(jnp.maximum, ms)
            acc = sum(jnp.exp2(ms[bi] - mx) * ob_ref[bi, rows, :] for bi in range(nbr))
            den = jnp.sum(jnp.where(lane == spare[hh], acc, 0.0), axis=-1, keepdims=True)
            res = acc / den
            if hh == 0:
                o_ref[0, rows, :] = res
            else:
                o_ref[0, rows, :] = jnp.where(lo, o_ref[0, rows, :], res)
            return carry

        lax.fori_loop(0, seq // ATT_PREP_ROWS, combine, 0)


def _dilated_attention(proj3, q_norm, k_norm):
    b, s, _ = proj3.shape
    pad = ATT_RAD * max(d for _, d in DILATED)
    npair = ATT_HEADS // 2
    qoff = 2 * CONV_CH // LANES
    slopes = jnp.asarray([2.0 ** (-8.0 * (h + 1) / ATT_HEADS) for h in range(ATT_HEADS)], F32)
    qg = jnp.tile(q_norm.reshape(1, HEAD_DIM), (1, 2))
    kg = jnp.tile(k_norm.reshape(1, HEAD_DIM), (1, 2))
    kernel = functools.partial(_attn_kernel, seq=s, pad=pad)
    return pl.pallas_call(
        kernel,
        grid=(b, npair),
        in_specs=[pl.BlockSpec(memory_space=pltpu.SMEM),
                  pl.BlockSpec((1, s, LANES), lambda i, j: (i, 0, qoff + j)),
                  pl.BlockSpec((1, s, LANES), lambda i, j: (i, 0, qoff + npair + j)),
                  pl.BlockSpec((1, s, LANES), lambda i, j: (i, 0, qoff + 2 * npair + j)),
                  pl.BlockSpec((1, LANES), lambda i, j: (0, 0)),
                  pl.BlockSpec((1, LANES), lambda i, j: (0, 0))],
        out_specs=pl.BlockSpec((1, s, LANES), lambda i, j: (i, 0, j)),
        out_shape=jax.ShapeDtypeStruct((b, s, ATT_WIDTH), F32),
        scratch_shapes=[pltpu.VMEM((2, s, LANES), F32),
                        pltpu.VMEM((2, s + 2 * pad, LANES), F32),
                        pltpu.VMEM((2, s + 2 * pad, LANES), F32),
                        pltpu.VMEM((len(DILATED), s, LANES), F32),
                        pltpu.VMEM((len(DILATED), s, LANES), F32)],
        compiler_params=_params(("parallel", "parallel"), 56),
        name="dilated_attn",
    )(slopes, proj3, proj3, proj3, qg, kg)


def _outproj_kernel(x_ref, conv_ref, att_ref, g_ref, b_ref, w1_ref, w2_ref, o_ref):
    c = conv_ref[...]
    mu = jnp.mean(c, axis=-1, keepdims=True)
    cc = c - mu
    var = jnp.mean(cc * cc, axis=-1, keepdims=True)
    hn = cc * lax.rsqrt(var + EPS) * g_ref[...] + b_ref[...]
    a = hn * jax.nn.sigmoid(hn)
    y = jnp.dot(a.astype(BF16), w1_ref[...], preferred_element_type=F32)
    y = y + jnp.dot(att_ref[...].astype(BF16), w2_ref[...], preferred_element_type=F32)
    o_ref[...] = x_ref[...] + y


def _out_projection(x2d, conv2d, att2d, ln_g, ln_b, w_out_bf16, tm=512):
    m, d = x2d.shape
    row = lambda i: (i, 0)
    fixed = lambda i: (0, 0)
    return pl.pallas_call(
        _outproj_kernel,
        grid=(m // tm,),
        in_specs=[pl.BlockSpec((tm, d), row),
                  pl.BlockSpec((tm, CONV_CH), row),
                  pl.BlockSpec((tm, ATT_WIDTH), row),
                  pl.BlockSpec((1, CONV_CH), fixed),
                  pl.BlockSpec((1, CONV_CH), fixed),
                  pl.BlockSpec((CONV_CH, d), fixed),
                  pl.BlockSpec((ATT_WIDTH, d), lambda i: (1, 0))],
        out_specs=pl.BlockSpec((tm, d), row),
        out_shape=jax.ShapeDtypeStruct((m, d), F32),
        compiler_params=_params(("parallel",), 32),
        name="ln_out_proj",
    )(x2d, conv2d, att2d, ln_g.reshape(1, CONV_CH), ln_b.reshape(1, CONV_CH), w_out_bf16, w_out_bf16)


def _even_mixer(x, mix_g, w_in, conv_w, conv_b, ln_g, ln_b, q_norm, k_norm, w_out):
    b, s, d = x.shape
    x2d = x.reshape(b * s, d)
    proj = _norm_matmul(x2d, mix_g, w_in.astype(BF16))
    proj3 = proj.reshape(b, s, -1)
    conv = _conformer_conv(proj3, conv_w, conv_b)
    att = _dilated_attention(proj3, q_norm, k_norm)
    out = _out_projection(x2d, conv.reshape(b * s, CONV_CH), att.reshape(b * s, ATT_WIDTH),
                          ln_g, ln_b, w_out.astype(BF16))
    return out.reshape(b, s, d)


def _router_kernel(x_ref, g_ref, wr_ref, br_ref, h_ref, aff_ref):
    h = _rmsnorm(x_ref[0], g_ref[...])
    logits = jnp.dot(h, wr_ref[...], precision=lax.Precision.HIGHEST,
                     preferred_element_type=F32) + br_ref[...]
    lt = logits.T[:N_EXPERTS, :]
    mx = jnp.max(lt, axis=0, keepdims=True)
    ex = jnp.exp(lt - mx)
    aff_ref[0] = ex / jnp.sum(ex, axis=0, keepdims=True)
    h_ref[0] = h


def _router(x, g, w_router, b_router, tm=512):
    b, s, d = x.shape
    wr = jnp.zeros((d, LANES), F32).at[:, :N_EXPERTS].set(w_router)
    br = jnp.zeros((1, LANES), F32).at[0, :N_EXPERTS].set(b_router)
    return pl.pallas_call(
        _router_kernel,
        grid=(b, s // tm),
        in_specs=[pl.BlockSpec((1, tm, d), lambda i, j: (i, j, 0)),
                  pl.BlockSpec((1, d), lambda i, j: (0, 0)),
                  pl.BlockSpec((d, LANES), lambda i, j: (0, 0)),
                  pl.BlockSpec((1, LANES), lambda i, j: (0, 0))],
        out_specs=[pl.BlockSpec((1, tm, d), lambda i, j: (i, j, 0)),
                   pl.BlockSpec((1, N_EXPERTS, tm), lambda i, j: (i, 0, j))],
        out_shape=[jax.ShapeDtypeStruct((b, s, d), F32),
                   jax.ShapeDtypeStruct((b, N_EXPERTS, s), F32)],
        compiler_params=_params(("parallel", "parallel"), 32),
        name="moe_router",
    )(x, g.reshape(1, d), wr, br)


def _select_kernel(aff_ref, post_ref, starts_ref, idx_ref, gatet_ref, cs_ref, *, cap):
    a = aff_ref[0]
    s = a.shape[1]
    capf = float(cap)

    def count(mask):
        return jnp.sum(jnp.where(mask, 1.0, 0.0), axis=-1, keepdims=True)

    def enough(cand):
        return count(a >= cand) >= capf

    tiny = float(2.0 ** F32_MIN_EXP)
    thr = jnp.full((N_EXPERTS, 1), tiny, F32)
    normal = enough(thr)
    for j in range(6, -1, -1):
        cand = thr * float(2.0 ** (2 ** j))
        thr = jnp.where(enough(cand), cand, thr)
    step = thr * 0.5
    for _ in range(F32_MANTISSA_BITS):
        cand = thr + step
        thr = jnp.where(enough(cand), cand, thr)
        step = step * 0.5
    thr = jnp.where(normal, thr, 0.0)

    ri = lax.broadcasted_iota(I32, (LANES, LANES), 0)
    ci = lax.broadcasted_iota(I32, (LANES, LANES), 1)
    tri = jnp.where(ri < ci, 1.0, 0.0).astype(BF16)

    lane = lax.broadcasted_iota(I32, (1, LANES), 1)
    chunks_per_tile = MOE_SCATTER_TOKENS // LANES

    def exclusive_cumsum(mask):
        ind = jnp.where(mask, 1.0, 0.0)
        off = jnp.zeros((N_EXPERTS, 1), F32)
        starts = jnp.zeros((N_EXPERTS, LANES), F32)
        for j in range(s // LANES):
            if j % chunks_per_tile == 0:
                starts = jnp.where(lane == j // chunks_per_tile, off, starts)
            xc = ind[:, j * LANES:(j + 1) * LANES]
            cs_ref[:, j * LANES:(j + 1) * LANES] = (
                jnp.dot(xc.astype(BF16), tri, preferred_element_type=F32) + off)
            off = off + jnp.sum(xc, axis=-1, keepdims=True)
        starts = jnp.where(lane == s // MOE_SCATTER_TOKENS, off, starts)
        return cs_ref[...], starts

    gt = a > thr
    eq = a == thr
    need = capf - count(gt)
    tie_rank, _ = exclusive_cumsum(eq)
    sel = gt | (eq & (tie_rank < need))
    slot, starts = exclusive_cumsum(sel)
    slot = jnp.where(sel, slot, -1.0)
    starts_ref[0] = starts.astype(I32)
    fill = jnp.zeros((LANES - N_EXPERTS, s), F32)
    post_ref[0] = jnp.concatenate([slot, fill], axis=0).T

    radix = float(s)
    tok = lax.broadcasted_iota(I32, (N_EXPERTS, s), 1).astype(F32)
    packed = jnp.where(sel, (tok - slot) * radix + tok, -1.0)
    gate = jnp.where(sel, a, 0.0)
    for k in range(s.bit_length() - 1):
        dist = jnp.floor(packed * (1.0 / radix))
        bit = jnp.floor(dist * 0.5 ** k) - 2.0 * jnp.floor(dist * 0.5 ** (k + 1))
        moving = jnp.where(packed >= 0.0, bit, 0.0) == 1.0
        arrived = pltpu.roll(jnp.where(moving, packed, -1.0), s - 2 ** k, axis=1)
        arrived_gate = pltpu.roll(jnp.where(moving, gate, 0.0), s - 2 ** k, axis=1)
        gate = jnp.where(arrived >= 0.0, arrived_gate, jnp.where(moving, 0.0, gate))
        packed = jnp.where(arrived >= 0.0, arrived, jnp.where(moving, -1.0, packed))
    head = packed[:, :cap]
    idx_ref[0] = (head - radix * jnp.floor(head * (1.0 / radix))).astype(I32)
    gatet_ref[0] = jnp.concatenate([gate[:, :cap], jnp.zeros((LANES - N_EXPERTS, cap), F32)], axis=0).T


def _select(aff, cap):
    b, e, s = aff.shape
    assert s // MOE_SCATTER_TOKENS < LANES
    assert s & (s - 1) == 0 and s * s <= 2 ** 24
    blk = pl.BlockSpec((1, e, s), lambda i: (i, 0, 0))
    return pl.pallas_call(
        functools.partial(_select_kernel, cap=cap),
        grid=(b,),
        in_specs=[blk],
        out_specs=[pl.BlockSpec((1, s, LANES), lambda i: (i, 0, 0)),
                   pl.BlockSpec((1, e, LANES), lambda i: (i, 0, 0)),
                   pl.BlockSpec((1, e, cap), lambda i: (i, 0, 0)),
                   pl.BlockSpec((1, cap, LANES), lambda i: (i, 0, 0))],
        out_shape=[jax.ShapeDtypeStruct((b, s, LANES), F32),
                   jax.ShapeDtypeStruct((b, e, LANES), I32),
                   jax.ShapeDtypeStruct((b, e, cap), I32),
                   jax.ShapeDtypeStruct((b, cap, LANES), F32)],
        scratch_shapes=[pltpu.VMEM((e, s), F32)],
        compiler_params=_params(("parallel",), 32),
        name="moe_select",
    )(aff)


def _gather_kernel(idx_ref, h_ref, x_ref, rows_scr, *, cap):
    group = rows_scr.shape[1]

    def copy_group(gidx, carry):
        base = gidx * group
        for r in range(group):
            tok = idx_ref[0, 0, 0, base + r]
            rows_scr[gidx, pl.ds(r, 1), :] = h_ref[0, pl.ds(tok, 1), :]
        return carry

    lax.fori_loop(0, cap // group, copy_group, 0)
    x_ref[0, 0] = rows_scr[...].reshape(cap, rows_scr.shape[2]).astype(BF16)


def _gather_rows(h, idx, cap):
    b, s, d = h.shape
    e = idx.shape[1]
    return pl.pallas_call(
        functools.partial(_gather_kernel, cap=cap),
        grid=(b, e),
        in_specs=[pl.BlockSpec((1, 1, 1, cap), lambda i, j: (i, j, 0, 0), memory_space=pltpu.SMEM),
                  pl.BlockSpec((1, s, d), lambda i, j: (i, 0, 0), pipeline_mode=pl.Buffered(1))],
        out_specs=pl.BlockSpec((1, 1, cap, d), lambda i, j: (i, j, 0, 0)),
        out_shape=jax.ShapeDtypeStruct((b, e, cap, d), BF16),
        scratch_shapes=[pltpu.VMEM((cap // MOE_GATHER_GROUP, MOE_GATHER_GROUP, d), F32)],
        compiler_params=_params(("parallel", "parallel"), 40),
        name="moe_gather",
    )(idx.reshape(b, e, 1, cap), h)


def _expert_kernel(x_ref, gate_ref, wg_ref, wu_ref, wd_ref, y_ref, wg_s, wu_s, wd_s):
    ei = pl.program_id(0)

    @pl.when(pl.program_id(1) == 0)
    def _():
        wg_s[...] = wg_ref[0, 0].astype(BF16)
        wu_s[...] = wu_ref[0, 0].astype(BF16)
        wd_s[...] = wd_ref[0, 0].astype(BF16)

    lane = lax.broadcasted_iota(I32, (1, LANES), 1)
    gate = jnp.sum(jnp.where(lane == ei, gate_ref[0], 0.0), axis=-1, keepdims=True)
    x = x_ref[0, 0]
    g = jnp.dot(x, wg_s[...], preferred_element_type=F32)
    u = jnp.dot(x, wu_s[...], preferred_element_type=F32)
    a = (g * jax.nn.sigmoid(g) * u).astype(BF16)
    y = jnp.dot(a, wd_s[...], preferred_element_type=F32)
    y_ref[0, 0] = (y * gate).astype(BF16)


def _experts(xin, gate_t, w_gate, w_up, w_down, layer):
    b, e, cap, d = xin.shape
    f = w_gate.shape[3]
    return pl.pallas_call(
        _expert_kernel,
        grid=(e, b),
        in_specs=[pl.BlockSpec((1, 1, cap, d), lambda j, i: (i, j, 0, 0)),
                  pl.BlockSpec((1, cap, LANES), lambda j, i: (i, 0, 0)),
                  pl.BlockSpec((1, 1, d, f), lambda j, i: (layer, j, 0, 0)),
                  pl.BlockSpec((1, 1, d, f), lambda j, i: (layer, j, 0, 0)),
                  pl.BlockSpec((1, 1, f, d), lambda j, i: (layer, j, 0, 0))],
        out_specs=pl.BlockSpec((1, 1, cap, d), lambda j, i: (i, j, 0, 0)),
        out_shape=jax.ShapeDtypeStruct((b, e, cap, d), BF16),
        scratch_shapes=[pltpu.VMEM((d, f), BF16), pltpu.VMEM((d, f), BF16), pltpu.VMEM((f, d), BF16)],
        compiler_params=_params(("parallel", "arbitrary"), 56),
        name="moe_experts",
    )(xin, gate_t, w_gate, w_up, w_down)


def _scatter_kernel(starts_ref, x_ref, y_ref, post_ref, o_ref, *, cap):
    bi = pl.program_id(0)
    ti = pl.program_id(1)
    win = MOE_SCATTER_TOKENS
    lane = lax.broadcasted_iota(I32, (1, LANES), 1)
    wslot = lax.broadcasted_iota(I32, (1, win), 1).astype(F32)
    post = post_ref[0]

    def slot_column(e):
        return jnp.sum(jnp.where(lane == e, post, 0.0), axis=-1, keepdims=True)

    def window_start(e):
        first = starts_ref[bi, e, ti]
        off = jnp.minimum(lax.shift_left(lax.shift_right_logical(first, 4), 4), cap - win)
        return pl.multiple_of(off, 16)

    acc = x_ref[0]
    for e in range(N_EXPERTS):
        off = window_start(e)
        onehot = jnp.where(slot_column(e) - off.astype(F32) == wslot, 1.0, 0.0).astype(BF16)
        acc = acc + jnp.dot(onehot, y_ref[0, e, pl.ds(off, win), :], preferred_element_type=F32)
    o_ref[0] = acc

    tail = cap - win
    for e in range(N_EXPERTS):
        covered = window_start(e) + win

        @pl.when(starts_ref[bi, e, ti + 1] > covered)
        def _(e=e, covered=covered):
            pcol = slot_column(e)
            hit = (pcol - float(tail) == wslot) & (pcol >= covered.astype(F32))
            o_ref[0] += jnp.dot(jnp.where(hit, 1.0, 0.0).astype(BF16),
                                y_ref[0, e, tail:cap, :], preferred_element_type=F32)


def _scatter(x, y, post, starts, cap):
    b, s, d = x.shape
    e = y.shape[1]
    tn = MOE_SCATTER_TOKENS
    assert cap % 16 == 0 and tn <= cap <= 2 * tn
    grid_spec = pltpu.PrefetchScalarGridSpec(
        num_scalar_prefetch=1,
        grid=(b, s // tn),
        in_specs=[pl.BlockSpec((1, tn, d), lambda i, j, st: (i, j, 0)),
                  pl.BlockSpec((1, e, cap, d), lambda i, j, st: (i, 0, 0, 0),
                               pipeline_mode=pl.Buffered(1)),
                  pl.BlockSpec((1, tn, LANES), lambda i, j, st: (i, j, 0))],
        out_specs=pl.BlockSpec((1, tn, d), lambda i, j, st: (i, j, 0)),
    )
    return pl.pallas_call(
        functools.partial(_scatter_kernel, cap=cap),
        grid_spec=grid_spec,
        out_shape=jax.ShapeDtypeStruct((b, s, d), F32),
        compiler_params=_params(("parallel", "parallel"), 48),
        name="moe_scatter",
    )(starts, x, y, post)


def _moe(x, ffn_g, w_router, b_router, w_gate, w_up, w_down, layer):
    b, s, d = x.shape
    cap = CAPACITY_FACTOR * s // N_EXPERTS
    h, aff = _router(x, ffn_g, w_router, b_router)
    post, starts, idx, gate_t = _select(aff, cap)
    xin = _gather_rows(h, idx, cap)
    y = _experts(xin, gate_t, w_gate, w_up, w_down, layer)
    return _scatter(x, y, post, starts, cap)


SSM_CHUNK = 16


SSM_SLAB_GROUPS = LANES // SSM_GROUP
SSM_CHUNK_LANES = SSM_CHUNK * SSM_GROUP
SSM_HALF_STEPS = LANES // SSM_GROUP


def _transpose_lane_groups(xs):
    lgrp = lax.broadcasted_iota(I32, (1, LANES), 1) // SSM_GROUP
    xs = list(xs)
    assert len(xs) == SSM_SLAB_GROUPS
    h = SSM_SLAB_GROUPS // 2
    while h:
        upper = (lgrp & h) != 0
        for a0 in range(SSM_SLAB_GROUPS):
            if a0 & h:
                continue
            x0, x1 = xs[a0], xs[a0 + h]
            xs[a0] = jnp.where(upper, pltpu.roll(x1, h * SSM_GROUP, axis=1), x0)
            xs[a0 + h] = jnp.where(upper, x1, pltpu.roll(x0, LANES - h * SSM_GROUP, axis=1))
        h //= 2
    return xs


def _norm_flatten_kernel(x_ref, g_ref, ug_ref, u_scr):
    tm, d = x_ref.shape
    nch = tm // SSM_CHUNK
    u = _rmsnorm(x_ref[...], g_ref[...])
    lgrp = lax.broadcasted_iota(I32, (1, LANES), 1) // SSM_GROUP
    for slab in range(d // LANES):
        u_scr[slab] = u[:, slab * LANES:(slab + 1) * LANES]
        steps = [u_scr[slab, pl.ds(t, nch, stride=SSM_CHUNK), :] for t in range(SSM_CHUNK)]
        for gi in range(SSM_SLAB_GROUPS):
            for half in range(SSM_CHUNK_LANES // LANES):
                acc = None
                for tt in range(SSM_HALF_STEPS):
                    shift = ((tt - gi) * SSM_GROUP) % LANES
                    piece = steps[half * SSM_HALF_STEPS + tt]
                    if shift:
                        piece = pltpu.roll(piece, shift, axis=1)
                    acc = piece if acc is None else jnp.where(lgrp == tt, piece, acc)
                ug_ref[slab * SSM_SLAB_GROUPS + gi, :, half * LANES:(half + 1) * LANES] = acc.astype(BF16)


def _norm_flatten(x2d, g, tm=512):
    m, d = x2d.shape
    ngrp = d // SSM_GROUP
    nch = tm // SSM_CHUNK
    return pl.pallas_call(
        _norm_flatten_kernel,
        grid=(m // tm,),
        in_specs=[pl.BlockSpec((tm, d), lambda i: (i, 0)),
                  pl.BlockSpec((1, d), lambda i: (0, 0))],
        out_specs=pl.BlockSpec((ngrp, nch, SSM_CHUNK_LANES), lambda i: (0, i, 0)),
        out_shape=jax.ShapeDtypeStruct((ngrp, m // SSM_CHUNK, SSM_CHUNK_LANES), BF16),
        scratch_shapes=[pltpu.VMEM((d // LANES, tm, LANES), F32)],
        compiler_params=_params(("parallel",), 32),
        name="s5_norm_flatten",
    )(x2d, g.reshape(1, d))


def _s5_core_kernel(u_ref, m_ref, w_ref, v_ref, a_ref, y_ref, *, nc):
    u = u_ref[0]
    n_rows = u.shape[0]
    y = jnp.dot(u, m_ref[0], preferred_element_type=F32)
    x = jnp.dot(u, w_ref[0], preferred_element_type=F32)
    hr = x[:, :LANES]
    hi = x[:, LANES:]
    ar = a_ref[0, 0:1, :]
    ai = a_ref[0, 1:2, :]
    lane = lax.broadcasted_iota(I32, (1, LANES), 1)
    fwd = lane < SSM_STATE
    rown = lax.broadcasted_iota(I32, (n_rows, 1), 0) % nc

    def shifted(val, step):
        down = jnp.where(rown >= step, pltpu.roll(val, step, axis=0), 0.0)
        up = jnp.where(rown < nc - step, pltpu.roll(val, n_rows - step, axis=0), 0.0)
        return jnp.where(fwd, down, up)

    step = 1
    while step < nc:
        pr = shifted(hr, step)
        pi = shifted(hi, step)
        hr, hi = hr + ar * pr - ai * pi, hi + ar * pi + ai * pr
        ar, ai = ar * ar - ai * ai, 2.0 * ar * ai
        step *= 2
    hin_r = shifted(hr, 1).astype(BF16)
    hin_i = shifted(hi, 1).astype(BF16)
    y = y + jnp.dot(hin_r, v_ref[0, :LANES, :], preferred_element_type=F32)
    y = y + jnp.dot(hin_i, v_ref[0, LANES:, :], preferred_element_type=F32)
    y_ref[0] = y


def _s5_core(ug, mmat, wmat, vmat, amat, nc):
    g, n, k = ug.shape
    blk = lambda r, c: pl.BlockSpec((1, r, c), lambda i: (i, 0, 0))
    return pl.pallas_call(
        functools.partial(_s5_core_kernel, nc=nc),
        grid=(g,),
        in_specs=[blk(n, k), blk(k, k), blk(k, 2 * LANES), blk(2 * LANES, k), blk(8, LANES)],
        out_specs=blk(n, k),
        out_shape=jax.ShapeDtypeStruct((g, n, k), F32),
        compiler_params=_params(("parallel",), 48),
        name="s5_core",
    )(ug, mmat, wmat, vmat, amat)


def _s5_post_kernel(x_ref, g_ref, yg_first_ref, yg_next_ref, d_ref, w_ref, o_ref, y_scr):
    step = pl.program_id(0)
    x = x_ref[...]
    tm, d = x.shape
    nch = tm // SSM_CHUNK

    def unflatten(yg_ref, slot):
        for slab in range(d // LANES):
            for half in range(SSM_CHUNK_LANES // LANES):
                grps = [yg_ref[slab * SSM_SLAB_GROUPS + gi, :, half * LANES:(half + 1) * LANES]
                        for gi in range(SSM_SLAB_GROUPS)]
                for tt, rows in enumerate(_transpose_lane_groups(grps)):
                    y_scr[slot, slab, pl.ds(half * SSM_HALF_STEPS + tt, nch, stride=SSM_CHUNK), :] = rows

    @pl.when(step == 0)
    def _():
        unflatten(yg_first_ref, 0)

    cur = step % 2
    y = jnp.concatenate([y_scr[cur, slab] for slab in range(d // LANES)], axis=1)
    unflatten(yg_next_ref, 1 - cur)
    hf = _rmsnorm(x, g_ref[...])
    z = jax.nn.gelu(y + d_ref[...] * hf)
    vg = jnp.dot(z.astype(BF16), w_ref[...], preferred_element_type=F32)
    o_ref[...] = x + vg[:, :d] * jax.nn.sigmoid(vg[:, d:])


def _s5_post(x2d, g, yg, d_skip, w_glu_bf16, tm=512):
    m, d = x2d.shape
    ngrp = d // SSM_GROUP
    nch = tm // SSM_CHUNK
    last = m // tm - 1
    row = lambda i: (i, 0)
    fixed = lambda i: (0, 0)
    return pl.pallas_call(
        _s5_post_kernel,
        grid=(m // tm,),
        in_specs=[pl.BlockSpec((tm, d), row), pl.BlockSpec((1, d), fixed),
                  pl.BlockSpec((ngrp, nch, SSM_CHUNK_LANES), lambda i: (0, 0, 0)),
                  pl.BlockSpec((ngrp, nch, SSM_CHUNK_LANES), lambda i: (0, jnp.minimum(i + 1, last), 0)),
                  pl.BlockSpec((1, d), fixed),
                  pl.BlockSpec((d, 2 * d), fixed)],
        out_specs=pl.BlockSpec((tm, d), row),
        out_shape=jax.ShapeDtypeStruct((m, d), F32),
        scratch_shapes=[pltpu.VMEM((2, d // LANES, tm, LANES), F32)],
        compiler_params=_params(("arbitrary",), 48),
        name="s5_gelu_glu",
    )(x2d, g.reshape(1, d), yg, yg, d_skip.reshape(1, d), w_glu_bf16)


def _s5_operator_kernel(lam_re_ref, lam_im_ref, ldt_ref, bt_re_ref, bt_im_ref, c_re_ref, c_im_ref,
                        m_ref, w_ref, v_ref, a_ref):
    t = SSM_CHUNK
    c = SSM_GROUP
    lane = lax.broadcasted_iota(I32, (1, LANES), 1)
    fwd = lane < SSM_STATE
    lr = lam_re_ref[0]
    li = lam_im_ref[0]
    dt = jnp.exp(ldt_ref[0])
    mag = jnp.exp(lr * dt)
    ang = li * dt
    lb_re = mag * jnp.cos(ang)
    lb_im = mag * jnp.sin(ang)
    den = lr * lr + li * li
    nr = lb_re - 1.0
    f_re = (nr * lr + lb_im * li) / den
    f_im = (lb_im * lr - nr * li) / den
    bt_re = bt_re_ref[0]
    bt_im = bt_im_ref[0]
    bb_re = f_re * bt_re - f_im * bt_im
    bb_im = f_re * bt_im + f_im * bt_re
    pw = [(jnp.ones((1, LANES), F32), jnp.zeros((1, LANES), F32))]
    for _ in range(t):
        pr, pi = pw[-1]
        pw.append((pr * lb_re - pi * lb_im, pr * lb_im + pi * lb_re))
    c_re = c_re_ref[0]
    c_im = c_im_ref[0]
    cl = [(c_re * pr - c_im * pi, c_re * pi + c_im * pr) for pr, pi in pw]

    def dot_nt(lhs, rhs):
        return lax.dot_general(lhs.astype(BF16), rhs.astype(BF16), (((1,), (1,)), ((), ())),
                               preferred_element_type=F32)

    def kernels(order, keep):
        sre = jnp.concatenate([cl[l][0] for l in order], axis=0)
        sim = jnp.concatenate([cl[l][1] for l in order], axis=0)
        return (dot_nt(jnp.where(keep, bb_re, 0.0), sre) - dot_nt(jnp.where(keep, bb_im, 0.0), sim))

    ktf = kernels(range(t), fwd)
    ktb = kernels(range(t - 1, -1, -1), jnp.logical_not(fwd))
    lane2 = lax.broadcasted_iota(I32, (1, t * c), 1)
    last = (t - 1) * c
    lo = ktb + jnp.where(lane2 >= last, pltpu.roll(ktf, last, axis=1), 0.0)
    hi = jnp.where(lane2 < last, pltpu.roll(ktf, t * c - c, axis=1), 0.0)
    by_lag = jnp.concatenate([lo, hi], axis=1)
    for s in range(t):
        k = (t - 1 - s) * c
        m_ref[0, s * c:(s + 1) * c, :] = by_lag[:, k:k + t * c].astype(BF16)

    vt_rows = []
    for s in range(t):
        pr = jnp.where(fwd, pw[t - 1 - s][0], pw[s][0])
        pi = jnp.where(fwd, pw[t - 1 - s][1], pw[s][1])
        w_ref[0, s * c:(s + 1) * c, :] = jnp.concatenate(
            [pr * bb_re - pi * bb_im, pi * bb_re + pr * bb_im], axis=1).astype(BF16)
        vr = jnp.where(fwd, cl[s + 1][0], cl[t - s][0])
        vi = jnp.where(fwd, cl[s + 1][1], cl[t - s][1])
        vt_rows.append(jnp.concatenate([vr, -vi], axis=1))
    v_ref[0] = jnp.concatenate(vt_rows, axis=0).T.astype(BF16)
    row = lax.broadcasted_iota(I32, (8, 1), 0)
    a_ref[0] = jnp.where(row == 0, pw[t][0], jnp.where(row == 1, pw[t][1], 0.0))


def _s5_operators(lam_re, lam_im, log_dt, b_re, b_im, c_re, c_im):
    g, p = lam_re.shape[1:]
    c = SSM_GROUP
    k = SSM_CHUNK * c
    assert 2 * p == LANES and k == 2 * LANES

    def both(a):
        return jnp.concatenate([a[0], a[1]], axis=-1)

    rowvec = lambda a: both(a).reshape(g, 1, LANES)
    ldt = rowvec(jnp.broadcast_to(log_dt[..., None], (2, g, p)))
    bt = lambda a: both(jnp.swapaxes(a, 2, 3))
    vec = pl.BlockSpec((1, 1, LANES), lambda i: (i, 0, 0))
    mat = pl.BlockSpec((1, c, LANES), lambda i: (i, 0, 0))
    sq = pl.BlockSpec((1, k, k), lambda i: (i, 0, 0))
    return pl.pallas_call(
        _s5_operator_kernel,
        grid=(g,),
        in_specs=[vec, vec, vec, mat, mat, mat, mat],
        out_specs=[sq, sq, sq, pl.BlockSpec((1, 8, LANES), lambda i: (i, 0, 0))],
        out_shape=[jax.ShapeDtypeStruct((g, k, k), BF16)] * 3 + [jax.ShapeDtypeStruct((g, 8, LANES), F32)],
        compiler_params=_params(("parallel",), 32),
        name="s5_operators",
    )(rowvec(lam_re), rowvec(lam_im), ldt, bt(b_re), bt(b_im), both(c_re), both(c_im))


def _s5_mixer(x, mix_g, lam_re, lam_im, log_dt, b_re, b_im, c_re, c_im, d_skip, w_glu):
    b, s, d = x.shape
    t = SSM_CHUNK
    g = d // SSM_GROUP
    nc = s // t
    x2d = x.reshape(b * s, d)
    ug = _norm_flatten(x2d, mix_g)
    mmat, wmat, vmat, amat = _s5_operators(lam_re, lam_im, log_dt, b_re, b_im, c_re, c_im)
    yg = _s5_core(ug, mmat, wmat, vmat, amat, nc)
    out = _s5_post(x2d, mix_g, yg, d_skip, w_glu.astype(BF16))
    return out.reshape(b, s, d)


def kernel(x, mix_norm_even, w_in, conv_w, conv_b, conv_ln_g, conv_ln_b, q_norm, k_norm, w_out, mix_norm_odd, ssm_lam_re, ssm_lam_im, ssm_log_dt, ssm_b_re, ssm_b_im, ssm_c_re, ssm_c_im, ssm_d, w_glu, ffn_norm, w_router, b_router, w_e_gate, w_e_up, w_e_down):
    depth = ffn_norm.shape[0]
    for layer in range(depth):
        i = layer // 2
        if layer % 2 == 0:
            x = _even_mixer(x, mix_norm_even[i], w_in[i], conv_w[i], conv_b[i], conv_ln_g[i],
                            conv_ln_b[i], q_norm[i], k_norm[i], w_out[i])
        else:
            x = _s5_mixer(x, mix_norm_odd[i], ssm_lam_re[i], ssm_lam_im[i], ssm_log_dt[i],
                          ssm_b_re[i], ssm_b_im[i], ssm_c_re[i], ssm_c_im[i], ssm_d[i], w_glu[i])
        x = _moe(x, ffn_norm[layer], w_router[layer], b_router[layer],
                 w_e_gate, w_e_up, w_e_down, layer)
    return x
```

```python
import functools
import math

import jax
import jax.numpy as jnp
from jax import lax
from jax.experimental import pallas as pl
from jax.experimental.pallas import tpu as pltpu

F32 = jnp.float32
BF16 = jnp.bfloat16
I32 = jnp.int32

EPS = 1e-6
NEG_INF = -1e30
LOG2_E = math.log2(math.e)

CONV_CH = 512
CONV_WIDTH = 31
ATT_HEADS = 8
HEAD_DIM = 64
ATT_WIDTH = ATT_HEADS * HEAD_DIM
DILATED = ((128, 1), (512, 4), (2048, 16))
SSM_GROUP = 16
SSM_STATE = 64
N_EXPERTS = 16
CAPACITY_FACTOR = 2
MOE_GATHER_TOKENS = 1024
MOE_GATHER_GROUP = 16
MOE_SCATTER_TOKENS = 256

LANES = 128
F32_MIN_EXP = -126
F32_MANTISSA_BITS = 23
MIB = 1024 * 1024


def _params(sem, vmem_mib):
    return pltpu.CompilerParams(dimension_semantics=sem, vmem_limit_bytes=vmem_mib * MIB)


def _rmsnorm(x, g):
    return x * lax.rsqrt(jnp.mean(x * x, axis=-1, keepdims=True) + EPS) * g


def _norm_matmul_kernel(x_ref, g_ref, w_ref, o_ref):
    y = _rmsnorm(x_ref[...], g_ref[...])
    o_ref[...] = jnp.dot(y.astype(BF16), w_ref[...], preferred_element_type=F32)


def _norm_matmul(x2d, g, w_bf16, tm=512):
    m, d = x2d.shape
    n = w_bf16.shape[1]
    return pl.pallas_call(
        _norm_matmul_kernel,
        grid=(m // tm,),
        in_specs=[pl.BlockSpec((tm, d), lambda i: (i, 0)),
                  pl.BlockSpec((1, d), lambda i: (0, 0)),
                  pl.BlockSpec((d, n), lambda i: (0, 0))],
        out_specs=pl.BlockSpec((tm, n), lambda i: (i, 0)),
        out_shape=jax.ShapeDtypeStruct((m, n), F32),
        compiler_params=_params(("parallel",), 48),
        name="norm_in_proj",
    )(x2d, g.reshape(1, d), w_bf16)


CONV_HALO = 16
CONV_ROWS = 256


def _conv_kernel(val_ref, gate_ref, w_ref, b_ref, o_ref, hp_ref):
    s = val_ref.shape[1]
    zeros = jnp.zeros((CONV_HALO, LANES), F32)
    hp_ref[0:CONV_HALO, :] = zeros
    hp_ref[CONV_HALO + s:2 * CONV_HALO + s, :] = zeros
    hp_ref[CONV_HALO:CONV_HALO + s, :] = val_ref[0] * jax.nn.sigmoid(gate_ref[0])
    first = CONV_HALO - CONV_WIDTH // 2
    for c in range(s // CONV_ROWS):
        r0 = c * CONV_ROWS
        acc = jnp.broadcast_to(b_ref[...], (CONV_ROWS, LANES))
        for k in range(CONV_WIDTH):
            acc = acc + w_ref[k:k + 1, :] * hp_ref[r0 + first + k:r0 + first + k + CONV_ROWS, :]
        o_ref[0, r0:r0 + CONV_ROWS, :] = acc


def _conformer_conv(proj3, conv_w, conv_b):
    b, s, _ = proj3.shape
    nct = CONV_CH // LANES
    return pl.pallas_call(
        _conv_kernel,
        grid=(b, nct),
        in_specs=[pl.BlockSpec((1, s, LANES), lambda i, j: (i, 0, j)),
                  pl.BlockSpec((1, s, LANES), lambda i, j: (i, 0, nct + j)),
                  pl.BlockSpec((CONV_WIDTH, LANES), lambda i, j: (0, j)),
                  pl.BlockSpec((1, LANES), lambda i, j: (0, j))],
        out_specs=pl.BlockSpec((1, s, LANES), lambda i, j: (i, 0, j)),
        out_shape=jax.ShapeDtypeStruct((b, s, CONV_CH), F32),
        scratch_shapes=[pltpu.VMEM((s + 2 * CONV_HALO, LANES), F32)],
        compiler_params=_params(("parallel", "parallel"), 32),
        name="glu_dwconv",
    )(proj3, proj3, conv_w, conv_b.reshape(1, CONV_CH))


ATT_QB = 128
ATT_RAD = 64
ATT_KW = ATT_QB + 2 * ATT_RAD
ATT_PREP_ROWS = 512
ATT_UNROLL = 32


def _attn_kernel(slopes_ref, q_ref, k_ref, v_ref, qg_ref, kg_ref, o_ref,
                 qs_ref, kp_ref, vp_ref, ob_ref, mb_ref, *, seq, pad):
    pair = pl.program_id(1)
    lane = lax.broadcasted_iota(I32, (1, LANES), 1)
    lo = lane < HEAD_DIM
    scale = LOG2_E / math.sqrt(HEAD_DIM)
    spare = (HEAD_DIM, 0)

    zpad = jnp.zeros((pad, LANES), F32)
    for hh in range(2):
        kpad = jnp.broadcast_to(jnp.where(lane == spare[hh], NEG_INF, 0.0), (pad, LANES))
        kp_ref[hh, 0:pad, :] = kpad
        kp_ref[hh, pad + seq:2 * pad + seq, :] = kpad
        vp_ref[hh, 0:pad, :] = zpad
        vp_ref[hh, pad + seq:2 * pad + seq, :] = zpad

    def head_rms(x, g):
        x2 = x * x
        s_lo = jnp.sum(jnp.where(lo, x2, 0.0), axis=-1, keepdims=True)
        s_hi = jnp.sum(jnp.where(lo, 0.0, x2), axis=-1, keepdims=True)
        ms = jnp.where(lo, s_lo, s_hi) * (1.0 / HEAD_DIM)
        return x * lax.rsqrt(ms + EPS) * g

    def prep(i, carry):
        r = pl.multiple_of(i * ATT_PREP_ROWS, ATT_PREP_ROWS)
        rows = pl.ds(r, ATT_PREP_ROWS)
        prows = pl.ds(pad + r, ATT_PREP_ROWS)
        qn = head_rms(q_ref[0, rows, :], qg_ref[...]) * scale
        qs_ref[0, rows, :] = jnp.where(lo, qn, jnp.where(lane == spare[0], 1.0, 0.0))
        qs_ref[1, rows, :] = jnp.where(lo, jnp.where(lane == spare[1], 1.0, 0.0), qn)
        kn = head_rms(k_ref[0, rows, :], kg_ref[...])
        kp_ref[0, prows, :] = jnp.where(lo, kn, 0.0)
        kp_ref[1, prows, :] = jnp.where(lo, 0.0, kn)
        v = v_ref[0, rows, :]
        vp_ref[0, prows, :] = jnp.where(lo, v, jnp.where(lane == spare[0], 1.0, 0.0))
        vp_ref[1, prows, :] = jnp.where(lo, jnp.where(lane == spare[1], 1.0, 0.0), v)
        return carry

    lax.fori_loop(0, seq // ATT_PREP_ROWS, prep, 0)

    rowi = lax.broadcasted_iota(I32, (ATT_QB, ATT_KW), 0)
    coli = lax.broadcasted_iota(I32, (ATT_QB, ATT_KW), 1)
    arel = jnp.abs(coli - ATT_RAD - rowi)

    for hh in range(2):
        slope = slopes_ref[pair * 2 + hh] * LOG2_E
        for bi, (window, dil) in enumerate(DILATED):
            assert window // (2 * dil) == ATT_RAD
            nbc = seq // dil // ATT_QB
            bias = jnp.where(arel <= ATT_RAD, -(slope * dil) * arel.astype(F32), NEG_INF)

            def blk(j, carry, bi=bi, dil=dil, nbc=nbc, bias=bias, hh=hh):
                c = j // nbc
                n = j - c * nbc
                qstart = c + dil * ATT_QB * n
                kstart = pad + c + dil * (ATT_QB * n - ATT_RAD)
                if dil == 1:
                    qsl = pl.ds(qstart, ATT_QB)
                    ksl = pl.ds(kstart, ATT_KW)
                else:
                    qsl = pl.ds(qstart, ATT_QB, stride=dil)
                    ksl = pl.ds(kstart, ATT_KW, stride=dil)
                qb = qs_ref[hh, qsl, :].astype(BF16)
                kw = kp_ref[hh, ksl, :].astype(BF16)
                vw = vp_ref[hh, ksl, :].astype(BF16)
                sc = lax.dot_general(qb, kw, (((1,), (1,)), ((), ())),
                                     preferred_element_type=F32) + bias
                m = jnp.max(sc, axis=-1, keepdims=True)
                p = jnp.exp2(sc - m)
                ob_ref[bi, qsl, :] = jnp.dot(p.astype(BF16), vw, preferred_element_type=F32)
                mb_ref[bi, qsl, :] = jnp.broadcast_to(m, (ATT_QB, LANES))
                return carry

            lax.fori_loop(0, seq // ATT_QB, blk, 0, unroll=ATT_UNROLL)

        def combine(i, carry, hh=hh):
            r = pl.multiple_of(i * ATT_PREP_ROWS, ATT_PREP_ROWS)
            rows = pl.ds(r, ATT_PREP_ROWS)
            nbr = len(DILATED)
            ms = [mb_ref[bi, rows, :] for bi in range(nbr)]
            mx = functools.reduce(jnp.maximum, ms)
            acc = sum(jnp.exp2(ms[bi] - mx) * ob_ref[bi, rows, :] for bi in range(nbr))
            den = jnp.sum(jnp.where(lane == spare[hh], acc, 0.0), axis=-1, keepdims=True)
            res = acc / den
            if hh == 0:
                o_ref[0, rows, :] = res
            else:
                o_ref[0, rows, :] = jnp.where(lo, o_ref[0, rows, :], res)
            return carry

        lax.fori_loop(0, seq // ATT_PREP_ROWS, combine, 0)


def _dilated_attention(proj3, q_norm, k_norm):
    b, s, _ = proj3.shape
    pad = ATT_RAD * max(d for _, d in DILATED)
    npair = ATT_HEADS // 2
    qoff = 2 * CONV_CH // LANES
    slopes = jnp.asarray([2.0 ** (-8.0 * (h + 1) / ATT_HEADS) for h in range(ATT_HEADS)], F32)
    qg = jnp.tile(q_norm.reshape(1, HEAD_DIM), (1, 2))
    kg = jnp.tile(k_norm.reshape(1, HEAD_DIM), (1, 2))
    kernel = functools.partial(_attn_kernel, seq=s, pad=pad)
    return pl.pallas_call(
        kernel,
        grid=(b, npair),
        in_specs=[pl.BlockSpec(memory_space=pltpu.SMEM),
                  pl.BlockSpec((1, s, LANES), lambda i, j: (i, 0, qoff + j)),
                  pl.BlockSpec((1, s, LANES), lambda i, j: (i, 0, qoff + npair + j)),
                  pl.BlockSpec((1, s, LANES), lambda i, j: (i, 0, qoff + 2 * npair + j)),
                  pl.BlockSpec((1, LANES), lambda i, j: (0, 0)),
                  pl.BlockSpec((1, LANES), lambda i, j: (0, 0))],
        out_specs=pl.BlockSpec((1, s, LANES), lambda i, j: (i, 0, j)),
        out_shape=jax.ShapeDtypeStruct((b, s, ATT_WIDTH), F32),
        scratch_shapes=[pltpu.VMEM((2, s, LANES), F32),
                        pltpu.VMEM((2, s + 2 * pad, LANES), F32),
                        pltpu.VMEM((2, s + 2 * pad, LANES), F32),
                        pltpu.VMEM((len(DILATED), s, LANES), F32),
                        pltpu.VMEM((len(DILATED), s, LANES), F32)],
        compiler_params=_params(("parallel", "parallel"), 56),
        name="dilated_attn",
    )(slopes, proj3, proj3, proj3, qg, kg)


def _outproj_kernel(x_ref, conv_ref, att_ref, g_ref, b_ref, w1_ref, w2_ref, o_ref):
    c = conv_ref[...]
    mu = jnp.mean(c, axis=-1, keepdims=True)
    cc = c - mu
    var = jnp.mean(cc * cc, axis=-1, keepdims=True)
    hn = cc * lax.rsqrt(var + EPS) * g_ref[...] + b_ref[...]
    a = hn * jax.nn.sigmoid(hn)
    y = jnp.dot(a.astype(BF16), w1_ref[...], preferred_element_type=F32)
    y = y + jnp.dot(att_ref[...].astype(BF16), w2_ref[...], preferred_element_type=F32)
    o_ref[...] = x_ref[...] + y


def _out_projection(x2d, conv2d, att2d, ln_g, ln_b, w_out_bf16, tm=512):
    m, d = x2d.shape
    row = lambda i: (i, 0)
    fixed = lambda i: (0, 0)
    return pl.pallas_call(
        _outproj_kernel,
        grid=(m // tm,),
        in_specs=[pl.BlockSpec((tm, d), row),
                  pl.BlockSpec((tm, CONV_CH), row),
                  pl.BlockSpec((tm, ATT_WIDTH), row),
                  pl.BlockSpec((1, CONV_CH), fixed),
                  pl.BlockSpec((1, CONV_CH), fixed),
                  pl.BlockSpec((CONV_CH, d), fixed),
                  pl.BlockSpec((ATT_WIDTH, d), lambda i: (1, 0))],
        out_specs=pl.BlockSpec((tm, d), row),
        out_shape=jax.ShapeDtypeStruct((m, d), F32),
        compiler_params=_params(("parallel",), 32),
        name="ln_out_proj",
    )(x2d, conv2d, att2d, ln_g.reshape(1, CONV_CH), ln_b.reshape(1, CONV_CH), w_out_bf16, w_out_bf16)


def _even_mixer(x, mix_g, w_in, conv_w, conv_b, ln_g, ln_b, q_norm, k_norm, w_out):
    b, s, d = x.shape
    x2d = x.reshape(b * s, d)
    proj = _norm_matmul(x2d, mix_g, w_in.astype(BF16))
    proj3 = proj.reshape(b, s, -1)
    conv = _conformer_conv(proj3, conv_w, conv_b)
    att = _dilated_attention(proj3, q_norm, k_norm)
    out = _out_projection(x2d, conv.reshape(b * s, CONV_CH), att.reshape(b * s, ATT_WIDTH),
                          ln_g, ln_b, w_out.astype(BF16))
    return out.reshape(b, s, d)


def _router_kernel(x_ref, g_ref, wr_ref, br_ref, aff_ref):
    h = _rmsnorm(x_ref[0], g_ref[...])
    logits = jnp.dot(h, wr_ref[...], precision=lax.Precision.HIGHEST,
                     preferred_element_type=F32) + br_ref[...]
    lt = logits.T[:N_EXPERTS, :]
    mx = jnp.max(lt, axis=0, keepdims=True)
    ex = jnp.exp(lt - mx)
    aff_ref[0] = ex / jnp.sum(ex, axis=0, keepdims=True)


def _router(x, g, w_router, b_router, tm=512):
    b, s, d = x.shape
    wr = jnp.zeros((d, LANES), F32).at[:, :N_EXPERTS].set(w_router)
    br = jnp.zeros((1, LANES), F32).at[0, :N_EXPERTS].set(b_router)
    return pl.pallas_call(
        _router_kernel,
        grid=(b, s // tm),
        in_specs=[pl.BlockSpec((1, tm, d), lambda i, j: (i, j, 0)),
                  pl.BlockSpec((1, d), lambda i, j: (0, 0)),
                  pl.BlockSpec((d, LANES), lambda i, j: (0, 0)),
                  pl.BlockSpec((1, LANES), lambda i, j: (0, 0))],
        out_specs=pl.BlockSpec((1, N_EXPERTS, tm), lambda i, j: (i, 0, j)),
        out_shape=jax.ShapeDtypeStruct((b, N_EXPERTS, s), F32),
        compiler_params=_params(("parallel", "parallel"), 32),
        name="moe_router",
    )(x, g.reshape(1, d), wr, br)


def _select_kernel(aff_ref, post_ref, starts_ref, idx_ref, gatet_ref, cs_ref, *, cap):
    a = aff_ref[0]
    s = a.shape[1]
    capf = float(cap)

    def count(mask):
        return jnp.sum(jnp.where(mask, 1.0, 0.0), axis=-1, keepdims=True)

    def enough(cand):
        return count(a >= cand) >= capf

    tiny = float(2.0 ** F32_MIN_EXP)
    thr = jnp.full((N_EXPERTS, 1), tiny, F32)
    normal = enough(thr)
    for j in range(6, -1, -1):
        cand = thr * float(2.0 ** (2 ** j))
        thr = jnp.where(enough(cand), cand, thr)
    step = thr * 0.5
    for _ in range(F32_MANTISSA_BITS):
        cand = thr + step
        thr = jnp.where(enough(cand), cand, thr)
        step = step * 0.5
    thr = jnp.where(normal, thr, 0.0)

    ri = lax.broadcasted_iota(I32, (LANES, LANES), 0)
    ci = lax.broadcasted_iota(I32, (LANES, LANES), 1)
    tri = jnp.where(ri < ci, 1.0, 0.0).astype(BF16)

    lane = lax.broadcasted_iota(I32, (1, LANES), 1)
    chunks_per_tile = MOE_SCATTER_TOKENS // LANES

    def exclusive_cumsum(mask):
        ind = jnp.where(mask, 1.0, 0.0)
        off = jnp.zeros((N_EXPERTS, 1), F32)
        starts = jnp.zeros((N_EXPERTS, LANES), F32)
        for j in range(s // LANES):
            if j % chunks_per_tile == 0:
                starts = jnp.where(lane == j // chunks_per_tile, off, starts)
            xc = ind[:, j * LANES:(j + 1) * LANES]
            cs_ref[:, j * LANES:(j + 1) * LANES] = (
                jnp.dot(xc.astype(BF16), tri, preferred_element_type=F32) + off)
            off = off + jnp.sum(xc, axis=-1, keepdims=True)
        starts = jnp.where(lane == s // MOE_SCATTER_TOKENS, off, starts)
        return cs_ref[...], starts

    gt = a > thr
    eq = a == thr
    need = capf - count(gt)
    tie_rank, _ = exclusive_cumsum(eq)
    sel = gt | (eq & (tie_rank < need))
    slot, starts = exclusive_cumsum(sel)
    slot = jnp.where(sel, slot, -1.0)
    starts_ref[0] = starts.astype(I32)
    fill = jnp.zeros((LANES - N_EXPERTS, s), F32)
    post_ref[0] = jnp.concatenate([slot, fill], axis=0).T

    radix = float(s)
    tok = lax.broadcasted_iota(I32, (N_EXPERTS, s), 1).astype(F32)
    packed = jnp.where(sel, (tok - slot) * radix + tok, -1.0)
    gate = jnp.where(sel, a, 0.0)
    for k in range(s.bit_length() - 1):
        dist = jnp.floor(packed * (1.0 / radix))
        bit = jnp.floor(dist * 0.5 ** k) - 2.0 * jnp.floor(dist * 0.5 ** (k + 1))
        moving = jnp.where(packed >= 0.0, bit, 0.0) == 1.0
        arrived = pltpu.roll(jnp.where(moving, packed, -1.0), s - 2 ** k, axis=1)
        arrived_gate = pltpu.roll(jnp.where(moving, gate, 0.0), s - 2 ** k, axis=1)
        gate = jnp.where(arrived >= 0.0, arrived_gate, jnp.where(moving, 0.0, gate))
        packed = jnp.where(arrived >= 0.0, arrived, jnp.where(moving, -1.0, packed))
    head = packed[:, :cap]
    idx_ref[0] = (head - radix * jnp.floor(head * (1.0 / radix))).astype(I32)
    gatet_ref[0] = jnp.concatenate([gate[:, :cap], jnp.zeros((LANES - N_EXPERTS, cap), F32)], axis=0).T


def _select(aff, cap):
    b, e, s = aff.shape
    assert s // MOE_SCATTER_TOKENS < LANES
    assert s & (s - 1) == 0 and s * s <= 2 ** 24
    blk = pl.BlockSpec((1, e, s), lambda i: (i, 0, 0))
    return pl.pallas_call(
        functools.partial(_select_kernel, cap=cap),
        grid=(b,),
        in_specs=[blk],
        out_specs=[pl.BlockSpec((1, s, LANES), lambda i: (i, 0, 0)),
                   pl.BlockSpec((1, e, LANES), lambda i: (i, 0, 0)),
                   pl.BlockSpec((1, e, cap), lambda i: (i, 0, 0)),
                   pl.BlockSpec((1, cap, LANES), lambda i: (i, 0, 0))],
        out_shape=[jax.ShapeDtypeStruct((b, s, LANES), F32),
                   jax.ShapeDtypeStruct((b, e, LANES), I32),
                   jax.ShapeDtypeStruct((b, e, cap), I32),
                   jax.ShapeDtypeStruct((b, cap, LANES), F32)],
        scratch_shapes=[pltpu.VMEM((e, s), F32)],
        compiler_params=_params(("parallel",), 32),
        name="moe_select",
    )(aff)


def _gather_kernel(idx_ref, x_ref, g_ref, h_ref, rows_scr, *, cap):
    group = rows_scr.shape[1]

    def copy_group(gidx, carry):
        base = gidx * group
        for r in range(group):
            tok = idx_ref[0, 0, 0, base + r]
            rows_scr[gidx, pl.ds(r, 1), :] = x_ref[0, pl.ds(tok, 1), :]
        return carry

    lax.fori_loop(0, cap // group, copy_group, 0)
    rows = rows_scr[...].reshape(cap, rows_scr.shape[2])
    h_ref[0, 0] = _rmsnorm(rows, g_ref[...]).astype(BF16)


def _gather_rows(x, g, idx, cap):
    b, s, d = x.shape
    e = idx.shape[1]
    return pl.pallas_call(
        functools.partial(_gather_kernel, cap=cap),
        grid=(b, e),
        in_specs=[pl.BlockSpec((1, 1, 1, cap), lambda i, j: (i, j, 0, 0), memory_space=pltpu.SMEM),
                  pl.BlockSpec((1, s, d), lambda i, j: (i, 0, 0), pipeline_mode=pl.Buffered(1)),
                  pl.BlockSpec((1, d), lambda i, j: (0, 0))],
        out_specs=pl.BlockSpec((1, 1, cap, d), lambda i, j: (i, j, 0, 0)),
        out_shape=jax.ShapeDtypeStruct((b, e, cap, d), BF16),
        scratch_shapes=[pltpu.VMEM((cap // MOE_GATHER_GROUP, MOE_GATHER_GROUP, d), F32)],
        compiler_params=_params(("parallel", "parallel"), 40),
        name="moe_gather",
    )(idx.reshape(b, e, 1, cap), x, g.reshape(1, d))


def _expert_kernel(x_ref, gate_ref, wg_ref, wu_ref, wd_ref, y_ref, wg_s, wu_s, wd_s):
    ei = pl.program_id(0)

    @pl.when(pl.program_id(1) == 0)
    def _():
        wg_s[...] = wg_ref[0, 0].astype(BF16)
        wu_s[...] = wu_ref[0, 0].astype(BF16)
        wd_s[...] = wd_ref[0, 0].astype(BF16)

    lane = lax.broadcasted_iota(I32, (1, LANES), 1)
    gate = jnp.sum(jnp.where(lane == ei, gate_ref[0], 0.0), axis=-1, keepdims=True)
    x = x_ref[0, 0]
    g = jnp.dot(x, wg_s[...], preferred_element_type=F32)
    u = jnp.dot(x, wu_s[...], preferred_element_type=F32)
    a = (g * jax.nn.sigmoid(g) * u).astype(BF16)
    y = jnp.dot(a, wd_s[...], preferred_element_type=F32)
    y_ref[0, 0] = (y * gate).astype(BF16)


def _experts(xin, gate_t, w_gate, w_up, w_down, layer):
    b, e, cap, d = xin.shape
    f = w_gate.shape[3]
    return pl.pallas_call(
        _expert_kernel,
        grid=(e, b),
        in_specs=[pl.BlockSpec((1, 1, cap, d), lambda j, i: (i, j, 0, 0)),
                  pl.BlockSpec((1, cap, LANES), lambda j, i: (i, 0, 0)),
                  pl.BlockSpec((1, 1, d, f), lambda j, i: (layer, j, 0, 0)),
                  pl.BlockSpec((1, 1, d, f), lambda j, i: (layer, j, 0, 0)),
                  pl.BlockSpec((1, 1, f, d), lambda j, i: (layer, j, 0, 0))],
        out_specs=pl.BlockSpec((1, 1, cap, d), lambda j, i: (i, j, 0, 0)),
        out_shape=jax.ShapeDtypeStruct((b, e, cap, d), BF16),
        scratch_shapes=[pltpu.VMEM((d, f), BF16), pltpu.VMEM((d, f), BF16), pltpu.VMEM((f, d), BF16)],
        compiler_params=_params(("parallel", "arbitrary"), 56),
        name="moe_experts",
    )(xin, gate_t, w_gate, w_up, w_down)


def _scatter_kernel(starts_ref, x_ref, y_ref, post_ref, o_ref, *, cap):
    bi = pl.program_id(0)
    ti = pl.program_id(1)
    win = MOE_SCATTER_TOKENS
    lane = lax.broadcasted_iota(I32, (1, LANES), 1)
    wslot = lax.broadcasted_iota(I32, (1, win), 1).astype(F32)
    post = post_ref[0]

    def slot_column(e):
        return jnp.sum(jnp.where(lane == e, post, 0.0), axis=-1, keepdims=True)

    def window_start(e):
        first = starts_ref[bi, e, ti]
        off = jnp.minimum(lax.shift_left(lax.shift_right_logical(first, 4), 4), cap - win)
        return pl.multiple_of(off, 16)

    acc = x_ref[0]
    for e in range(N_EXPERTS):
        off = window_start(e)
        onehot = jnp.where(slot_column(e) - off.astype(F32) == wslot, 1.0, 0.0).astype(BF16)
        acc = acc + jnp.dot(onehot, y_ref[0, e, pl.ds(off, win), :], preferred_element_type=F32)
    o_ref[0] = acc

    tail = cap - win
    for e in range(N_EXPERTS):
        covered = window_start(e) + win

        @pl.when(starts_ref[bi, e, ti + 1] > covered)
        def _(e=e, covered=covered):
            pcol = slot_column(e)
            hit = (pcol - float(tail) == wslot) & (pcol >= covered.astype(F32))
            o_ref[0] += jnp.dot(jnp.where(hit, 1.0, 0.0).astype(BF16),
                                y_ref[0, e, tail:cap, :], preferred_element_type=F32)


def _scatter(x, y, post, starts, cap):
    b, s, d = x.shape
    e = y.shape[1]
    tn = MOE_SCATTER_TOKENS
    assert cap % 16 == 0 and tn <= cap <= 2 * tn
    grid_spec = pltpu.PrefetchScalarGridSpec(
        num_scalar_prefetch=1,
        grid=(b, s // tn),
        in_specs=[pl.BlockSpec((1, tn, d), lambda i, j, st: (i, j, 0)),
                  pl.BlockSpec((1, e, cap, d), lambda i, j, st: (i, 0, 0, 0),
                               pipeline_mode=pl.Buffered(1)),
                  pl.BlockSpec((1, tn, LANES), lambda i, j, st: (i, j, 0))],
        out_specs=pl.BlockSpec((1, tn, d), lambda i, j, st: (i, j, 0)),
    )
    return pl.pallas_call(
        functools.partial(_scatter_kernel, cap=cap),
        grid_spec=grid_spec,
        out_shape=jax.ShapeDtypeStruct((b, s, d), F32),
        compiler_params=_params(("parallel", "parallel"), 48),
        name="moe_scatter",
    )(starts, x, y, post)


def _moe(x, ffn_g, w_router, b_router, w_gate, w_up, w_down, layer):
    b, s, d = x.shape
    cap = CAPACITY_FACTOR * s // N_EXPERTS
    aff = _router(x, ffn_g, w_router, b_router)
    post, starts, idx, gate_t = _select(aff, cap)
    xin = _gather_rows(x, ffn_g, idx, cap)
    y = _experts(xin, gate_t, w_gate, w_up, w_down, layer)
    return _scatter(x, y, post, starts, cap)


SSM_CHUNK = 16


SSM_SLAB_GROUPS = LANES // SSM_GROUP
SSM_CHUNK_LANES = SSM_CHUNK * SSM_GROUP
SSM_HALF_STEPS = LANES // SSM_GROUP


def _transpose_lane_groups(xs):
    lgrp = lax.broadcasted_iota(I32, (1, LANES), 1) // SSM_GROUP
    xs = list(xs)
    assert len(xs) == SSM_SLAB_GROUPS
    h = SSM_SLAB_GROUPS // 2
    while h:
        upper = (lgrp & h) != 0
        for a0 in range(SSM_SLAB_GROUPS):
            if a0 & h:
                continue
            x0, x1 = xs[a0], xs[a0 + h]
            xs[a0] = jnp.where(upper, pltpu.roll(x1, h * SSM_GROUP, axis=1), x0)
            xs[a0 + h] = jnp.where(upper, x1, pltpu.roll(x0, LANES - h * SSM_GROUP, axis=1))
        h //= 2
    return xs


def _norm_flatten_kernel(x_ref, g_ref, ug_ref, u_scr):
    tm, d = x_ref.shape
    nch = tm // SSM_CHUNK
    u = _rmsnorm(x_ref[...], g_ref[...])
    lgrp = lax.broadcasted_iota(I32, (1, LANES), 1) // SSM_GROUP
    for slab in range(d // LANES):
        u_scr[slab] = u[:, slab * LANES:(slab + 1) * LANES]
        steps = [u_scr[slab, pl.ds(t, nch, stride=SSM_CHUNK), :] for t in range(SSM_CHUNK)]
        for gi in range(SSM_SLAB_GROUPS):
            for half in range(SSM_CHUNK_LANES // LANES):
                acc = None
                for tt in range(SSM_HALF_STEPS):
                    shift = ((tt - gi) * SSM_GROUP) % LANES
                    piece = steps[half * SSM_HALF_STEPS + tt]
                    if shift:
                        piece = pltpu.roll(piece, shift, axis=1)
                    acc = piece if acc is None else jnp.where(lgrp == tt, piece, acc)
                ug_ref[slab * SSM_SLAB_GROUPS + gi, :, half * LANES:(half + 1) * LANES] = acc.astype(BF16)


def _norm_flatten(x2d, g, tm=512):
    m, d = x2d.shape
    ngrp = d // SSM_GROUP
    nch = tm // SSM_CHUNK
    return pl.pallas_call(
        _norm_flatten_kernel,
        grid=(m // tm,),
        in_specs=[pl.BlockSpec((tm, d), lambda i: (i, 0)),
                  pl.BlockSpec((1, d), lambda i: (0, 0))],
        out_specs=pl.BlockSpec((ngrp, nch, SSM_CHUNK_LANES), lambda i: (0, i, 0)),
        out_shape=jax.ShapeDtypeStruct((ngrp, m // SSM_CHUNK, SSM_CHUNK_LANES), BF16),
        scratch_shapes=[pltpu.VMEM((d // LANES, tm, LANES), F32)],
        compiler_params=_params(("parallel",), 32),
        name="s5_norm_flatten",
    )(x2d, g.reshape(1, d))


def _s5_core_kernel(u_ref, m_ref, w_ref, v_ref, a_ref, y_ref, *, nc):
    u = u_ref[0]
    n_rows = u.shape[0]
    y = jnp.dot(u, m_ref[0], preferred_element_type=F32)
    x = jnp.dot(u, w_ref[0], preferred_element_type=F32)
    hr = x[:, :LANES]
    hi = x[:, LANES:]
    ar = a_ref[0, 0:1, :]
    ai = a_ref[0, 1:2, :]
    lane = lax.broadcasted_iota(I32, (1, LANES), 1)
    fwd = lane < SSM_STATE
    rown = lax.broadcasted_iota(I32, (n_rows, 1), 0) % nc

    def shifted(val, step):
        down = jnp.where(rown >= step, pltpu.roll(val, step, axis=0), 0.0)
        up = jnp.where(rown < nc - step, pltpu.roll(val, n_rows - step, axis=0), 0.0)
        return jnp.where(fwd, down, up)

    step = 1
    while step < nc:
        pr = shifted(hr, step)
        pi = shifted(hi, step)
        hr, hi = hr + ar * pr - ai * pi, hi + ar * pi + ai * pr
        ar, ai = ar * ar - ai * ai, 2.0 * ar * ai
        step *= 2
    hin_r = shifted(hr, 1).astype(BF16)
    hin_i = shifted(hi, 1).astype(BF16)
    y = y + jnp.dot(hin_r, v_ref[0, :LANES, :], preferred_element_type=F32)
    y = y + jnp.dot(hin_i, v_ref[0, LANES:, :], preferred_element_type=F32)
    y_ref[0] = y


def _s5_core(ug, mmat, wmat, vmat, amat, nc):
    g, n, k = ug.shape
    blk = lambda r, c: pl.BlockSpec((1, r, c), lambda i: (i, 0, 0))
    return pl.pallas_call(
        functools.partial(_s5_core_kernel, nc=nc),
        grid=(g,),
        in_specs=[blk(n, k), blk(k, k), blk(k, 2 * LANES), blk(2 * LANES, k), blk(8, LANES)],
        out_specs=blk(n, k),
        out_shape=jax.ShapeDtypeStruct((g, n, k), F32),
        compiler_params=_params(("parallel",), 48),
        name="s5_core",
    )(ug, mmat, wmat, vmat, amat)


def _s5_post_kernel(x_ref, g_ref, yg_first_ref, yg_next_ref, d_ref, w_ref, o_ref, y_scr):
    step = pl.program_id(0)
    x = x_ref[...]
    tm, d = x.shape
    nch = tm // SSM_CHUNK

    def unflatten(yg_ref, slot):
        for slab in range(d // LANES):
            for half in range(SSM_CHUNK_LANES // LANES):
                grps = [yg_ref[slab * SSM_SLAB_GROUPS + gi, :, half * LANES:(half + 1) * LANES]
                        for gi in range(SSM_SLAB_GROUPS)]
                for tt, rows in enumerate(_transpose_lane_groups(grps)):
                    y_scr[slot, slab, pl.ds(half * SSM_HALF_STEPS + tt, nch, stride=SSM_CHUNK), :] = rows

    @pl.when(step == 0)
    def _():
        unflatten(yg_first_ref, 0)

    cur = step % 2
    y = jnp.concatenate([y_scr[cur, slab] for slab in range(d // LANES)], axis=1)
    unflatten(yg_next_ref, 1 - cur)
    hf = _rmsnorm(x, g_ref[...])
    z = jax.nn.gelu(y + d_ref[...] * hf)
    vg = jnp.dot(z.astype(BF16), w_ref[...], preferred_element_type=F32)
    o_ref[...] = x + vg[:, :d] * jax.nn.sigmoid(vg[:, d:])


def _s5_post(x2d, g, yg, d_skip, w_glu_bf16, tm=512):
    m, d = x2d.shape
    ngrp = d // SSM_GROUP
    nch = tm // SSM_CHUNK
    last = m // tm - 1
    row = lambda i: (i, 0)
    fixed = lambda i: (0, 0)
    return pl.pallas_call(
        _s5_post_kernel,
        grid=(m // tm,),
        in_specs=[pl.BlockSpec((tm, d), row), pl.BlockSpec((1, d), fixed),
                  pl.BlockSpec((ngrp, nch, SSM_CHUNK_LANES), lambda i: (0, 0, 0)),
                  pl.BlockSpec((ngrp, nch, SSM_CHUNK_LANES), lambda i: (0, jnp.minimum(i + 1, last), 0)),
                  pl.BlockSpec((1, d), fixed),
                  pl.BlockSpec((d, 2 * d), fixed)],
        out_specs=pl.BlockSpec((tm, d), row),
        out_shape=jax.ShapeDtypeStruct((m, d), F32),
        scratch_shapes=[pltpu.VMEM((2, d // LANES, tm, LANES), F32)],
        compiler_params=_params(("arbitrary",), 48),
        name="s5_gelu_glu",
    )(x2d, g.reshape(1, d), yg, yg, d_skip.reshape(1, d), w_glu_bf16)


def _s5_operator_kernel(lam_re_ref, lam_im_ref, ldt_ref, bt_re_ref, bt_im_ref, c_re_ref, c_im_ref,
                        m_ref, w_ref, v_ref, a_ref):
    t = SSM_CHUNK
    c = SSM_GROUP
    lane = lax.broadcasted_iota(I32, (1, LANES), 1)
    fwd = lane < SSM_STATE
    lr = lam_re_ref[0]
    li = lam_im_ref[0]
    dt = jnp.exp(ldt_ref[0])
    mag = jnp.exp(lr * dt)
    ang = li * dt
    lb_re = mag * jnp.cos(ang)
    lb_im = mag * jnp.sin(ang)
    den = lr * lr + li * li
    nr = lb_re - 1.0
    f_re = (nr * lr + lb_im * li) / den
    f_im = (lb_im * lr - nr * li) / den
    bt_re = bt_re_ref[0]
    bt_im = bt_im_ref[0]
    bb_re = f_re * bt_re - f_im * bt_im
    bb_im = f_re * bt_im + f_im * bt_re
    pw = [(jnp.ones((1, LANES), F32), jnp.zeros((1, LANES), F32))]
    for _ in range(t):
        pr, pi = pw[-1]
        pw.append((pr * lb_re - pi * lb_im, pr * lb_im + pi * lb_re))
    c_re = c_re_ref[0]
    c_im = c_im_ref[0]
    cl = [(c_re * pr - c_im * pi, c_re * pi + c_im * pr) for pr, pi in pw]

    def dot_nt(lhs, rhs):
        return lax.dot_general(lhs.astype(BF16), rhs.astype(BF16), (((1,), (1,)), ((), ())),
                               preferred_element_type=F32)

    def kernels(order, keep):
        sre = jnp.concatenate([cl[l][0] for l in order], axis=0)
        sim = jnp.concatenate([cl[l][1] for l in order], axis=0)
        return (dot_nt(jnp.where(keep, bb_re, 0.0), sre) - dot_nt(jnp.where(keep, bb_im, 0.0), sim))

    ktf = kernels(range(t), fwd)
    ktb = kernels(range(t - 1, -1, -1), jnp.logical_not(fwd))
    lane2 = lax.broadcasted_iota(I32, (1, t * c), 1)
    last = (t - 1) * c
    lo = ktb + jnp.where(lane2 >= last, pltpu.roll(ktf, last, axis=1), 0.0)
    hi = jnp.where(lane2 < last, pltpu.roll(ktf, t * c - c, axis=1), 0.0)
    by_lag = jnp.concatenate([lo, hi], axis=1)
    for s in range(t):
        k = (t - 1 - s) * c
        m_ref[0, s * c:(s + 1) * c, :] = by_lag[:, k:k + t * c].astype(BF16)

    vt_rows = []
    for s in range(t):
        pr = jnp.where(fwd, pw[t - 1 - s][0], pw[s][0])
        pi = jnp.where(fwd, pw[t - 1 - s][1], pw[s][1])
        w_ref[0, s * c:(s + 1) * c, :] = jnp.concatenate(
            [pr * bb_re - pi * bb_im, pi * bb_re + pr * bb_im], axis=1).astype(BF16)
        vr = jnp.where(fwd, cl[s + 1][0], cl[t - s][0])
        vi = jnp.where(fwd, cl[s + 1][1], cl[t - s][1])
        vt_rows.append(jnp.concatenate([vr, -vi], axis=1))
    v_ref[0] = jnp.concatenate(vt_rows, axis=0).T.astype(BF16)
    row = lax.broadcasted_iota(I32, (8, 1), 0)
    a_ref[0] = jnp.where(row == 0, pw[t][0], jnp.where(row == 1, pw[t][1], 0.0))


def _s5_operators(lam_re, lam_im, log_dt, b_re, b_im, c_re, c_im):
    g, p = lam_re.shape[1:]
    c = SSM_GROUP
    k = SSM_CHUNK * c
    assert 2 * p == LANES and k == 2 * LANES

    def both(a):
        return jnp.concatenate([a[0], a[1]], axis=-1)

    rowvec = lambda a: both(a).reshape(g, 1, LANES)
    ldt = rowvec(jnp.broadcast_to(log_dt[..., None], (2, g, p)))
    bt = lambda a: both(jnp.swapaxes(a, 2, 3))
    vec = pl.BlockSpec((1, 1, LANES), lambda i: (i, 0, 0))
    mat = pl.BlockSpec((1, c, LANES), lambda i: (i, 0, 0))
    sq = pl.BlockSpec((1, k, k), lambda i: (i, 0, 0))
    return pl.pallas_call(
        _s5_operator_kernel,
        grid=(g,),
        in_specs=[vec, vec, vec, mat, mat, mat, mat],
        out_specs=[sq, sq, sq, pl.BlockSpec((1, 8, LANES), lambda i: (i, 0, 0))],
        out_shape=[jax.ShapeDtypeStruct((g, k, k), BF16)] * 3 + [jax.ShapeDtypeStruct((g, 8, LANES), F32)],
        compiler_params=_params(("parallel",), 32),
        name="s5_operators",
    )(rowvec(lam_re), rowvec(lam_im), ldt, bt(b_re), bt(b_im), both(c_re), both(c_im))


def _s5_mixer(x, mix_g, lam_re, lam_im, log_dt, b_re, b_im, c_re, c_im, d_skip, w_glu):
    b, s, d = x.shape
    t = SSM_CHUNK
    g = d // SSM_GROUP
    nc = s // t
    x2d = x.reshape(b * s, d)
    ug = _norm_flatten(x2d, mix_g)
    mmat, wmat, vmat, amat = _s5_operators(lam_re, lam_im, log_dt, b_re, b_im, c_re, c_im)
    yg = _s5_core(ug, mmat, wmat, vmat, amat, nc)
    out = _s5_post(x2d, mix_g, yg, d_skip, w_glu.astype(BF16))
    return out.reshape(b, s, d)


def kernel(x, mix_norm_even, w_in, conv_w, conv_b, conv_ln_g, conv_ln_b, q_norm, k_norm, w_out, mix_norm_odd, ssm_lam_re, ssm_lam_im, ssm_log_dt, ssm_b_re, ssm_b_im, ssm_c_re, ssm_c_im, ssm_d, w_glu, ffn_norm, w_router, b_router, w_e_gate, w_e_up, w_e_down):
    depth = ffn_norm.shape[0]
    for layer in range(depth):
        i = layer // 2
        if layer % 2 == 0:
            x = _even_mixer(x, mix_norm_even[i], w_in[i], conv_w[i], conv_b[i], conv_ln_g[i],
                            conv_ln_b[i], q_norm[i], k_norm[i], w_out[i])
        else:
            x = _s5_mixer(x, mix_norm_odd[i], ssm_lam_re[i], ssm_lam_im[i], ssm_log_dt[i],
                          ssm_b_re[i], ssm_b_im[i], ssm_c_re[i], ssm_c_im[i], ssm_d[i], w_glu[i])
        x = _moe(x, ffn_norm[layer], w_router[layer], b_router[layer],
                 w_e_gate, w_e_up, w_e_down, layer)
    return x
```

```python
import functools
import math

import jax
import jax.numpy as jnp
from jax import lax
from jax.experimental import pallas as pl
from jax.experimental.pallas import tpu as pltpu

F32 = jnp.float32
BF16 = jnp.bfloat16
I32 = jnp.int32

EPS = 1e-6
NEG_INF = -1e30
LOG2_E = math.log2(math.e)

CONV_CH = 512
CONV_WIDTH = 31
ATT_HEADS = 8
HEAD_DIM = 64
ATT_WIDTH = ATT_HEADS * HEAD_DIM
DILATED = ((128, 1), (512, 4), (2048, 16))
SSM_GROUP = 16
SSM_STATE = 64
N_EXPERTS = 16
CAPACITY_FACTOR = 2
MOE_GATHER_TOKENS = 1024
MOE_GATHER_GROUP = 16
MOE_SCATTER_TOKENS = 256

LANES = 128
F32_MIN_EXP = -126
F32_MANTISSA_BITS = 23
MIB = 1024 * 1024


def _params(sem, vmem_mib):
    return pltpu.CompilerParams(dimension_semantics=sem, vmem_limit_bytes=vmem_mib * MIB)


def _rmsnorm(x, g):
    return x * lax.rsqrt(jnp.mean(x * x, axis=-1, keepdims=True) + EPS) * g


def _norm_matmul_kernel(x_ref, g_ref, w_ref, o_ref):
    y = _rmsnorm(x_ref[...], g_ref[...])
    o_ref[...] = jnp.dot(y.astype(BF16), w_ref[...], preferred_element_type=F32)


def _norm_matmul(x2d, g, w_bf16, tm=512):
    m, d = x2d.shape
    n = w_bf16.shape[1]
    return pl.pallas_call(
        _norm_matmul_kernel,
        grid=(m // tm,),
        in_specs=[pl.BlockSpec((tm, d), lambda i: (i, 0)),
                  pl.BlockSpec((1, d), lambda i: (0, 0)),
                  pl.BlockSpec((d, n), lambda i: (0, 0))],
        out_specs=pl.BlockSpec((tm, n), lambda i: (i, 0)),
        out_shape=jax.ShapeDtypeStruct((m, n), F32),
        compiler_params=_params(("parallel",), 48),
        name="norm_in_proj",
    )(x2d, g.reshape(1, d), w_bf16)


CONV_HALO = 16
CONV_ROWS = 256


def _conv_kernel(val_ref, gate_ref, w_ref, b_ref, o_ref, hp_ref):
    s = val_ref.shape[1]
    zeros = jnp.zeros((CONV_HALO, LANES), F32)
    hp_ref[0:CONV_HALO, :] = zeros
    hp_ref[CONV_HALO + s:2 * CONV_HALO + s, :] = zeros
    hp_ref[CONV_HALO:CONV_HALO + s, :] = val_ref[0] * jax.nn.sigmoid(gate_ref[0])
    first = CONV_HALO - CONV_WIDTH // 2
    for c in range(s // CONV_ROWS):
        r0 = c * CONV_ROWS
        acc = jnp.broadcast_to(b_ref[...], (CONV_ROWS, LANES))
        for k in range(CONV_WIDTH):
            acc = acc + w_ref[k:k + 1, :] * hp_ref[r0 + first + k:r0 + first + k + CONV_ROWS, :]
        o_ref[0, r0:r0 + CONV_ROWS, :] = acc


def _conformer_conv(proj3, conv_w, conv_b):
    b, s, _ = proj3.shape
    nct = CONV_CH // LANES
    return pl.pallas_call(
        _conv_kernel,
        grid=(b, nct),
        in_specs=[pl.BlockSpec((1, s, LANES), lambda i, j: (i, 0, j)),
                  pl.BlockSpec((1, s, LANES), lambda i, j: (i, 0, nct + j)),
                  pl.BlockSpec((CONV_WIDTH, LANES), lambda i, j: (0, j)),
                  pl.BlockSpec((1, LANES), lambda i, j: (0, j))],
        out_specs=pl.BlockSpec((1, s, LANES), lambda i, j: (i, 0, j)),
        out_shape=jax.ShapeDtypeStruct((b, s, CONV_CH), F32),
        scratch_shapes=[pltpu.VMEM((s + 2 * CONV_HALO, LANES), F32)],
        compiler_params=_params(("parallel", "parallel"), 32),
        name="glu_dwconv",
    )(proj3, proj3, conv_w, conv_b.reshape(1, CONV_CH))


ATT_QB = 128
ATT_RAD = 64
ATT_KW = ATT_QB + 2 * ATT_RAD
ATT_PREP_ROWS = 512
ATT_UNROLL = 32


def _attn_kernel(slopes_ref, q_ref, k_ref, v_ref, qg_ref, kg_ref, o_ref,
                 qs_ref, kp_ref, vp_ref, ob_ref, mb_ref, *, seq, pad):
    pair = pl.program_id(1)
    lane = lax.broadcasted_iota(I32, (1, LANES), 1)
    lo = lane < HEAD_DIM
    scale = LOG2_E / math.sqrt(HEAD_DIM)
    spare = (HEAD_DIM, 0)

    zpad = jnp.zeros((pad, LANES), F32)
    for hh in range(2):
        kpad = jnp.broadcast_to(jnp.where(lane == spare[hh], NEG_INF, 0.0), (pad, LANES))
        kp_ref[hh, 0:pad, :] = kpad
        kp_ref[hh, pad + seq:2 * pad + seq, :] = kpad
        vp_ref[hh, 0:pad, :] = zpad
        vp_ref[hh, pad + seq:2 * pad + seq, :] = zpad

    def head_rms(x, g):
        x2 = x * x
        s_lo = jnp.sum(jnp.where(lo, x2, 0.0), axis=-1, keepdims=True)
        s_hi = jnp.sum(jnp.where(lo, 0.0, x2), axis=-1, keepdims=True)
        ms = jnp.where(lo, s_lo, s_hi) * (1.0 / HEAD_DIM)
        return x * lax.rsqrt(ms + EPS) * g

    def prep(i, carry):
        r = pl.multiple_of(i * ATT_PREP_ROWS, ATT_PREP_ROWS)
        rows = pl.ds(r, ATT_PREP_ROWS)
        prows = pl.ds(pad + r, ATT_PREP_ROWS)
        qn = head_rms(q_ref[0, rows, :], qg_ref[...]) * scale
        qs_ref[0, rows, :] = jnp.where(lo, qn, jnp.where(lane == spare[0], 1.0, 0.0))
        qs_ref[1, rows, :] = jnp.where(lo, jnp.where(lane == spare[1], 1.0, 0.0), qn)
        kn = head_rms(k_ref[0, rows, :], kg_ref[...])
        kp_ref[0, prows, :] = jnp.where(lo, kn, 0.0)
        kp_ref[1, prows, :] = jnp.where(lo, 0.0, kn)
        v = v_ref[0, rows, :]
        vp_ref[0, prows, :] = jnp.where(lo, v, jnp.where(lane == spare[0], 1.0, 0.0))
        vp_ref[1, prows, :] = jnp.where(lo, jnp.where(lane == spare[1], 1.0, 0.0), v)
        return carry

    lax.fori_loop(0, seq // ATT_PREP_ROWS, prep, 0)

    rowi = lax.broadcasted_iota(I32, (ATT_QB, ATT_KW), 0)
    coli = lax.broadcasted_iota(I32, (ATT_QB, ATT_KW), 1)
    arel = jnp.abs(coli - ATT_RAD - rowi)

    for hh in range(2):
        slope = slopes_ref[pair * 2 + hh] * LOG2_E
        for bi, (window, dil) in enumerate(DILATED):
            assert window // (2 * dil) == ATT_RAD
            nbc = seq // dil // ATT_QB
            bias = jnp.where(arel <= ATT_RAD, -(slope * dil) * arel.astype(F32), NEG_INF)

            def blk(j, carry, bi=bi, dil=dil, nbc=nbc, bias=bias, hh=hh):
                c = j // nbc
                n = j - c * nbc
                qstart = c + dil * ATT_QB * n
                kstart = pad + c + dil * (ATT_QB * n - ATT_RAD)
                if dil == 1:
                    qsl = pl.ds(qstart, ATT_QB)
                    ksl = pl.ds(kstart, ATT_KW)
                else:
                    qsl = pl.ds(qstart, ATT_QB, stride=dil)
                    ksl = pl.ds(kstart, ATT_KW, stride=dil)
                qb = qs_ref[hh, qsl, :].astype(BF16)
                kw = kp_ref[hh, ksl, :].astype(BF16)
                vw = vp_ref[hh, ksl, :].astype(BF16)
                sc = lax.dot_general(qb, kw, (((1,), (1,)), ((), ())),
                                     preferred_element_type=F32) + bias
                m = jnp.max(sc, axis=-1, keepdims=True)
                p = jnp.exp2(sc - m)
                ob_ref[bi, qsl, :] = jnp.dot(p.astype(BF16), vw, preferred_element_type=F32)
                mb_ref[bi, qsl, :] = jnp.broadcast_to(m, (ATT_QB, LANES))
                return carry

            lax.fori_loop(0, seq // ATT_QB, blk, 0, unroll=ATT_UNROLL)

        def combine(i, carry, hh=hh):
            r = pl.multiple_of(i * ATT_PREP_ROWS, ATT_PREP_ROWS)
            rows = pl.ds(r, ATT_PREP_ROWS)
            nbr = len(DILATED)
            ms = [mb_ref[bi, rows, :] for bi in range(nbr)]
            mx = functools.reduce(jnp.maximum, ms)
            acc = sum(jnp.exp2(ms[bi] - mx) * ob_ref[bi, rows, :] for bi in range(nbr))
            den = jnp.sum(jnp.where(lane == spare[hh], acc, 0.0), axis=-1, keepdims=True)
            res = acc / den
            if hh == 0:
                o_ref[0, rows, :] = res
            else:
                o_ref[0, rows, :] = jnp.where(lo, o_ref[0, rows, :], res)
            return carry

        lax.fori_loop(0, seq // ATT_PREP_ROWS, combine, 0)


def _dilated_attention(proj3, q_norm, k_norm):
    b, s, _ = proj3.shape
    pad = ATT_RAD * max(d for _, d in DILATED)
    npair = ATT_HEADS // 2
    qoff = 2 * CONV_CH // LANES
    slopes = jnp.asarray([2.0 ** (-8.0 * (h + 1) / ATT_HEADS) for h in range(ATT_HEADS)], F32)
    qg = jnp.tile(q_norm.reshape(1, HEAD_DIM), (1, 2))
    kg = jnp.tile(k_norm.reshape(1, HEAD_DIM), (1, 2))
    kernel = functools.partial(_attn_kernel, seq=s, pad=pad)
    return pl.pallas_call(
        kernel,
        grid=(b, npair),
        in_specs=[pl.BlockSpec(memory_space=pltpu.SMEM),
                  pl.BlockSpec((1, s, LANES), lambda i, j: (i, 0, qoff + j)),
                  pl.BlockSpec((1, s, LANES), lambda i, j: (i, 0, qoff + npair + j)),
                  pl.BlockSpec((1, s, LANES), lambda i, j: (i, 0, qoff + 2 * npair + j)),
                  pl.BlockSpec((1, LANES), lambda i, j: (0, 0)),
                  pl.BlockSpec((1, LANES), lambda i, j: (0, 0))],
        out_specs=pl.BlockSpec((1, s, LANES), lambda i, j: (i, 0, j)),
        out_shape=jax.ShapeDtypeStruct((b, s, ATT_WIDTH), F32),
        scratch_shapes=[pltpu.VMEM((2, s, LANES), F32),
                        pltpu.VMEM((2, s + 2 * pad, LANES), F32),
                        pltpu.VMEM((2, s + 2 * pad, LANES), F32),
                        pltpu.VMEM((len(DILATED), s, LANES), F32),
                        pltpu.VMEM((len(DILATED), s, LANES), F32)],
        compiler_params=_params(("parallel", "parallel"), 56),
        name="dilated_attn",
    )(slopes, proj3, proj3, proj3, qg, kg)


def _outproj_kernel(x_ref, conv_ref, att_ref, g_ref, b_ref, w1_ref, w2_ref, o_ref):
    c = conv_ref[...]
    mu = jnp.mean(c, axis=-1, keepdims=True)
    cc = c - mu
    var = jnp.mean(cc * cc, axis=-1, keepdims=True)
    hn = cc * lax.rsqrt(var + EPS) * g_ref[...] + b_ref[...]
    a = hn * jax.nn.sigmoid(hn)
    y = jnp.dot(a.astype(BF16), w1_ref[...], preferred_element_type=F32)
    y = y + jnp.dot(att_ref[...].astype(BF16), w2_ref[...], preferred_element_type=F32)
    o_ref[...] = x_ref[...] + y


def _out_projection(x2d, conv2d, att2d, ln_g, ln_b, w_out_bf16, tm=512):
    m, d = x2d.shape
    row = lambda i: (i, 0)
    fixed = lambda i: (0, 0)
    return pl.pallas_call(
        _outproj_kernel,
        grid=(m // tm,),
        in_specs=[pl.BlockSpec((tm, d), row),
                  pl.BlockSpec((tm, CONV_CH), row),
                  pl.BlockSpec((tm, ATT_WIDTH), row),
                  pl.BlockSpec((1, CONV_CH), fixed),
                  pl.BlockSpec((1, CONV_CH), fixed),
                  pl.BlockSpec((CONV_CH, d), fixed),
                  pl.BlockSpec((ATT_WIDTH, d), lambda i: (1, 0))],
        out_specs=pl.BlockSpec((tm, d), row),
        out_shape=jax.ShapeDtypeStruct((m, d), F32),
        compiler_params=_params(("parallel",), 32),
        name="ln_out_proj",
    )(x2d, conv2d, att2d, ln_g.reshape(1, CONV_CH), ln_b.reshape(1, CONV_CH), w_out_bf16, w_out_bf16)


def _even_mixer(x, mix_g, w_in, conv_w, conv_b, ln_g, ln_b, q_norm, k_norm, w_out):
    b, s, d = x.shape
    x2d = x.reshape(b * s, d)
    proj = _norm_matmul(x2d, mix_g, w_in.astype(BF16))
    proj3 = proj.reshape(b, s, -1)
    conv = _conformer_conv(proj3, conv_w, conv_b)
    att = _dilated_attention(proj3, q_norm, k_norm)
    out = _out_projection(x2d, conv.reshape(b * s, CONV_CH), att.reshape(b * s, ATT_WIDTH),
                          ln_g, ln_b, w_out.astype(BF16))
    return out.reshape(b, s, d)


def _router_kernel(x_ref, g_ref, whi_ref, wlo_ref, br_ref, h_ref, aff_ref):
    h = _rmsnorm(x_ref[0], g_ref[...])
    h_hi = h.astype(BF16)
    h_lo = (h - h_hi.astype(F32)).astype(BF16)
    logits = (jnp.dot(h_hi, whi_ref[...], preferred_element_type=F32)
              + (jnp.dot(h_hi, wlo_ref[...], preferred_element_type=F32)
                 + jnp.dot(h_lo, whi_ref[...], preferred_element_type=F32))) + br_ref[...]
    lt = logits.T[:N_EXPERTS, :]
    mx = jnp.max(lt, axis=0, keepdims=True)
    ex = jnp.exp(lt - mx)
    aff_ref[0] = ex / jnp.sum(ex, axis=0, keepdims=True)
    h_ref[0] = h


def _router(x, g, w_router, b_router, tm=512):
    b, s, d = x.shape
    wr = jnp.zeros((d, LANES), F32).at[:, :N_EXPERTS].set(w_router)
    br = jnp.zeros((1, LANES), F32).at[0, :N_EXPERTS].set(b_router)
    w_hi = wr.astype(BF16)
    w_lo = (wr - w_hi.astype(F32)).astype(BF16)
    return pl.pallas_call(
        _router_kernel,
        grid=(b, s // tm),
        in_specs=[pl.BlockSpec((1, tm, d), lambda i, j: (i, j, 0)),
                  pl.BlockSpec((1, d), lambda i, j: (0, 0)),
                  pl.BlockSpec((d, LANES), lambda i, j: (0, 0)),
                  pl.BlockSpec((d, LANES), lambda i, j: (0, 0)),
                  pl.BlockSpec((1, LANES), lambda i, j: (0, 0))],
        out_specs=[pl.BlockSpec((1, tm, d), lambda i, j: (i, j, 0)),
                   pl.BlockSpec((1, N_EXPERTS, tm), lambda i, j: (i, 0, j))],
        out_shape=[jax.ShapeDtypeStruct((b, s, d), F32),
                   jax.ShapeDtypeStruct((b, N_EXPERTS, s), F32)],
        compiler_params=_params(("parallel", "parallel"), 32),
        name="moe_router",
    )(x, g.reshape(1, d), w_hi, w_lo, br)


def _select_kernel(aff_ref, post_ref, starts_ref, idx_ref, gatet_ref, cs_ref, *, cap):
    a = aff_ref[0]
    s = a.shape[1]
    capf = float(cap)

    def count(mask):
        return jnp.sum(jnp.where(mask, 1.0, 0.0), axis=-1, keepdims=True)

    def enough(cand):
        return count(a >= cand) >= capf

    tiny = float(2.0 ** F32_MIN_EXP)
    thr = jnp.full((N_EXPERTS, 1), tiny, F32)
    normal = enough(thr)
    for j in range(6, -1, -1):
        cand = thr * float(2.0 ** (2 ** j))
        thr = jnp.where(enough(cand), cand, thr)
    step = thr * 0.5
    for _ in range(F32_MANTISSA_BITS):
        cand = thr + step
        thr = jnp.where(enough(cand), cand, thr)
        step = step * 0.5
    thr = jnp.where(normal, thr, 0.0)

    ri = lax.broadcasted_iota(I32, (LANES, LANES), 0)
    ci = lax.broadcasted_iota(I32, (LANES, LANES), 1)
    tri = jnp.where(ri < ci, 1.0, 0.0).astype(BF16)

    lane = lax.broadcasted_iota(I32, (1, LANES), 1)
    chunks_per_tile = MOE_SCATTER_TOKENS // LANES

    def exclusive_cumsum(mask):
        ind = jnp.where(mask, 1.0, 0.0)
        off = jnp.zeros((N_EXPERTS, 1), F32)
        starts = jnp.zeros((N_EXPERTS, LANES), F32)
        for j in range(s // LANES):
            if j % chunks_per_tile == 0:
                starts = jnp.where(lane == j // chunks_per_tile, off, starts)
            xc = ind[:, j * LANES:(j + 1) * LANES]
            cs_ref[:, j * LANES:(j + 1) * LANES] = (
                jnp.dot(xc.astype(BF16), tri, preferred_element_type=F32) + off)
            off = off + jnp.sum(xc, axis=-1, keepdims=True)
        starts = jnp.where(lane == s // MOE_SCATTER_TOKENS, off, starts)
        return cs_ref[...], starts

    gt = a > thr
    eq = a == thr
    need = capf - count(gt)
    tie_rank, _ = exclusive_cumsum(eq)
    sel = gt | (eq & (tie_rank < need))
    slot, starts = exclusive_cumsum(sel)
    slot = jnp.where(sel, slot, -1.0)
    starts_ref[0] = starts.astype(I32)
    fill = jnp.zeros((LANES - N_EXPERTS, s), F32)
    post_ref[0] = jnp.concatenate([slot, fill], axis=0).T

    radix = float(s)
    tok = lax.broadcasted_iota(I32, (N_EXPERTS, s), 1).astype(F32)
    packed = jnp.where(sel, (tok - slot) * radix + tok, -1.0)
    gate = jnp.where(sel, a, 0.0)
    for k in range(s.bit_length() - 1):
        dist = jnp.floor(packed * (1.0 / radix))
        bit = jnp.floor(dist * 0.5 ** k) - 2.0 * jnp.floor(dist * 0.5 ** (k + 1))
        moving = jnp.where(packed >= 0.0, bit, 0.0) == 1.0
        arrived = pltpu.roll(jnp.where(moving, packed, -1.0), s - 2 ** k, axis=1)
        arrived_gate = pltpu.roll(jnp.where(moving, gate, 0.0), s - 2 ** k, axis=1)
        gate = jnp.where(arrived >= 0.0, arrived_gate, jnp.where(moving, 0.0, gate))
        packed = jnp.where(arrived >= 0.0, arrived, jnp.where(moving, -1.0, packed))
    head = packed[:, :cap]
    idx_ref[0] = (head - radix * jnp.floor(head * (1.0 / radix))).astype(I32)
    gatet_ref[0] = jnp.concatenate([gate[:, :cap], jnp.zeros((LANES - N_EXPERTS, cap), F32)], axis=0).T


def _select(aff, cap):
    b, e, s = aff.shape
    assert s // MOE_SCATTER_TOKENS < LANES
    assert s & (s - 1) == 0 and s * s <= 2 ** 24
    blk = pl.BlockSpec((1, e, s), lambda i: (i, 0, 0))
    return pl.pallas_call(
        functools.partial(_select_kernel, cap=cap),
        grid=(b,),
        in_specs=[blk],
        out_specs=[pl.BlockSpec((1, s, LANES), lambda i: (i, 0, 0)),
                   pl.BlockSpec((1, e, LANES), lambda i: (i, 0, 0)),
                   pl.BlockSpec((1, e, cap), lambda i: (i, 0, 0)),
                   pl.BlockSpec((1, cap, LANES), lambda i: (i, 0, 0))],
        out_shape=[jax.ShapeDtypeStruct((b, s, LANES), F32),
                   jax.ShapeDtypeStruct((b, e, LANES), I32),
                   jax.ShapeDtypeStruct((b, e, cap), I32),
                   jax.ShapeDtypeStruct((b, cap, LANES), F32)],
        scratch_shapes=[pltpu.VMEM((e, s), F32)],
        compiler_params=_params(("parallel",), 32),
        name="moe_select",
    )(aff)


def _gather_kernel(idx_ref, h_ref, x_ref, rows_scr, *, cap):
    group = rows_scr.shape[1]

    def copy_group(gidx, carry):
        base = gidx * group
        for r in range(group):
            tok = idx_ref[0, 0, 0, base + r]
            rows_scr[gidx, pl.ds(r, 1), :] = h_ref[0, pl.ds(tok, 1), :]
        return carry

    lax.fori_loop(0, cap // group, copy_group, 0)
    x_ref[0, 0] = rows_scr[...].reshape(cap, rows_scr.shape[2]).astype(BF16)


def _gather_rows(h, idx, cap):
    b, s, d = h.shape
    e = idx.shape[1]
    return pl.pallas_call(
        functools.partial(_gather_kernel, cap=cap),
        grid=(b, e),
        in_specs=[pl.BlockSpec((1, 1, 1, cap), lambda i, j: (i, j, 0, 0), memory_space=pltpu.SMEM),
                  pl.BlockSpec((1, s, d), lambda i, j: (i, 0, 0), pipeline_mode=pl.Buffered(1))],
        out_specs=pl.BlockSpec((1, 1, cap, d), lambda i, j: (i, j, 0, 0)),
        out_shape=jax.ShapeDtypeStruct((b, e, cap, d), BF16),
        scratch_shapes=[pltpu.VMEM((cap // MOE_GATHER_GROUP, MOE_GATHER_GROUP, d), F32)],
        compiler_params=_params(("parallel", "parallel"), 40),
        name="moe_gather",
    )(idx.reshape(b, e, 1, cap), h)


def _expert_kernel(x_ref, gate_ref, wg_ref, wu_ref, wd_ref, y_ref, wg_s, wu_s, wd_s):
    ei = pl.program_id(0)

    @pl.when(pl.program_id(1) == 0)
    def _():
        wg_s[...] = wg_ref[0, 0].astype(BF16)
        wu_s[...] = wu_ref[0, 0].astype(BF16)
        wd_s[...] = wd_ref[0, 0].astype(BF16)

    lane = lax.broadcasted_iota(I32, (1, LANES), 1)
    gate = jnp.sum(jnp.where(lane == ei, gate_ref[0], 0.0), axis=-1, keepdims=True)
    x = x_ref[0, 0]
    g = jnp.dot(x, wg_s[...], preferred_element_type=F32)
    u = jnp.dot(x, wu_s[...], preferred_element_type=F32)
    a = (g * jax.nn.sigmoid(g) * u).astype(BF16)
    y = jnp.dot(a, wd_s[...], preferred_element_type=F32)
    y_ref[0, 0] = (y * gate).astype(BF16)


def _experts(xin, gate_t, w_gate, w_up, w_down, layer):
    b, e, cap, d = xin.shape
    f = w_gate.shape[3]
    return pl.pallas_call(
        _expert_kernel,
        grid=(e, b),
        in_specs=[pl.BlockSpec((1, 1, cap, d), lambda j, i: (i, j, 0, 0)),
                  pl.BlockSpec((1, cap, LANES), lambda j, i: (i, 0, 0)),
                  pl.BlockSpec((1, 1, d, f), lambda j, i: (layer, j, 0, 0)),
                  pl.BlockSpec((1, 1, d, f), lambda j, i: (layer, j, 0, 0)),
                  pl.BlockSpec((1, 1, f, d), lambda j, i: (layer, j, 0, 0))],
        out_specs=pl.BlockSpec((1, 1, cap, d), lambda j, i: (i, j, 0, 0)),
        out_shape=jax.ShapeDtypeStruct((b, e, cap, d), BF16),
        scratch_shapes=[pltpu.VMEM((d, f), BF16), pltpu.VMEM((d, f), BF16), pltpu.VMEM((f, d), BF16)],
        compiler_params=_params(("parallel", "arbitrary"), 56),
        name="moe_experts",
    )(xin, gate_t, w_gate, w_up, w_down)


def _scatter_kernel(starts_ref, x_ref, y_ref, post_ref, o_ref, *, cap):
    bi = pl.program_id(0)
    ti = pl.program_id(1)
    win = MOE_SCATTER_TOKENS
    lane = lax.broadcasted_iota(I32, (1, LANES), 1)
    wslot = lax.broadcasted_iota(I32, (1, win), 1).astype(F32)
    post = post_ref[0]

    def slot_column(e):
        return jnp.sum(jnp.where(lane == e, post, 0.0), axis=-1, keepdims=True)

    def window_start(e):
        first = starts_ref[bi, e, ti]
        off = jnp.minimum(lax.shift_left(lax.shift_right_logical(first, 4), 4), cap - win)
        return pl.multiple_of(off, 16)

    acc = x_ref[0]
    for e in range(N_EXPERTS):
        off = window_start(e)
        onehot = jnp.where(slot_column(e) - off.astype(F32) == wslot, 1.0, 0.0).astype(BF16)
        acc = acc + jnp.dot(onehot, y_ref[0, e, pl.ds(off, win), :], preferred_element_type=F32)
    o_ref[0] = acc

    tail = cap - win
    for e in range(N_EXPERTS):
        covered = window_start(e) + win

        @pl.when(starts_ref[bi, e, ti + 1] > covered)
        def _(e=e, covered=covered):
            pcol = slot_column(e)
            hit = (pcol - float(tail) == wslot) & (pcol >= covered.astype(F32))
            o_ref[0] += jnp.dot(jnp.where(hit, 1.0, 0.0).astype(BF16),
                                y_ref[0, e, tail:cap, :], preferred_element_type=F32)


def _scatter(x, y, post, starts, cap):
    b, s, d = x.shape
    e = y.shape[1]
    tn = MOE_SCATTER_TOKENS
    assert cap % 16 == 0 and tn <= cap <= 2 * tn
    grid_spec = pltpu.PrefetchScalarGridSpec(
        num_scalar_prefetch=1,
        grid=(b, s // tn),
        in_specs=[pl.BlockSpec((1, tn, d), lambda i, j, st: (i, j, 0)),
                  pl.BlockSpec((1, e, cap, d), lambda i, j, st: (i, 0, 0, 0),
                               pipeline_mode=pl.Buffered(1)),
                  pl.BlockSpec((1, tn, LANES), lambda i, j, st: (i, j, 0))],
        out_specs=pl.BlockSpec((1, tn, d), lambda i, j, st: (i, j, 0)),
    )
    return pl.pallas_call(
        functools.partial(_scatter_kernel, cap=cap),
        grid_spec=grid_spec,
        out_shape=jax.ShapeDtypeStruct((b, s, d), F32),
        compiler_params=_params(("parallel", "parallel"), 48),
        name="moe_scatter",
    )(starts, x, y, post)


def _moe(x, ffn_g, w_router, b_router, w_gate, w_up, w_down, layer):
    b, s, d = x.shape
    cap = CAPACITY_FACTOR * s // N_EXPERTS
    h, aff = _router(x, ffn_g, w_router, b_router)
    post, starts, idx, gate_t = _select(aff, cap)
    xin = _gather_rows(h, idx, cap)
    y = _experts(xin, gate_t, w_gate, w_up, w_down, layer)
    return _scatter(x, y, post, starts, cap)


SSM_CHUNK = 16


SSM_SLAB_GROUPS = LANES // SSM_GROUP
SSM_CHUNK_LANES = SSM_CHUNK * SSM_GROUP
SSM_HALF_STEPS = LANES // SSM_GROUP


def _transpose_lane_groups(xs):
    lgrp = lax.broadcasted_iota(I32, (1, LANES), 1) // SSM_GROUP
    xs = list(xs)
    assert len(xs) == SSM_SLAB_GROUPS
    h = SSM_SLAB_GROUPS // 2
    while h:
        upper = (lgrp & h) != 0
        for a0 in range(SSM_SLAB_GROUPS):
            if a0 & h:
                continue
            x0, x1 = xs[a0], xs[a0 + h]
            xs[a0] = jnp.where(upper, pltpu.roll(x1, h * SSM_GROUP, axis=1), x0)
            xs[a0 + h] = jnp.where(upper, x1, pltpu.roll(x0, LANES - h * SSM_GROUP, axis=1))
        h //= 2
    return xs


def _norm_flatten_kernel(x_ref, g_ref, ug_ref, u_scr):
    tm, d = x_ref.shape
    nch = tm // SSM_CHUNK
    u = _rmsnorm(x_ref[...], g_ref[...])
    lgrp = lax.broadcasted_iota(I32, (1, LANES), 1) // SSM_GROUP
    for slab in range(d // LANES):
        u_scr[slab] = u[:, slab * LANES:(slab + 1) * LANES]
        steps = [u_scr[slab, pl.ds(t, nch, stride=SSM_CHUNK), :] for t in range(SSM_CHUNK)]
        for gi in range(SSM_SLAB_GROUPS):
            for half in range(SSM_CHUNK_LANES // LANES):
                acc = None
                for tt in range(SSM_HALF_STEPS):
                    shift = ((tt - gi) * SSM_GROUP) % LANES
                    piece = steps[half * SSM_HALF_STEPS + tt]
                    if shift:
                        piece = pltpu.roll(piece, shift, axis=1)
                    acc = piece if acc is None else jnp.where(lgrp == tt, piece, acc)
                ug_ref[slab * SSM_SLAB_GROUPS + gi, :, half * LANES:(half + 1) * LANES] = acc.astype(BF16)


def _norm_flatten(x2d, g, tm=512):
    m, d = x2d.shape
    ngrp = d // SSM_GROUP
    nch = tm // SSM_CHUNK
    return pl.pallas_call(
        _norm_flatten_kernel,
        grid=(m // tm,),
        in_specs=[pl.BlockSpec((tm, d), lambda i: (i, 0)),
                  pl.BlockSpec((1, d), lambda i: (0, 0))],
        out_specs=pl.BlockSpec((ngrp, nch, SSM_CHUNK_LANES), lambda i: (0, i, 0)),
        out_shape=jax.ShapeDtypeStruct((ngrp, m // SSM_CHUNK, SSM_CHUNK_LANES), BF16),
        scratch_shapes=[pltpu.VMEM((d // LANES, tm, LANES), F32)],
        compiler_params=_params(("parallel",), 32),
        name="s5_norm_flatten",
    )(x2d, g.reshape(1, d))


def _s5_core_kernel(u_ref, m_ref, w_ref, v_ref, a_ref, y_ref, *, nc):
    u = u_ref[0]
    n_rows = u.shape[0]
    y = jnp.dot(u, m_ref[0], preferred_element_type=F32)
    x = jnp.dot(u, w_ref[0], preferred_element_type=F32)
    hr = x[:, :LANES]
    hi = x[:, LANES:]
    ar = a_ref[0, 0:1, :]
    ai = a_ref[0, 1:2, :]
    lane = lax.broadcasted_iota(I32, (1, LANES), 1)
    fwd = lane < SSM_STATE
    rown = lax.broadcasted_iota(I32, (n_rows, 1), 0) % nc

    def shifted(val, step):
        down = jnp.where(rown >= step, pltpu.roll(val, step, axis=0), 0.0)
        up = jnp.where(rown < nc - step, pltpu.roll(val, n_rows - step, axis=0), 0.0)
        return jnp.where(fwd, down, up)

    step = 1
    while step < nc:
        pr = shifted(hr, step)
        pi = shifted(hi, step)
        hr, hi = hr + ar * pr - ai * pi, hi + ar * pi + ai * pr
        ar, ai = ar * ar - ai * ai, 2.0 * ar * ai
        step *= 2
    hin_r = shifted(hr, 1).astype(BF16)
    hin_i = shifted(hi, 1).astype(BF16)
    y = y + jnp.dot(hin_r, v_ref[0, :LANES, :], preferred_element_type=F32)
    y = y + jnp.dot(hin_i, v_ref[0, LANES:, :], preferred_element_type=F32)
    y_ref[0] = y


def _s5_core(ug, mmat, wmat, vmat, amat, nc):
    g, n, k = ug.shape
    blk = lambda r, c: pl.BlockSpec((1, r, c), lambda i: (i, 0, 0))
    return pl.pallas_call(
        functools.partial(_s5_core_kernel, nc=nc),
        grid=(g,),
        in_specs=[blk(n, k), blk(k, k), blk(k, 2 * LANES), blk(2 * LANES, k), blk(8, LANES)],
        out_specs=blk(n, k),
        out_shape=jax.ShapeDtypeStruct((g, n, k), F32),
        compiler_params=_params(("parallel",), 48),
        name="s5_core",
    )(ug, mmat, wmat, vmat, amat)


def _s5_post_kernel(x_ref, g_ref, yg_first_ref, yg_next_ref, d_ref, w_ref, o_ref, y_scr):
    step = pl.program_id(0)
    x = x_ref[...]
    tm, d = x.shape
    nch = tm // SSM_CHUNK

    def unflatten(yg_ref, slot):
        for slab in range(d // LANES):
            for half in range(SSM_CHUNK_LANES // LANES):
                grps = [yg_ref[slab * SSM_SLAB_GROUPS + gi, :, half * LANES:(half + 1) * LANES]
                        for gi in range(SSM_SLAB_GROUPS)]
                for tt, rows in enumerate(_transpose_lane_groups(grps)):
                    y_scr[slot, slab, pl.ds(half * SSM_HALF_STEPS + tt, nch, stride=SSM_CHUNK), :] = rows

    @pl.when(step == 0)
    def _():
        unflatten(yg_first_ref, 0)

    cur = step % 2
    y = jnp.concatenate([y_scr[cur, slab] for slab in range(d // LANES)], axis=1)
    unflatten(yg_next_ref, 1 - cur)
    hf = _rmsnorm(x, g_ref[...])
    z = jax.nn.gelu(y + d_ref[...] * hf)
    vg = jnp.dot(z.astype(BF16), w_ref[...], preferred_element_type=F32)
    o_ref[...] = x + vg[:, :d] * jax.nn.sigmoid(vg[:, d:])


def _s5_post(x2d, g, yg, d_skip, w_glu_bf16, tm=512):
    m, d = x2d.shape
    ngrp = d // SSM_GROUP
    nch = tm // SSM_CHUNK
    last = m // tm - 1
    row = lambda i: (i, 0)
    fixed = lambda i: (0, 0)
    return pl.pallas_call(
        _s5_post_kernel,
        grid=(m // tm,),
        in_specs=[pl.BlockSpec((tm, d), row), pl.BlockSpec((1, d), fixed),
                  pl.BlockSpec((ngrp, nch, SSM_CHUNK_LANES), lambda i: (0, 0, 0)),
                  pl.BlockSpec((ngrp, nch, SSM_CHUNK_LANES), lambda i: (0, jnp.minimum(i + 1, last), 0)),
                  pl.BlockSpec((1, d), fixed),
                  pl.BlockSpec((d, 2 * d), fixed)],
        out_specs=pl.BlockSpec((tm, d), row),
        out_shape=jax.ShapeDtypeStruct((m, d), F32),
        scratch_shapes=[pltpu.VMEM((2, d // LANES, tm, LANES), F32)],
        compiler_params=_params(("arbitrary",), 48),
        name="s5_gelu_glu",
    )(x2d, g.reshape(1, d), yg, yg, d_skip.reshape(1, d), w_glu_bf16)


def _s5_operator_kernel(lam_re_ref, lam_im_ref, ldt_ref, bt_re_ref, bt_im_ref, c_re_ref, c_im_ref,
                        m_ref, w_ref, v_ref, a_ref):
    t = SSM_CHUNK
    c = SSM_GROUP
    lane = lax.broadcasted_iota(I32, (1, LANES), 1)
    fwd = lane < SSM_STATE
    lr = lam_re_ref[0]
    li = lam_im_ref[0]
    dt = jnp.exp(ldt_ref[0])
    mag = jnp.exp(lr * dt)
    ang = li * dt
    lb_re = mag * jnp.cos(ang)
    lb_im = mag * jnp.sin(ang)
    den = lr * lr + li * li
    nr = lb_re - 1.0
    f_re = (nr * lr + lb_im * li) / den
    f_im = (lb_im * lr - nr * li) / den
    bt_re = bt_re_ref[0]
    bt_im = bt_im_ref[0]
    bb_re = f_re * bt_re - f_im * bt_im
    bb_im = f_re * bt_im + f_im * bt_re
    pw = [(jnp.ones((1, LANES), F32), jnp.zeros((1, LANES), F32))]
    for _ in range(t):
        pr, pi = pw[-1]
        pw.append((pr * lb_re - pi * lb_im, pr * lb_im + pi * lb_re))
    c_re = c_re_ref[0]
    c_im = c_im_ref[0]
    cl = [(c_re * pr - c_im * pi, c_re * pi + c_im * pr) for pr, pi in pw]

    def dot_nt(lhs, rhs):
        return lax.dot_general(lhs.astype(BF16), rhs.astype(BF16), (((1,), (1,)), ((), ())),
                               preferred_element_type=F32)

    def kernels(order, keep):
        sre = jnp.concatenate([cl[l][0] for l in order], axis=0)
        sim = jnp.concatenate([cl[l][1] for l in order], axis=0)
        return (dot_nt(jnp.where(keep, bb_re, 0.0), sre) - dot_nt(jnp.where(keep, bb_im, 0.0), sim))

    ktf = kernels(range(t), fwd)
    ktb = kernels(range(t - 1, -1, -1), jnp.logical_not(fwd))
    lane2 = lax.broadcasted_iota(I32, (1, t * c), 1)
    last = (t - 1) * c
    lo = ktb + jnp.where(lane2 >= last, pltpu.roll(ktf, last, axis=1), 0.0)
    hi = jnp.where(lane2 < last, pltpu.roll(ktf, t * c - c, axis=1), 0.0)
    by_lag = jnp.concatenate([lo, hi], axis=1)
    for s in range(t):
        k = (t - 1 - s) * c
        m_ref[0, s * c:(s + 1) * c, :] = by_lag[:, k:k + t * c].astype(BF16)

    vt_rows = []
    for s in range(t):
        pr = jnp.where(fwd, pw[t - 1 - s][0], pw[s][0])
        pi = jnp.where(fwd, pw[t - 1 - s][1], pw[s][1])
        w_ref[0, s * c:(s + 1) * c, :] = jnp.concatenate(
            [pr * bb_re - pi * bb_im, pi * bb_re + pr * bb_im], axis=1).astype(BF16)
        vr = jnp.where(fwd, cl[s + 1][0], cl[t - s][0])
        vi = jnp.where(fwd, cl[s + 1][1], cl[t - s][1])
        vt_rows.append(jnp.concatenate([vr, -vi], axis=1))
    v_ref[0] = jnp.concatenate(vt_rows, axis=0).T.astype(BF16)
    row = lax.broadcasted_iota(I32, (8, 1), 0)
    a_ref[0] = jnp.where(row == 0, pw[t][0], jnp.where(row == 1, pw[t][1], 0.0))


def _s5_operators(lam_re, lam_im, log_dt, b_re, b_im, c_re, c_im):
    g, p = lam_re.shape[1:]
    c = SSM_GROUP
    k = SSM_CHUNK * c
    assert 2 * p == LANES and k == 2 * LANES

    def both(a):
        return jnp.concatenate([a[0], a[1]], axis=-1)

    rowvec = lambda a: both(a).reshape(g, 1, LANES)
    ldt = rowvec(jnp.broadcast_to(log_dt[..., None], (2, g, p)))
    bt = lambda a: both(jnp.swapaxes(a, 2, 3))
    vec = pl.BlockSpec((1, 1, LANES), lambda i: (i, 0, 0))
    mat = pl.BlockSpec((1, c, LANES), lambda i: (i, 0, 0))
    sq = pl.BlockSpec((1, k, k), lambda i: (i, 0, 0))
    return pl.pallas_call(
        _s5_operator_kernel,
        grid=(g,),
        in_specs=[vec, vec, vec, mat, mat, mat, mat],
        out_specs=[sq, sq, sq, pl.BlockSpec((1, 8, LANES), lambda i: (i, 0, 0))],
        out_shape=[jax.ShapeDtypeStruct((g, k, k), BF16)] * 3 + [jax.ShapeDtypeStruct((g, 8, LANES), F32)],
        compiler_params=_params(("parallel",), 32),
        name="s5_operators",
    )(rowvec(lam_re), rowvec(lam_im), ldt, bt(b_re), bt(b_im), both(c_re), both(c_im))


def _s5_mixer(x, mix_g, lam_re, lam_im, log_dt, b_re, b_im, c_re, c_im, d_skip, w_glu):
    b, s, d = x.shape
    t = SSM_CHUNK
    g = d // SSM_GROUP
    nc = s // t
    x2d = x.reshape(b * s, d)
    ug = _norm_flatten(x2d, mix_g)
    mmat, wmat, vmat, amat = _s5_operators(lam_re, lam_im, log_dt, b_re, b_im, c_re, c_im)
    yg = _s5_core(ug, mmat, wmat, vmat, amat, nc)
    out = _s5_post(x2d, mix_g, yg, d_skip, w_glu.astype(BF16))
    return out.reshape(b, s, d)


def kernel(x, mix_norm_even, w_in, conv_w, conv_b, conv_ln_g, conv_ln_b, q_norm, k_norm, w_out, mix_norm_odd, ssm_lam_re, ssm_lam_im, ssm_log_dt, ssm_b_re, ssm_b_im, ssm_c_re, ssm_c_im, ssm_d, w_glu, ffn_norm, w_router, b_router, w_e_gate, w_e_up, w_e_down):
    depth = ffn_norm.shape[0]
    for layer in range(depth):
        i = layer // 2
        if layer % 2 == 0:
            x = _even_mixer(x, mix_norm_even[i], w_in[i], conv_w[i], conv_b[i], conv_ln_g[i],
                            conv_ln_b[i], q_norm[i], k_norm[i], w_out[i])
        else:
            x = _s5_mixer(x, mix_norm_odd[i], ssm_lam_re[i], ssm_lam_im[i], ssm_log_dt[i],
                          ssm_b_re[i], ssm_b_im[i], ssm_c_re[i], ssm_c_im[i], ssm_d[i], w_glu[i])
        x = _moe(x, ffn_norm[layer], w_router[layer], b_router[layer],
                 w_e_gate, w_e_up, w_e_down, layer)
    return x
```

```python
import functools
import math

import jax
import jax.numpy as jnp
from jax import lax
from jax.experimental import pallas as pl
from jax.experimental.pallas import tpu as pltpu

F32 = jnp.float32
BF16 = jnp.bfloat16
I32 = jnp.int32

EPS = 1e-6
NEG_INF = -1e30
LOG2_E = math.log2(math.e)

CONV_CH = 512
CONV_WIDTH = 31
ATT_HEADS = 8
HEAD_DIM = 64
ATT_WIDTH = ATT_HEADS * HEAD_DIM
DILATED = ((128, 1), (512, 4), (2048, 16))
SSM_GROUP = 16
SSM_STATE = 64
N_EXPERTS = 16
CAPACITY_FACTOR = 2
MOE_GATHER_TOKENS = 1024
MOE_GATHER_GROUP = 16
MOE_SCATTER_TOKENS = 256

LANES = 128
F32_MIN_EXP = -126
F32_MANTISSA_BITS = 23
MIB = 1024 * 1024


def _params(sem, vmem_mib):
    return pltpu.CompilerParams(dimension_semantics=sem, vmem_limit_bytes=vmem_mib * MIB)


def _rmsnorm(x, g):
    return x * lax.rsqrt(jnp.mean(x * x, axis=-1, keepdims=True) + EPS) * g


def _norm_matmul_kernel(x_ref, g_ref, w_ref, o_ref):
    y = _rmsnorm(x_ref[...], g_ref[...])
    o_ref[...] = jnp.dot(y.astype(BF16), w_ref[...], preferred_element_type=F32)


def _norm_matmul(x2d, g, w_bf16, tm=512):
    m, d = x2d.shape
    n = w_bf16.shape[1]
    return pl.pallas_call(
        _norm_matmul_kernel,
        grid=(m // tm,),
        in_specs=[pl.BlockSpec((tm, d), lambda i: (i, 0)),
                  pl.BlockSpec((1, d), lambda i: (0, 0)),
                  pl.BlockSpec((d, n), lambda i: (0, 0))],
        out_specs=pl.BlockSpec((tm, n), lambda i: (i, 0)),
        out_shape=jax.ShapeDtypeStruct((m, n), F32),
        compiler_params=_params(("parallel",), 48),
        name="norm_in_proj",
    )(x2d, g.reshape(1, d), w_bf16)


CONV_HALO = 16
CONV_ROWS = 256


def _conv_kernel(val_ref, gate_ref, w_ref, b_ref, o_ref, hp_ref):
    s = val_ref.shape[1]
    zeros = jnp.zeros((CONV_HALO, LANES), F32)
    hp_ref[0:CONV_HALO, :] = zeros
    hp_ref[CONV_HALO + s:2 * CONV_HALO + s, :] = zeros
    hp_ref[CONV_HALO:CONV_HALO + s, :] = val_ref[0] * jax.nn.sigmoid(gate_ref[0])
    first = CONV_HALO - CONV_WIDTH // 2
    for c in range(s // CONV_ROWS):
        r0 = c * CONV_ROWS
        acc = jnp.broadcast_to(b_ref[...], (CONV_ROWS, LANES))
        for k in range(CONV_WIDTH):
            acc = acc + w_ref[k:k + 1, :] * hp_ref[r0 + first + k:r0 + first + k + CONV_ROWS, :]
        o_ref[0, r0:r0 + CONV_ROWS, :] = acc


def _conformer_conv(proj3, conv_w, conv_b):
    b, s, _ = proj3.shape
    nct = CONV_CH // LANES
    return pl.pallas_call(
        _conv_kernel,
        grid=(b, nct),
        in_specs=[pl.BlockSpec((1, s, LANES), lambda i, j: (i, 0, j)),
                  pl.BlockSpec((1, s, LANES), lambda i, j: (i, 0, nct + j)),
                  pl.BlockSpec((CONV_WIDTH, LANES), lambda i, j: (0, j)),
                  pl.BlockSpec((1, LANES), lambda i, j: (0, j))],
        out_specs=pl.BlockSpec((1, s, LANES), lambda i, j: (i, 0, j)),
        out_shape=jax.ShapeDtypeStruct((b, s, CONV_CH), F32),
        scratch_shapes=[pltpu.VMEM((s + 2 * CONV_HALO, LANES), F32)],
        compiler_params=_params(("parallel", "parallel"), 32),
        name="glu_dwconv",
    )(proj3, proj3, conv_w, conv_b.reshape(1, CONV_CH))


ATT_QB = 128
ATT_RAD = 64
ATT_KW = ATT_QB + 2 * ATT_RAD
ATT_PREP_ROWS = 512
ATT_UNROLL = 32


def _attn_kernel(slopes_ref, q_ref, k_ref, v_ref, qg_ref, kg_ref, o_ref,
                 qs_ref, kp_ref, vp_ref, ob_ref, mb_ref, *, seq, pad):
    pair = pl.program_id(1)
    lane = lax.broadcasted_iota(I32, (1, LANES), 1)
    lo = lane < HEAD_DIM
    scale = LOG2_E / math.sqrt(HEAD_DIM)
    spare = (HEAD_DIM, 0)

    zpad = jnp.zeros((pad, LANES), F32)
    for hh in range(2):
        kpad = jnp.broadcast_to(jnp.where(lane == spare[hh], NEG_INF, 0.0), (pad, LANES))
        kp_ref[hh, 0:pad, :] = kpad
        kp_ref[hh, pad + seq:2 * pad + seq, :] = kpad
        vp_ref[hh, 0:pad, :] = zpad
        vp_ref[hh, pad + seq:2 * pad + seq, :] = zpad

    def head_rms(x, g):
        x2 = x * x
        s_lo = jnp.sum(jnp.where(lo, x2, 0.0), axis=-1, keepdims=True)
        s_hi = jnp.sum(jnp.where(lo, 0.0, x2), axis=-1, keepdims=True)
        ms = jnp.where(lo, s_lo, s_hi) * (1.0 / HEAD_DIM)
        return x * lax.rsqrt(ms + EPS) * g

    def prep(i, carry):
        r = pl.multiple_of(i * ATT_PREP_ROWS, ATT_PREP_ROWS)
        rows = pl.ds(r, ATT_PREP_ROWS)
        prows = pl.ds(pad + r, ATT_PREP_ROWS)
        qn = head_rms(q_ref[0, rows, :], qg_ref[...]) * scale
        qs_ref[0, rows, :] = jnp.where(lo, qn, jnp.where(lane == spare[0], 1.0, 0.0))
        qs_ref[1, rows, :] = jnp.where(lo, jnp.where(lane == spare[1], 1.0, 0.0), qn)
        kn = head_rms(k_ref[0, rows, :], kg_ref[...])
        kp_ref[0, prows, :] = jnp.where(lo, kn, 0.0)
        kp_ref[1, prows, :] = jnp.where(lo, 0.0, kn)
        v = v_ref[0, rows, :]
        vp_ref[0, prows, :] = jnp.where(lo, v, jnp.where(lane == spare[0], 1.0, 0.0))
        vp_ref[1, prows, :] = jnp.where(lo, jnp.where(lane == spare[1], 1.0, 0.0), v)
        return carry

    lax.fori_loop(0, seq // ATT_PREP_ROWS, prep, 0)

    rowi = lax.broadcasted_iota(I32, (ATT_QB, ATT_KW), 0)
    coli = lax.broadcasted_iota(I32, (ATT_QB, ATT_KW), 1)
    arel = jnp.abs(coli - ATT_RAD - rowi)

    for hh in range(2):
        slope = slopes_ref[pair * 2 + hh] * LOG2_E
        for bi, (window, dil) in enumerate(DILATED):
            assert window // (2 * dil) == ATT_RAD
            nbc = seq // dil // ATT_QB
            bias = jnp.where(arel <= ATT_RAD, -(slope * dil) * arel.astype(F32), NEG_INF)

            def blk(j, carry, bi=bi, dil=dil, nbc=nbc, bias=bias, hh=hh):
                c = j // nbc
                n = j - c * nbc
                qstart = c + dil * ATT_QB * n
                kstart = pad + c + dil * (ATT_QB * n - ATT_RAD)
                if dil == 1:
                    qsl = pl.ds(qstart, ATT_QB)
                    ksl = pl.ds(kstart, ATT_KW)
                else:
                    qsl = pl.ds(qstart, ATT_QB, stride=dil)
                    ksl = pl.ds(kstart, ATT_KW, stride=dil)
                qb = qs_ref[hh, qsl, :].astype(BF16)
                kw = kp_ref[hh, ksl, :].astype(BF16)
                vw = vp_ref[hh, ksl, :].astype(BF16)
                sc = lax.dot_general(qb, kw, (((1,), (1,)), ((), ())),
                                     preferred_element_type=F32) + bias
                m = jnp.max(sc, axis=-1, keepdims=True)
                p = jnp.exp2(sc - m)
                ob_ref[bi, qsl, :] = jnp.dot(p.astype(BF16), vw, preferred_element_type=F32)
                mb_ref[bi, qsl, :] = jnp.broadcast_to(m, (ATT_QB, LANES))
                return carry

            lax.fori_loop(0, seq // ATT_QB, blk, 0, unroll=ATT_UNROLL)

        def combine(i, carry, hh=hh):
            r = pl.multiple_of(i * ATT_PREP_ROWS, ATT_PREP_ROWS)
            rows = pl.ds(r, ATT_PREP_ROWS)
            nbr = len(DILATED)
            ms = [mb_ref[bi, rows, :] for bi in range(nbr)]
            mx = functools.reduce(jnp.maximum, ms)
            acc = sum(jnp.exp2(ms[bi] - mx) * ob_ref[bi, rows, :] for bi in range(nbr))
            den = jnp.sum(jnp.where(lane == spare[hh], acc, 0.0), axis=-1, keepdims=True)
            res = acc / den
            if hh == 0:
                o_ref[0, rows, :] = res
            else:
                o_ref[0, rows, :] = jnp.where(lo, o_ref[0, rows, :], res)
            return carry

        lax.fori_loop(0, seq // ATT_PREP_ROWS, combine, 0)


def _dilated_attention(proj3, q_norm, k_norm):
    b, s, _ = proj3.shape
    pad = ATT_RAD * max(d for _, d in DILATED)
    npair = ATT_HEADS // 2
    qoff = 2 * CONV_CH // LANES
    slopes = jnp.asarray([2.0 ** (-8.0 * (h + 1) / ATT_HEADS) for h in range(ATT_HEADS)], F32)
    qg = jnp.tile(q_norm.reshape(1, HEAD_DIM), (1, 2))
    kg = jnp.tile(k_norm.reshape(1, HEAD_DIM), (1, 2))
    kernel = functools.partial(_attn_kernel, seq=s, pad=pad)
    return pl.pallas_call(
        kernel,
        grid=(b, npair),
        in_specs=[pl.BlockSpec(memory_space=pltpu.SMEM),
                  pl.BlockSpec((1, s, LANES), lambda i, j: (i, 0, qoff + j)),
                  pl.BlockSpec((1, s, LANES), lambda i, j: (i, 0, qoff + npair + j)),
                  pl.BlockSpec((1, s, LANES), lambda i, j: (i, 0, qoff + 2 * npair + j)),
                  pl.BlockSpec((1, LANES), lambda i, j: (0, 0)),
                  pl.BlockSpec((1, LANES), lambda i, j: (0, 0))],
        out_specs=pl.BlockSpec((1, s, LANES), lambda i, j: (i, 0, j)),
        out_shape=jax.ShapeDtypeStruct((b, s, ATT_WIDTH), F32),
        scratch_shapes=[pltpu.VMEM((2, s, LANES), F32),
                        pltpu.VMEM((2, s + 2 * pad, LANES), F32),
                        pltpu.VMEM((2, s + 2 * pad, LANES), F32),
                        pltpu.VMEM((len(DILATED), s, LANES), F32),
                        pltpu.VMEM((len(DILATED), s, LANES), F32)],
        compiler_params=_params(("parallel", "parallel"), 56),
        name="dilated_attn",
    )(slopes, proj3, proj3, proj3, qg, kg)


def _outproj_kernel(x_ref, conv_ref, att_ref, g_ref, b_ref, w1_ref, w2_ref, o_ref):
    c = conv_ref[...]
    mu = jnp.mean(c, axis=-1, keepdims=True)
    cc = c - mu
    var = jnp.mean(cc * cc, axis=-1, keepdims=True)
    hn = cc * lax.rsqrt(var + EPS) * g_ref[...] + b_ref[...]
    a = hn * jax.nn.sigmoid(hn)
    y = jnp.dot(a.astype(BF16), w1_ref[...], preferred_element_type=F32)
    y = y + jnp.dot(att_ref[...].astype(BF16), w2_ref[...], preferred_element_type=F32)
    o_ref[...] = x_ref[...] + y


def _out_projection(x2d, conv2d, att2d, ln_g, ln_b, w_out_bf16, tm=512):
    m, d = x2d.shape
    row = lambda i: (i, 0)
    fixed = lambda i: (0, 0)
    return pl.pallas_call(
        _outproj_kernel,
        grid=(m // tm,),
        in_specs=[pl.BlockSpec((tm, d), row),
                  pl.BlockSpec((tm, CONV_CH), row),
                  pl.BlockSpec((tm, ATT_WIDTH), row),
                  pl.BlockSpec((1, CONV_CH), fixed),
                  pl.BlockSpec((1, CONV_CH), fixed),
                  pl.BlockSpec((CONV_CH, d), fixed),
                  pl.BlockSpec((ATT_WIDTH, d), lambda i: (1, 0))],
        out_specs=pl.BlockSpec((tm, d), row),
        out_shape=jax.ShapeDtypeStruct((m, d), F32),
        compiler_params=_params(("parallel",), 32),
        name="ln_out_proj",
    )(x2d, conv2d, att2d, ln_g.reshape(1, CONV_CH), ln_b.reshape(1, CONV_CH), w_out_bf16, w_out_bf16)


def _even_mixer(x, mix_g, w_in, conv_w, conv_b, ln_g, ln_b, q_norm, k_norm, w_out):
    b, s, d = x.shape
    x2d = x.reshape(b * s, d)
    proj = _norm_matmul(x2d, mix_g, w_in.astype(BF16))
    proj3 = proj.reshape(b, s, -1)
    conv = _conformer_conv(proj3, conv_w, conv_b)
    att = _dilated_attention(proj3, q_norm, k_norm)
    out = _out_projection(x2d, conv.reshape(b * s, CONV_CH), att.reshape(b * s, ATT_WIDTH),
                          ln_g, ln_b, w_out.astype(BF16))
    return out.reshape(b, s, d)


def _router_kernel(x_ref, g_ref, whi_ref, wlo_ref, br_ref, h_ref, aff_ref):
    h = _rmsnorm(x_ref[0], g_ref[...])
    h_hi = h.astype(BF16)
    h_lo = (h - h_hi.astype(F32)).astype(BF16)
    logits = (jnp.dot(h_hi, whi_ref[...], preferred_element_type=F32)
              + (jnp.dot(h_hi, wlo_ref[...], preferred_element_type=F32)
                 + jnp.dot(h_lo, whi_ref[...], preferred_element_type=F32))) + br_ref[...]
    lt = logits.T[:N_EXPERTS, :]
    mx = jnp.max(lt, axis=0, keepdims=True)
    ex = jnp.exp(lt - mx)
    aff_ref[0] = ex / jnp.sum(ex, axis=0, keepdims=True)
    h_ref[0] = h


def _router(x, g, w_router, b_router, tm=512):
    b, s, d = x.shape
    wr = jnp.zeros((d, LANES), F32).at[:, :N_EXPERTS].set(w_router)
    br = jnp.zeros((1, LANES), F32).at[0, :N_EXPERTS].set(b_router)
    w_hi = wr.astype(BF16)
    w_lo = (wr - w_hi.astype(F32)).astype(BF16)
    return pl.pallas_call(
        _router_kernel,
        grid=(b, s // tm),
        in_specs=[pl.BlockSpec((1, tm, d), lambda i, j: (i, j, 0)),
                  pl.BlockSpec((1, d), lambda i, j: (0, 0)),
                  pl.BlockSpec((d, LANES), lambda i, j: (0, 0)),
                  pl.BlockSpec((d, LANES), lambda i, j: (0, 0)),
                  pl.BlockSpec((1, LANES), lambda i, j: (0, 0))],
        out_specs=[pl.BlockSpec((1, tm, d), lambda i, j: (i, j, 0)),
                   pl.BlockSpec((1, N_EXPERTS, tm), lambda i, j: (i, 0, j))],
        out_shape=[jax.ShapeDtypeStruct((b, s, d), F32),
                   jax.ShapeDtypeStruct((b, N_EXPERTS, s), F32)],
        compiler_params=_params(("parallel", "parallel"), 32),
        name="moe_router",
    )(x, g.reshape(1, d), w_hi, w_lo, br)


def _select_kernel(aff_ref, post_ref, starts_ref, idx_ref, gatet_ref, cs_ref, *, cap):
    a = aff_ref[0]
    s = a.shape[1]
    capf = float(cap)

    def count(mask):
        return jnp.sum(jnp.where(mask, 1.0, 0.0), axis=-1, keepdims=True)

    def enough(cand):
        return count(a >= cand) >= capf

    tiny = float(2.0 ** F32_MIN_EXP)
    thr = jnp.full((N_EXPERTS, 1), tiny, F32)
    normal = enough(thr)
    for j in range(6, -1, -1):
        cand = thr * float(2.0 ** (2 ** j))
        thr = jnp.where(enough(cand), cand, thr)
    step = thr * 0.5
    for _ in range(F32_MANTISSA_BITS):
        cand = thr + step
        thr = jnp.where(enough(cand), cand, thr)
        step = step * 0.5
    thr = jnp.where(normal, thr, 0.0)

    ri = lax.broadcasted_iota(I32, (LANES, LANES), 0)
    ci = lax.broadcasted_iota(I32, (LANES, LANES), 1)
    tri = jnp.where(ri < ci, 1.0, 0.0).astype(BF16)

    lane = lax.broadcasted_iota(I32, (1, LANES), 1)
    chunks_per_tile = MOE_SCATTER_TOKENS // LANES

    def exclusive_cumsum(mask):
        ind = jnp.where(mask, 1.0, 0.0)
        off = jnp.zeros((N_EXPERTS, 1), F32)
        starts = jnp.zeros((N_EXPERTS, LANES), F32)
        for j in range(s // LANES):
            if j % chunks_per_tile == 0:
                starts = jnp.where(lane == j // chunks_per_tile, off, starts)
            xc = ind[:, j * LANES:(j + 1) * LANES]
            cs_ref[:, j * LANES:(j + 1) * LANES] = (
                jnp.dot(xc.astype(BF16), tri, preferred_element_type=F32) + off)
            off = off + jnp.sum(xc, axis=-1, keepdims=True)
        starts = jnp.where(lane == s // MOE_SCATTER_TOKENS, off, starts)
        return cs_ref[...], starts

    gt = a > thr
    eq = a == thr
    need = capf - count(gt)
    tie_rank, _ = exclusive_cumsum(eq)
    sel = gt | (eq & (tie_rank < need))
    slot, starts = exclusive_cumsum(sel)
    slot = jnp.where(sel, slot, -1.0)
    starts_ref[0] = starts.astype(I32)
    fill = jnp.zeros((LANES - N_EXPERTS, s), F32)
    post_ref[0] = jnp.concatenate([slot, fill], axis=0).T

    radix = float(s)
    tok = lax.broadcasted_iota(I32, (N_EXPERTS, s), 1).astype(F32)
    packed = jnp.where(sel, (tok - slot) * radix + tok, -1.0)
    gate = jnp.where(sel, a, 0.0)
    for k in range(s.bit_length() - 1):
        dist = jnp.floor(packed * (1.0 / radix))
        bit = jnp.floor(dist * 0.5 ** k) - 2.0 * jnp.floor(dist * 0.5 ** (k + 1))
        moving = jnp.where(packed >= 0.0, bit, 0.0) == 1.0
        arrived = pltpu.roll(jnp.where(moving, packed, -1.0), s - 2 ** k, axis=1)
        arrived_gate = pltpu.roll(jnp.where(moving, gate, 0.0), s - 2 ** k, axis=1)
        gate = jnp.where(arrived >= 0.0, arrived_gate, jnp.where(moving, 0.0, gate))
        packed = jnp.where(arrived >= 0.0, arrived, jnp.where(moving, -1.0, packed))
    head = packed[:, :cap]
    idx_ref[0] = (head - radix * jnp.floor(head * (1.0 / radix))).astype(I32)
    gatet_ref[0] = jnp.concatenate([gate[:, :cap], jnp.zeros((LANES - N_EXPERTS, cap), F32)], axis=0).T


def _select(aff, cap):
    b, e, s = aff.shape
    assert s // MOE_SCATTER_TOKENS < LANES
    assert s & (s - 1) == 0 and s * s <= 2 ** 24
    blk = pl.BlockSpec((1, e, s), lambda i: (i, 0, 0))
    return pl.pallas_call(
        functools.partial(_select_kernel, cap=cap),
        grid=(b,),
        in_specs=[blk],
        out_specs=[pl.BlockSpec((1, s, LANES), lambda i: (i, 0, 0)),
                   pl.BlockSpec((1, e, LANES), lambda i: (i, 0, 0)),
                   pl.BlockSpec((1, e, cap), lambda i: (i, 0, 0)),
                   pl.BlockSpec((1, cap, LANES), lambda i: (i, 0, 0))],
        out_shape=[jax.ShapeDtypeStruct((b, s, LANES), F32),
                   jax.ShapeDtypeStruct((b, e, LANES), I32),
                   jax.ShapeDtypeStruct((b, e, cap), I32),
                   jax.ShapeDtypeStruct((b, cap, LANES), F32)],
        scratch_shapes=[pltpu.VMEM((e, s), F32)],
        compiler_params=_params(("parallel",), 32),
        name="moe_select",
    )(aff)


def _gather_kernel(idx_ref, h_ref, x_ref, rows_scr, *, cap):
    group = rows_scr.shape[1]

    def copy_group(gidx, carry):
        base = gidx * group
        for r in range(group):
            tok = idx_ref[0, 0, 0, base + r]
            rows_scr[gidx, pl.ds(r, 1), :] = h_ref[0, pl.ds(tok, 1), :]
        return carry

    lax.fori_loop(0, cap // group, copy_group, 0)
    x_ref[0, 0] = rows_scr[...].reshape(cap, rows_scr.shape[2]).astype(BF16)


def _gather_rows(h, idx, cap):
    b, s, d = h.shape
    e = idx.shape[1]
    return pl.pallas_call(
        functools.partial(_gather_kernel, cap=cap),
        grid=(b, e),
        in_specs=[pl.BlockSpec((1, 1, 1, cap), lambda i, j: (i, j, 0, 0), memory_space=pltpu.SMEM),
                  pl.BlockSpec((1, s, d), lambda i, j: (i, 0, 0), pipeline_mode=pl.Buffered(1))],
        out_specs=pl.BlockSpec((1, 1, cap, d), lambda i, j: (i, j, 0, 0)),
        out_shape=jax.ShapeDtypeStruct((b, e, cap, d), BF16),
        scratch_shapes=[pltpu.VMEM((cap // MOE_GATHER_GROUP, MOE_GATHER_GROUP, d), F32)],
        compiler_params=_params(("parallel", "parallel"), 40),
        name="moe_gather",
    )(idx.reshape(b, e, 1, cap), h)


def _expert_kernel(x_ref, gate_ref, wg_ref, wu_ref, wd_ref, y_ref, wg_s, wu_s, wd_s):
    ei = pl.program_id(0)

    @pl.when(pl.program_id(1) == 0)
    def _():
        wg_s[...] = wg_ref[0, 0].astype(BF16)
        wu_s[...] = wu_ref[0, 0].astype(BF16)
        wd_s[...] = wd_ref[0, 0].astype(BF16)

    lane = lax.broadcasted_iota(I32, (1, LANES), 1)
    gate = jnp.sum(jnp.where(lane == ei, gate_ref[0], 0.0), axis=-1, keepdims=True)
    x = x_ref[0, 0]
    hidden = wg_s.shape[1]
    half = hidden // 2
    y = None
    for lo in range(0, hidden, half):
        g = jnp.dot(x, wg_s[:, lo:lo + half], preferred_element_type=F32)
        u = jnp.dot(x, wu_s[:, lo:lo + half], preferred_element_type=F32)
        a = (g * jax.nn.sigmoid(g) * u).astype(BF16)
        part = jnp.dot(a, wd_s[lo:lo + half, :], preferred_element_type=F32)
        y = part if y is None else y + part
    y_ref[0, 0] = (y * gate).astype(BF16)


def _experts(xin, gate_t, w_gate, w_up, w_down, layer):
    b, e, cap, d = xin.shape
    f = w_gate.shape[3]
    return pl.pallas_call(
        _expert_kernel,
        grid=(e, b),
        in_specs=[pl.BlockSpec((1, 1, cap, d), lambda j, i: (i, j, 0, 0)),
                  pl.BlockSpec((1, cap, LANES), lambda j, i: (i, 0, 0)),
                  pl.BlockSpec((1, 1, d, f), lambda j, i: (layer, j, 0, 0)),
                  pl.BlockSpec((1, 1, d, f), lambda j, i: (layer, j, 0, 0)),
                  pl.BlockSpec((1, 1, f, d), lambda j, i: (layer, j, 0, 0))],
        out_specs=pl.BlockSpec((1, 1, cap, d), lambda j, i: (i, j, 0, 0)),
        out_shape=jax.ShapeDtypeStruct((b, e, cap, d), BF16),
        scratch_shapes=[pltpu.VMEM((d, f), BF16), pltpu.VMEM((d, f), BF16), pltpu.VMEM((f, d), BF16)],
        compiler_params=_params(("parallel", "arbitrary"), 56),
        name="moe_experts",
    )(xin, gate_t, w_gate, w_up, w_down)


def _scatter_kernel(starts_ref, x_ref, y_ref, post_ref, o_ref, *, cap):
    bi = pl.program_id(0)
    ti = pl.program_id(1)
    win = MOE_SCATTER_TOKENS
    lane = lax.broadcasted_iota(I32, (1, LANES), 1)
    wslot = lax.broadcasted_iota(I32, (1, win), 1).astype(F32)
    post = post_ref[0]

    def slot_column(e):
        return jnp.sum(jnp.where(lane == e, post, 0.0), axis=-1, keepdims=True)

    def window_start(e):
        first = starts_ref[bi, e, ti]
        off = jnp.minimum(lax.shift_left(lax.shift_right_logical(first, 4), 4), cap - win)
        return pl.multiple_of(off, 16)

    acc = x_ref[0]
    for e in range(N_EXPERTS):
        off = window_start(e)
        onehot = jnp.where(slot_column(e) - off.astype(F32) == wslot, 1.0, 0.0).astype(BF16)
        acc = acc + jnp.dot(onehot, y_ref[0, e, pl.ds(off, win), :], preferred_element_type=F32)
    o_ref[0] = acc

    tail = cap - win
    for e in range(N_EXPERTS):
        covered = window_start(e) + win

        @pl.when(starts_ref[bi, e, ti + 1] > covered)
        def _(e=e, covered=covered):
            pcol = slot_column(e)
            hit = (pcol - float(tail) == wslot) & (pcol >= covered.astype(F32))
            o_ref[0] += jnp.dot(jnp.where(hit, 1.0, 0.0).astype(BF16),
                                y_ref[0, e, tail:cap, :], preferred_element_type=F32)


def _scatter(x, y, post, starts, cap):
    b, s, d = x.shape
    e = y.shape[1]
    tn = MOE_SCATTER_TOKENS
    assert cap % 16 == 0 and tn <= cap <= 2 * tn
    grid_spec = pltpu.PrefetchScalarGridSpec(
        num_scalar_prefetch=1,
        grid=(b, s // tn),
        in_specs=[pl.BlockSpec((1, tn, d), lambda i, j, st: (i, j, 0)),
                  pl.BlockSpec((1, e, cap, d), lambda i, j, st: (i, 0, 0, 0),
                               pipeline_mode=pl.Buffered(1)),
                  pl.BlockSpec((1, tn, LANES), lambda i, j, st: (i, j, 0))],
        out_specs=pl.BlockSpec((1, tn, d), lambda i, j, st: (i, j, 0)),
    )
    return pl.pallas_call(
        functools.partial(_scatter_kernel, cap=cap),
        grid_spec=grid_spec,
        out_shape=jax.ShapeDtypeStruct((b, s, d), F32),
        compiler_params=_params(("parallel", "parallel"), 48),
        name="moe_scatter",
    )(starts, x, y, post)


def _moe(x, ffn_g, w_router, b_router, w_gate, w_up, w_down, layer):
    b, s, d = x.shape
    cap = CAPACITY_FACTOR * s // N_EXPERTS
    h, aff = _router(x, ffn_g, w_router, b_router)
    post, starts, idx, gate_t = _select(aff, cap)
    xin = _gather_rows(h, idx, cap)
    y = _experts(xin, gate_t, w_gate, w_up, w_down, layer)
    return _scatter(x, y, post, starts, cap)


SSM_CHUNK = 16


SSM_SLAB_GROUPS = LANES // SSM_GROUP
SSM_CHUNK_LANES = SSM_CHUNK * SSM_GROUP
SSM_HALF_STEPS = LANES // SSM_GROUP


def _transpose_lane_groups(xs):
    lgrp = lax.broadcasted_iota(I32, (1, LANES), 1) // SSM_GROUP
    xs = list(xs)
    assert len(xs) == SSM_SLAB_GROUPS
    h = SSM_SLAB_GROUPS // 2
    while h:
        upper = (lgrp & h) != 0
        for a0 in range(SSM_SLAB_GROUPS):
            if a0 & h:
                continue
            x0, x1 = xs[a0], xs[a0 + h]
            xs[a0] = jnp.where(upper, pltpu.roll(x1, h * SSM_GROUP, axis=1), x0)
            xs[a0 + h] = jnp.where(upper, x1, pltpu.roll(x0, LANES - h * SSM_GROUP, axis=1))
        h //= 2
    return xs


def _norm_flatten_kernel(x_ref, g_ref, ug_ref, u_scr):
    tm, d = x_ref.shape
    nch = tm // SSM_CHUNK
    u = _rmsnorm(x_ref[...], g_ref[...])
    lgrp = lax.broadcasted_iota(I32, (1, LANES), 1) // SSM_GROUP
    for slab in range(d // LANES):
        u_scr[slab] = u[:, slab * LANES:(slab + 1) * LANES]
        steps = [u_scr[slab, pl.ds(t, nch, stride=SSM_CHUNK), :] for t in range(SSM_CHUNK)]
        for gi in range(SSM_SLAB_GROUPS):
            for half in range(SSM_CHUNK_LANES // LANES):
                acc = None
                for tt in range(SSM_HALF_STEPS):
                    shift = ((tt - gi) * SSM_GROUP) % LANES
                    piece = steps[half * SSM_HALF_STEPS + tt]
                    if shift:
                        piece = pltpu.roll(piece, shift, axis=1)
                    acc = piece if acc is None else jnp.where(lgrp == tt, piece, acc)
                ug_ref[slab * SSM_SLAB_GROUPS + gi, :, half * LANES:(half + 1) * LANES] = acc.astype(BF16)


def _norm_flatten(x2d, g, tm=512):
    m, d = x2d.shape
    ngrp = d // SSM_GROUP
    nch = tm // SSM_CHUNK
    return pl.pallas_call(
        _norm_flatten_kernel,
        grid=(m // tm,),
        in_specs=[pl.BlockSpec((tm, d), lambda i: (i, 0)),
                  pl.BlockSpec((1, d), lambda i: (0, 0))],
        out_specs=pl.BlockSpec((ngrp, nch, SSM_CHUNK_LANES), lambda i: (0, i, 0)),
        out_shape=jax.ShapeDtypeStruct((ngrp, m // SSM_CHUNK, SSM_CHUNK_LANES), BF16),
        scratch_shapes=[pltpu.VMEM((d // LANES, tm, LANES), F32)],
        compiler_params=_params(("parallel",), 32),
        name="s5_norm_flatten",
    )(x2d, g.reshape(1, d))


def _s5_core_kernel(u_ref, m_ref, w_ref, v_ref, a_ref, y_ref, *, nc):
    u = u_ref[0]
    n_rows = u.shape[0]
    y = jnp.dot(u, m_ref[0], preferred_element_type=F32)
    x = jnp.dot(u, w_ref[0], preferred_element_type=F32)
    hr = x[:, :LANES]
    hi = x[:, LANES:]
    ar = a_ref[0, 0:1, :]
    ai = a_ref[0, 1:2, :]
    lane = lax.broadcasted_iota(I32, (1, LANES), 1)
    fwd = lane < SSM_STATE
    rown = lax.broadcasted_iota(I32, (n_rows, 1), 0) % nc

    def shifted(val, step):
        down = jnp.where(rown >= step, pltpu.roll(val, step, axis=0), 0.0)
        up = jnp.where(rown < nc - step, pltpu.roll(val, n_rows - step, axis=0), 0.0)
        return jnp.where(fwd, down, up)

    step = 1
    while step < nc:
        pr = shifted(hr, step)
        pi = shifted(hi, step)
        hr, hi = hr + ar * pr - ai * pi, hi + ar * pi + ai * pr
        ar, ai = ar * ar - ai * ai, 2.0 * ar * ai
        step *= 2
    hin_r = shifted(hr, 1).astype(BF16)
    hin_i = shifted(hi, 1).astype(BF16)
    y = y + jnp.dot(hin_r, v_ref[0, :LANES, :], preferred_element_type=F32)
    y = y + jnp.dot(hin_i, v_ref[0, LANES:, :], preferred_element_type=F32)
    y_ref[0] = y


def _s5_core(ug, mmat, wmat, vmat, amat, nc):
    g, n, k = ug.shape
    blk = lambda r, c: pl.BlockSpec((1, r, c), lambda i: (i, 0, 0))
    return pl.pallas_call(
        functools.partial(_s5_core_kernel, nc=nc),
        grid=(g,),
        in_specs=[blk(n, k), blk(k, k), blk(k, 2 * LANES), blk(2 * LANES, k), blk(8, LANES)],
        out_specs=blk(n, k),
        out_shape=jax.ShapeDtypeStruct((g, n, k), F32),
        compiler_params=_params(("parallel",), 48),
        name="s5_core",
    )(ug, mmat, wmat, vmat, amat)


def _s5_post_kernel(x_ref, g_ref, yg_first_ref, yg_next_ref, d_ref, w_ref, o_ref, y_scr):
    step = pl.program_id(0)
    x = x_ref[...]
    tm, d = x.shape
    nch = tm // SSM_CHUNK

    def unflatten(yg_ref, slot):
        for slab in range(d // LANES):
            for half in range(SSM_CHUNK_LANES // LANES):
                grps = [yg_ref[slab * SSM_SLAB_GROUPS + gi, :, half * LANES:(half + 1) * LANES]
                        for gi in range(SSM_SLAB_GROUPS)]
                for tt, rows in enumerate(_transpose_lane_groups(grps)):
                    y_scr[slot, slab, pl.ds(half * SSM_HALF_STEPS + tt, nch, stride=SSM_CHUNK), :] = rows

    @pl.when(step == 0)
    def _():
        unflatten(yg_first_ref, 0)

    cur = step % 2
    y = jnp.concatenate([y_scr[cur, slab] for slab in range(d // LANES)], axis=1)
    unflatten(yg_next_ref, 1 - cur)
    hf = _rmsnorm(x, g_ref[...])
    z = jax.nn.gelu(y + d_ref[...] * hf)
    vg = jnp.dot(z.astype(BF16), w_ref[...], preferred_element_type=F32)
    o_ref[...] = x + vg[:, :d] * jax.nn.sigmoid(vg[:, d:])


def _s5_post(x2d, g, yg, d_skip, w_glu_bf16, tm=512):
    m, d = x2d.shape
    ngrp = d // SSM_GROUP
    nch = tm // SSM_CHUNK
    last = m // tm - 1
    row = lambda i: (i, 0)
    fixed = lambda i: (0, 0)
    return pl.pallas_call(
        _s5_post_kernel,
        grid=(m // tm,),
        in_specs=[pl.BlockSpec((tm, d), row), pl.BlockSpec((1, d), fixed),
                  pl.BlockSpec((ngrp, nch, SSM_CHUNK_LANES), lambda i: (0, 0, 0)),
                  pl.BlockSpec((ngrp, nch, SSM_CHUNK_LANES), lambda i: (0, jnp.minimum(i + 1, last), 0)),
                  pl.BlockSpec((1, d), fixed),
                  pl.BlockSpec((d, 2 * d), fixed)],
        out_specs=pl.BlockSpec((tm, d), row),
        out_shape=jax.ShapeDtypeStruct((m, d), F32),
        scratch_shapes=[pltpu.VMEM((2, d // LANES, tm, LANES), F32)],
        compiler_params=_params(("arbitrary",), 48),
        name="s5_gelu_glu",
    )(x2d, g.reshape(1, d), yg, yg, d_skip.reshape(1, d), w_glu_bf16)


def _s5_operator_kernel(lam_re_ref, lam_im_ref, ldt_ref, bt_re_ref, bt_im_ref, c_re_ref, c_im_ref,
                        m_ref, w_ref, v_ref, a_ref):
    t = SSM_CHUNK
    c = SSM_GROUP
    lane = lax.broadcasted_iota(I32, (1, LANES), 1)
    fwd = lane < SSM_STATE
    lr = lam_re_ref[0]
    li = lam_im_ref[0]
    dt = jnp.exp(ldt_ref[0])
    mag = jnp.exp(lr * dt)
    ang = li * dt
    lb_re = mag * jnp.cos(ang)
    lb_im = mag * jnp.sin(ang)
    den = lr * lr + li * li
    nr = lb_re - 1.0
    f_re = (nr * lr + lb_im * li) / den
    f_im = (lb_im * lr - nr * li) / den
    bt_re = bt_re_ref[0]
    bt_im = bt_im_ref[0]
    bb_re = f_re * bt_re - f_im * bt_im
    bb_im = f_re * bt_im + f_im * bt_re
    pw = [(jnp.ones((1, LANES), F32), jnp.zeros((1, LANES), F32))]
    for _ in range(t):
        pr, pi = pw[-1]
        pw.append((pr * lb_re - pi * lb_im, pr * lb_im + pi * lb_re))
    c_re = c_re_ref[0]
    c_im = c_im_ref[0]
    cl = [(c_re * pr - c_im * pi, c_re * pi + c_im * pr) for pr, pi in pw]

    def dot_nt(lhs, rhs):
        return lax.dot_general(lhs.astype(BF16), rhs.astype(BF16), (((1,), (1,)), ((), ())),
                               preferred_element_type=F32)

    def kernels(order, keep):
        sre = jnp.concatenate([cl[l][0] for l in order], axis=0)
        sim = jnp.concatenate([cl[l][1] for l in order], axis=0)
        return (dot_nt(jnp.where(keep, bb_re, 0.0), sre) - dot_nt(jnp.where(keep, bb_im, 0.0), sim))

    ktf = kernels(range(t), fwd)
    ktb = kernels(range(t - 1, -1, -1), jnp.logical_not(fwd))
    lane2 = lax.broadcasted_iota(I32, (1, t * c), 1)
    last = (t - 1) * c
    lo = ktb + jnp.where(lane2 >= last, pltpu.roll(ktf, last, axis=1), 0.0)
    hi = jnp.where(lane2 < last, pltpu.roll(ktf, t * c - c, axis=1), 0.0)
    by_lag = jnp.concatenate([lo, hi], axis=1)
    for s in range(t):
        k = (t - 1 - s) * c
        m_ref[0, s * c:(s + 1) * c, :] = by_lag[:, k:k + t * c].astype(BF16)

    vt_rows = []
    for s in range(t):
        pr = jnp.where(fwd, pw[t - 1 - s][0], pw[s][0])
        pi = jnp.where(fwd, pw[t - 1 - s][1], pw[s][1])
        w_ref[0, s * c:(s + 1) * c, :] = jnp.concatenate(
            [pr * bb_re - pi * bb_im, pi * bb_re + pr * bb_im], axis=1).astype(BF16)
        vr = jnp.where(fwd, cl[s + 1][0], cl[t - s][0])
        vi = jnp.where(fwd, cl[s + 1][1], cl[t - s][1])
        vt_rows.append(jnp.concatenate([vr, -vi], axis=1))
    v_ref[0] = jnp.concatenate(vt_rows, axis=0).T.astype(BF16)
    row = lax.broadcasted_iota(I32, (8, 1), 0)
    a_ref[0] = jnp.where(row == 0, pw[t][0], jnp.where(row == 1, pw[t][1], 0.0))


def _s5_operators(lam_re, lam_im, log_dt, b_re, b_im, c_re, c_im):
    g, p = lam_re.shape[1:]
    c = SSM_GROUP
    k = SSM_CHUNK * c
    assert 2 * p == LANES and k == 2 * LANES

    def both(a):
        return jnp.concatenate([a[0], a[1]], axis=-1)

    rowvec = lambda a: both(a).reshape(g, 1, LANES)
    ldt = rowvec(jnp.broadcast_to(log_dt[..., None], (2, g, p)))
    bt = lambda a: both(jnp.swapaxes(a, 2, 3))
    vec = pl.BlockSpec((1, 1, LANES), lambda i: (i, 0, 0))
    mat = pl.BlockSpec((1, c, LANES), lambda i: (i, 0, 0))
    sq = pl.BlockSpec((1, k, k), lambda i: (i, 0, 0))
    return pl.pallas_call(
        _s5_operator_kernel,
        grid=(g,),
        in_specs=[vec, vec, vec, mat, mat, mat, mat],
        out_specs=[sq, sq, sq, pl.BlockSpec((1, 8, LANES), lambda i: (i, 0, 0))],
        out_shape=[jax.ShapeDtypeStruct((g, k, k), BF16)] * 3 + [jax.ShapeDtypeStruct((g, 8, LANES), F32)],
        compiler_params=_params(("parallel",), 32),
        name="s5_operators",
    )(rowvec(lam_re), rowvec(lam_im), ldt, bt(b_re), bt(b_im), both(c_re), both(c_im))


def _s5_mixer(x, mix_g, lam_re, lam_im, log_dt, b_re, b_im, c_re, c_im, d_skip, w_glu):
    b, s, d = x.shape
    t = SSM_CHUNK
    g = d // SSM_GROUP
    nc = s // t
    x2d = x.reshape(b * s, d)
    ug = _norm_flatten(x2d, mix_g)
    mmat, wmat, vmat, amat = _s5_operators(lam_re, lam_im, log_dt, b_re, b_im, c_re, c_im)
    yg = _s5_core(ug, mmat, wmat, vmat, amat, nc)
    out = _s5_post(x2d, mix_g, yg, d_skip, w_glu.astype(BF16))
    return out.reshape(b, s, d)


def kernel(x, mix_norm_even, w_in, conv_w, conv_b, conv_ln_g, conv_ln_b, q_norm, k_norm, w_out, mix_norm_odd, ssm_lam_re, ssm_lam_im, ssm_log_dt, ssm_b_re, ssm_b_im, ssm_c_re, ssm_c_im, ssm_d, w_glu, ffn_norm, w_router, b_router, w_e_gate, w_e_up, w_e_down):
    depth = ffn_norm.shape[0]
    for layer in range(depth):
        i = layer // 2
        if layer % 2 == 0:
            x = _even_mixer(x, mix_norm_even[i], w_in[i], conv_w[i], conv_b[i], conv_ln_g[i],
                            conv_ln_b[i], q_norm[i], k_norm[i], w_out[i])
        else:
            x = _s5_mixer(x, mix_norm_odd[i], ssm_lam_re[i], ssm_lam_im[i], ssm_log_dt[i],
                          ssm_b_re[i], ssm_b_im[i], ssm_c_re[i], ssm_c_im[i], ssm_d[i], w_glu[i])
        x = _moe(x, ffn_norm[layer], w_router[layer], b_router[layer],
                 w_e_gate, w_e_up, w_e_down, layer)
    return x
```
